```python
import math
import jax, jax.numpy as jnp
from jax import lax
import numpy as np

D_MODEL = 1024
BATCH = 4
SEQ = 8192
DEPTH = 1
DEC_BATCH = 16
DEC_SEQ = 32
PAST_LEN = 2048

CHUNK = 64
Q_BLOCK = 128
H_A = 4
DK_A = 64
DV_A = 2 * DK_A
H_R = 4
DK_R = 128
DV_R = 128
D_MIX = H_A * DV_A + H_R * DV_R
D_FF = -(-8 * D_MODEL // (3 * 256)) * 256
N_BUCKETS = 32
MAX_DISTANCE = 128
ROPE_BASE = 10000.0
EPS = 1e-6
NEG_INF = -1e30
COLS = [H_A * 2 * DK_A, H_A * 2 * DK_A, H_A * DV_A, H_R * DK_R, H_R * DK_R, H_R * DV_R, H_R * DV_R]
D_IN = sum(COLS)
SPLITS = [int(s) for s in np.cumsum(COLS)[:-1]]

kernel_name = 'hybrid_diffattn_retention_stream_step'


def lambda_init_of(layer):
    return 0.8 - 0.6 * math.exp(-0.3 * layer)


def rms_norm(x, g):
    xf = x.astype(jnp.float32)
    y = xf * lax.rsqrt(jnp.mean(xf * xf, axis=-1, keepdims=True) + EPS)
    return (y * g.astype(jnp.float32)).astype(x.dtype)


def rms_unit(x):
    xf = x.astype(jnp.float32)
    return xf * lax.rsqrt(jnp.mean(xf * xf, axis=-1, keepdims=True) + EPS)


def t5_bucket(rel):
    nb = N_BUCKETS // 2
    ret = jnp.where(rel > 0, nb, 0)
    n = jnp.abs(rel)
    max_exact = nb // 2
    large = max_exact + (jnp.log(jnp.maximum(n, 1).astype(jnp.float32) / max_exact)
                         / math.log(MAX_DISTANCE / max_exact) * (nb - max_exact)).astype(jnp.int32)
    large = jnp.minimum(large, nb - 1)
    return ret + jnp.where(n < max_exact, n, large)


def rotate(x, pos):
    d = x.shape[-1]
    inv = 1.0 / (ROPE_BASE ** jnp.linspace(0.0, 1.0, d // 2, dtype=jnp.float32))
    ang = pos.astype(jnp.float32)[:, None] * inv[None, :]
    sin = jnp.sin(ang)[None, :, None, :]
    cos = jnp.cos(ang)[None, :, None, :]
    xf = x.astype(jnp.float32)
    x1 = xf[..., 0::2]
    x2 = xf[..., 1::2]
    out = jnp.stack([x1 * cos - x2 * sin, x1 * sin + x2 * cos], axis=-1).reshape(x.shape)
    return out.astype(x.dtype)


def diff_attention(q, k, v, q_pos, k_pos, rel_bias, lam, g_subln, lam_init):
    rel = k_pos[None, :] - q_pos[:, None]
    bias = jnp.transpose(rel_bias[t5_bucket(rel)], (2, 0, 1)).astype(jnp.float32)
    visible = (k_pos[None, :] // CHUNK) <= (q_pos[:, None] // CHUNK)
    logits = jnp.einsum('bqhmd,bkhmd->bhmqk', q, k, preferred_element_type=jnp.float32) * (DK_A ** -0.5)
    logits = jnp.where(visible, logits + bias[None, :, None], NEG_INF)
    p = jax.nn.softmax(logits, axis=-1)
    attn = p[:, :, 0] - lam.astype(jnp.float32) * p[:, :, 1]
    o = jnp.einsum('bhqk,bkhd->bqhd', attn.astype(v.dtype), v)
    return rms_norm(o, g_subln) * (1.0 - lam_init)


def diff_attention_blocked(q, k, v, pos, rel_bias, lam, g_subln, lam_init):
    B, S = q.shape[0], q.shape[1]
    nb = S // Q_BLOCK
    qb = jnp.moveaxis(q.reshape(B, nb, Q_BLOCK, H_A, 2, DK_A), 1, 0)
    pb = pos.reshape(nb, Q_BLOCK)
    ob = lax.map(lambda a: diff_attention(a[0], k, v, a[1], pos, rel_bias, lam, g_subln, lam_init), (qb, pb))
    return jnp.moveaxis(ob, 0, 1).reshape(B, S, H_A, DV_A)


def retention_chunk(S, q, k, v, log_gamma):
    q = q.astype(jnp.float32)
    k = k.astype(jnp.float32)
    v = v.astype(jnp.float32)
    L = q.shape[1]
    i = jnp.arange(L, dtype=jnp.float32)
    diff = i[:, None] - i[None, :]
    decay = jnp.where(diff >= 0, jnp.exp(jnp.maximum(diff, 0.0)[None] * log_gamma[:, None, None]), 0.0)
    scores = jnp.einsum('blhd,bmhd->bhlm', q, k) * decay[None]
    o_inner = jnp.einsum('bhlm,bmhe->blhe', scores, v)
    q_decay = jnp.exp((i + 1.0)[:, None] * log_gamma[None, :])
    o_cross = jnp.einsum('blhd,bhde->blhe', q * q_decay[None, :, :, None], S)
    k_decay = jnp.exp((L - 1.0 - i)[:, None] * log_gamma[None, :])
    S_new = jnp.exp(L * log_gamma)[None, :, None, None] * S + jnp.einsum('blhd,blhe->bhde', k * k_decay[None, :, :, None], v)
    return S_new, o_inner + o_cross


def retention_scan(S0, q, k, v, log_gamma):
    B, S = q.shape[0], q.shape[1]
    nc = S // CHUNK

    def split(t):
        return jnp.moveaxis(t.reshape(B, nc, CHUNK, t.shape[2], t.shape[3]), 1, 0)

    def step(Sc, xs):
        return retention_chunk(Sc, xs[0], xs[1], xs[2], log_gamma)

    S_fin, o = lax.scan(step, S0, (split(q), split(k), split(v)))
    return S_fin, jnp.moveaxis(o, 0, 1).reshape(B, S, H_R, DV_R)


def trunk_layer(x, c, pos, k_hist, v_hist, s_hist, w_ada, b_ada, g_norm1, g_norm2, w_in, g_q, g_k,
                lam_q1, lam_k1, lam_q2, lam_k2, g_subln, w_out, w_ff_gate, w_ff_up, w_ff_down,
                rel_bias, lam_init):
    B, L, _ = x.shape
    mod = jnp.einsum('bd,de->be', jax.nn.silu(c), w_ada) + b_ada
    shift1, scale1, gate1, shift2, scale2, gate2 = jnp.split(mod[:, None, :], 6, axis=-1)
    h = rms_norm(x, g_norm1) * (1.0 + scale1) + shift1
    z = jnp.einsum('bld,de->ble', h, w_in)
    qa, ka, va, qr, kr, vr, gr = jnp.split(z, SPLITS, axis=-1)
    qa = rms_norm(qa.reshape(B, L, H_A, 2, DK_A), g_q)
    ka = rms_norm(ka.reshape(B, L, H_A, 2, DK_A), g_k)
    va = va.reshape(B, L, H_A, DV_A)
    qr = rotate(qr.reshape(B, L, H_R, DK_R), pos)
    kr = rotate(kr.reshape(B, L, H_R, DK_R), pos) * (DK_R ** -0.5)
    vr = vr.reshape(B, L, H_R, DV_R)
    lam = jnp.exp(jnp.sum(lam_q1 * lam_k1)) - jnp.exp(jnp.sum(lam_q2 * lam_k2)) + lam_init
    log_gamma = jnp.log(1.0 - 2.0 ** (-5.0 - jnp.arange(H_R, dtype=jnp.float32)))
    if k_hist is None:
        oa = diff_attention_blocked(qa, ka, va, pos, rel_bias, lam, g_subln, lam_init)
        S0 = jnp.zeros((B, H_R, DK_R, DV_R), jnp.float32)
        S_new, orr = retention_scan(S0, qr, kr, vr, log_gamma)
    else:
        P = k_hist.shape[1]
        k_all = jnp.concatenate([k_hist.astype(ka.dtype), ka], axis=1)
        v_all = jnp.concatenate([v_hist.astype(va.dtype), va], axis=1)
        k_pos = jnp.arange(P + L, dtype=jnp.int32)
        oa = diff_attention(qa, k_all, v_all, pos, k_pos, rel_bias, lam, g_subln, lam_init)
        S_new, orr = retention_chunk(s_hist.astype(jnp.float32), qr, kr, vr, log_gamma)
    orr = rms_unit(orr).astype(x.dtype).reshape(B, L, H_R * DV_R) * jax.nn.silu(gr)
    o = jnp.concatenate([oa.reshape(B, L, H_A * DV_A), orr], axis=-1)
    x = x + gate1 * jnp.einsum('ble,ed->bld', o, w_out)
    h2 = rms_norm(x, g_norm2) * (1.0 + scale2) + shift2
    ff = jax.nn.silu(jnp.einsum('bld,df->blf', h2, w_ff_gate)) * jnp.einsum('bld,df->blf', h2, w_ff_up)
    x = x + gate2 * jnp.einsum('blf,fd->bld', ff, w_ff_down)
    return x, ka, va, S_new.astype(x.dtype)


def setup_inputs(seed: int = 0) -> dict:
    key = jax.random.key(seed)
    ks = jax.random.split(key, 24)
    f32 = jnp.float32

    def nrm(k, shape, s):
        return s * jax.random.normal(k, shape, f32)

    return {
        'x_prompt': nrm(ks[0], (BATCH, SEQ, D_MODEL), 1.0),
        'x_sample': nrm(ks[1], (DEC_BATCH, DEC_SEQ, D_MODEL), 1.0),
        'c_prompt': nrm(ks[2], (BATCH, D_MODEL), 1.0),
        'c_sample': nrm(ks[3], (DEC_BATCH, D_MODEL), 1.0),
        'cache_k': nrm(ks[4], (DEPTH, DEC_BATCH, PAST_LEN, H_A, 2, DK_A), 1.0),
        'cache_v': nrm(ks[5], (DEPTH, DEC_BATCH, PAST_LEN, H_A, DV_A), 1.0),
        'state_ret': nrm(ks[6], (DEPTH, DEC_BATCH, H_R, DK_R, DV_R), 0.5),
        'w_ada': nrm(ks[7], (DEPTH, D_MODEL, 6 * D_MODEL), 0.5 * D_MODEL ** -0.5),
        'b_ada': nrm(ks[8], (DEPTH, 6 * D_MODEL), 0.01),
        'g_norm1': 1.0 + nrm(ks[9], (DEPTH, D_MODEL), 0.05),
        'g_norm2': 1.0 + nrm(ks[10], (DEPTH, D_MODEL), 0.05),
        'w_in': nrm(ks[11], (DEPTH, D_MODEL, D_IN), D_MODEL ** -0.5),
        'g_q': 1.0 + nrm(ks[12], (DEPTH, DK_A), 0.05),
        'g_k': 1.0 + nrm(ks[13], (DEPTH, DK_A), 0.05),
        'lam_q1': nrm(ks[14], (DEPTH, DK_A), 0.1),
        'lam_k1': nrm(ks[15], (DEPTH, DK_A), 0.1),
        'lam_q2': nrm(ks[16], (DEPTH, DK_A), 0.1),
        'lam_k2': nrm(ks[17], (DEPTH, DK_A), 0.1),
        'g_subln': 1.0 + nrm(ks[18], (DEPTH, DV_A), 0.05),
        'w_out': nrm(ks[19], (DEPTH, D_MIX, D_MODEL), D_MIX ** -0.5),
        'w_ff_gate': nrm(ks[20], (DEPTH, D_MODEL, D_FF), D_MODEL ** -0.5),
        'w_ff_up': nrm(ks[21], (DEPTH, D_MODEL, D_FF), D_MODEL ** -0.5),
        'w_ff_down': nrm(ks[22], (DEPTH, D_FF, D_MODEL), D_FF ** -0.5),
        'rel_bias': nrm(ks[23], (N_BUCKETS, H_A), 0.5),
    }


def reference(x_prompt, x_sample, c_prompt, c_sample, cache_k, cache_v, state_ret, w_ada, b_ada,
              g_norm1, g_norm2, w_in, g_q, g_k, lam_q1, lam_k1, lam_q2, lam_k2, g_subln, w_out,
              w_ff_gate, w_ff_up, w_ff_down, rel_bias):
    S = x_prompt.shape[1]
    L = x_sample.shape[1]
    P = cache_k.shape[2]
    pos_prompt = jnp.arange(S, dtype=jnp.int32)
    pos_sample = P + jnp.arange(L, dtype=jnp.int32)
    xp, xs = x_prompt, x_sample
    kp_l, vp_l, sp_l, ks_l, vs_l, ss_l = [], [], [], [], [], []
    for l in range(DEPTH):
        lam_init = lambda_init_of(l)
        w = (w_ada[l], b_ada[l], g_norm1[l], g_norm2[l], w_in[l], g_q[l], g_k[l], lam_q1[l], lam_k1[l],
             lam_q2[l], lam_k2[l], g_subln[l], w_out[l], w_ff_gate[l], w_ff_up[l], w_ff_down[l], rel_bias)
        xp, kp, vp, sp = trunk_layer(xp, c_prompt, pos_prompt, None, None, None, *w, lam_init)
        xs, ksn, vsn, ssn = trunk_layer(xs, c_sample, pos_sample, cache_k[l], cache_v[l], state_ret[l], *w, lam_init)
        kp_l.append(kp)
        vp_l.append(vp)
        sp_l.append(sp)
        ks_l.append(ksn)
        vs_l.append(vsn)
        ss_l.append(ssn)
    return (xp, xs, jnp.stack(kp_l), jnp.stack(vp_l), jnp.stack(sp_l), jnp.stack(ks_l), jnp.stack(vs_l), jnp.stack(ss_l))
```

```python
import functools
import math

import jax
import jax.numpy as jnp
from jax import lax
from jax.experimental import pallas as pl
from jax.experimental.pallas import tpu as pltpu

F32 = jnp.float32
BF16 = jnp.bfloat16

CHUNK = 64
H_A = 4
DK_A = 64
H_R = 4
HEAD_W = 128
SEG_W = H_A * HEAD_W
N_SEG = 7
N_BUCKETS = 32
ROPE_BASE = 10000.0
EPS = 1e-6
NEG_INF = -1e30
BUCKET_THRESHOLDS = (12, 16, 23, 32, 46, 64, 91)
FAR_BUCKET = 15

MXU_W = 256
VMEM_LIMIT = 56 * 1024 * 1024


def _lambda_init(layer):
    return 0.8 - 0.6 * math.exp(-0.3 * layer)


def _params(sem):
    return pltpu.CompilerParams(dimension_semantics=sem, vmem_limit_bytes=VMEM_LIMIT)


def _resident(shape):
    return pl.BlockSpec(shape, lambda *_: (0,) * len(shape), pipeline_mode=pl.Buffered(1))


def _dot(a, b):
    return jnp.dot(a, b, preferred_element_type=F32)


def _dot_nt(a, b):
    return lax.dot_general(a, b, (((1,), (1,)), ((), ())), preferred_element_type=F32)


def _dot_tn(a, b):
    return lax.dot_general(a, b, (((0,), (0,)), ((), ())), preferred_element_type=F32)


def _silu(x):
    return x * jax.nn.sigmoid(x)


def _mod_kernel(c_ref, w_ref, b_ref, o_ref):
    s = _silu(c_ref[...]).astype(BF16)
    o_ref[...] = _dot(s, w_ref[...].astype(BF16)) + b_ref[...]


def _modulation(c_all, w_ada, b_ada):
    n, d = c_all.shape
    e = w_ada.shape[1]
    te = 1536
    assert e % te == 0
    return pl.pallas_call(
        _mod_kernel,
        grid=(e // te,),
        in_specs=[pl.BlockSpec((n, d), lambda j: (0, 0)),
                  pl.BlockSpec((d, te), lambda j: (0, j)),
                  pl.BlockSpec((1, te), lambda j: (0, j))],
        out_specs=pl.BlockSpec((n, te), lambda j: (0, j)),
        out_shape=jax.ShapeDtypeStruct((n, e), F32),
        compiler_params=_params(("parallel",)),
        name="modulation",
    )(c_all, w_ada, b_ada.reshape(1, e))


def _bias_table_kernel(rb_ref, o_ref, *, offset):
    h = pl.program_id(0)
    _, rows, cols = o_ref.shape
    r = lax.broadcasted_iota(jnp.int32, (rows, cols), 0)
    kc = lax.broadcasted_iota(jnp.int32, (rows, cols), 1) - offset
    rel = kc - r
    n = jnp.abs(rel)
    large = jnp.full((rows, cols), N_BUCKETS // 4, jnp.int32)
    for thr in BUCKET_THRESHOLDS:
        large = large + (n >= thr).astype(jnp.int32)
    bucket = jnp.where(rel > 0, N_BUCKETS // 2, 0) + jnp.where(n < N_BUCKETS // 4, n, large)
    val = jnp.zeros((rows, cols), F32)
    for b in range(N_BUCKETS):
        val = jnp.where(bucket == b, rb_ref[b, h], val)
    val = val - rb_ref[FAR_BUCKET, h]
    shift = int(math.log2(CHUNK))
    visible = lax.shift_right_arithmetic(kc, shift) <= lax.shift_right_arithmetic(r, shift)
    o_ref[0] = jnp.where(visible, val, NEG_INF)


def _bias_table(rel_bias, rows, cols, offset):
    assert offset >= BUCKET_THRESHOLDS[-1] and offset % CHUNK == 0
    return pl.pallas_call(
        functools.partial(_bias_table_kernel, offset=offset),
        grid=(H_A,),
        in_specs=[pl.BlockSpec(memory_space=pltpu.SMEM)],
        out_specs=pl.BlockSpec((1, rows, cols), lambda h: (h, 0, 0)),
        out_shape=jax.ShapeDtypeStruct((H_A, rows, cols), F32),
        compiler_params=_params(("parallel",)),
        name="bias_table",
    )(rel_bias)


def _inproj_kernel(x_ref, mod_ref, g1_ref, w_ref, gq_ref, gk_ref, cos_ref, sine_ref, sino_ref,
                   kf_ref, vf_ref, qa_ref, ka_ref, va_ref, qr_ref, kr_ref, vr_ref, sg_ref):
    g, r, d = x_ref.shape
    m = g * r
    x = x_ref[...]
    y = x * lax.rsqrt(jnp.mean(x * x, axis=-1, keepdims=True) + EPS) * g1_ref[...]
    hmod = y * (1.0 + mod_ref[:, 1:2, :]) + mod_ref[:, 0:1, :]
    hb = hmod.astype(BF16).reshape(m, d)

    def seg(j):
        return _dot(hb, w_ref[:, j * SEG_W:(j + 1) * SEG_W])

    def put(ref, val):
        ref[...] = val.reshape(g, r, SEG_W).astype(ref.dtype)

    gi = lax.broadcasted_iota(jnp.int32, (MXU_W, MXU_W), 0) // DK_A
    gj = lax.broadcasted_iota(jnp.int32, (MXU_W, MXU_W), 1) // DK_A
    ones_bd = (gi == gj).astype(BF16)

    def group_rms(z):
        sq = (z * z).astype(BF16)
        ms = jnp.concatenate([_dot(sq[:, c:c + MXU_W], ones_bd) for c in range(0, SEG_W, MXU_W)], axis=1)
        return z * lax.rsqrt(ms * (1.0 / DK_A) + EPS)

    def table(ref):
        return jnp.broadcast_to(ref[...][None], (g, r, HEAD_W)).reshape(m, HEAD_W)

    cos, sine, sino = table(cos_ref), table(sine_ref), table(sino_ref)

    def rotate(z):
        outs = []
        for hh in range(H_R):
            zh = z[:, hh * HEAD_W:(hh + 1) * HEAD_W]
            nxt = pltpu.roll(zh, HEAD_W - 1, axis=1)
            prv = pltpu.roll(zh, 1, axis=1)
            outs.append(zh * cos + nxt * sine + prv * sino)
        return jnp.concatenate(outs, axis=1)

    put(qa_ref, group_rms(seg(0)) * gq_ref[...] * (DK_A ** -0.5))
    ka = group_rms(seg(1)) * gk_ref[...]
    put(kf_ref, ka)
    put(ka_ref, ka)
    va = seg(2)
    put(vf_ref, va)
    put(va_ref, va)
    put(qr_ref, rotate(seg(3)))
    put(kr_ref, rotate(seg(4)) * (HEAD_W ** -0.5))
    put(vr_ref, seg(5))
    put(sg_ref, _silu(seg(6)))


def _inproj(x, mod, g1, w_in, gq_t, gk_t, tables, g_blk, r_blk):
    nb, l, d = x.shape
    assert nb % g_blk == 0 and l % r_blk == 0
    nl = l // r_blk
    tok = lambda w: pl.BlockSpec((g_blk, r_blk, w), lambda i: (i // nl, i % nl, 0))
    tab = pl.BlockSpec((r_blk, HEAD_W), lambda i: (i % nl, 0))
    out = lambda dt: jax.ShapeDtypeStruct((nb, l, SEG_W), dt)
    return pl.pallas_call(
        _inproj_kernel,
        grid=((nb // g_blk) * nl,),
        in_specs=[tok(d),
                  pl.BlockSpec((g_blk, 6, d), lambda i: (i // nl, 0, 0)),
                  _resident((1, d)), _resident(w_in.shape), _resident((1, SEG_W)), _resident((1, SEG_W)),
                  tab, tab, tab],
        out_specs=[tok(SEG_W)] * 9,
        out_shape=[out(F32), out(F32)] + [out(BF16)] * 7,
        compiler_params=_params(("parallel",)),
        name="inproj",
    )(x, mod, g1, w_in, gq_t, gk_t, *tables)


def _stack_maps(q):
    lane = lax.broadcasted_iota(jnp.int32, q.shape, 1)
    zero = jnp.zeros_like(q)
    return jnp.concatenate([jnp.where(lane < DK_A, q, zero), jnp.where(lane >= DK_A, q, zero)], axis=0)


def _lambda(lq1, lk1, lq2, lk2, lam_init):
    s1 = jnp.sum(lq1[...] * lk1[...], axis=-1, keepdims=True)
    s2 = jnp.sum(lq2[...] * lk2[...], axis=-1, keepdims=True)
    return jnp.exp(s1) - jnp.exp(s2) + lam_init


def _diff_finish(acc, l, lam, gs, lam_init):
    t = acc.shape[0] // 2
    o = acc[:t] * (1.0 / l[:t]) - lam * (acc[t:] * (1.0 / l[t:]))
    on = o * lax.rsqrt(jnp.mean(o * o, axis=-1, keepdims=True) + EPS)
    return on * gs * (1.0 - lam_init)


def _attn_prompt_kernel(q_ref, k_ref, v_ref, tbl_ref, lq1, lk1, lq2, lk2, gs_ref, o_ref,
                        m_sc, l_sc, acc_sc, *, lam_init):
    qi = pl.program_id(2)
    t = q_ref.shape[1]
    qq = _stack_maps(q_ref[0])
    m_sc[...] = jnp.full(m_sc.shape, NEG_INF, F32)
    l_sc[...] = jnp.zeros(l_sc.shape, F32)
    acc_sc[...] = jnp.zeros(acc_sc.shape, F32)

    def tile(j, bias):
        start = pl.multiple_of(j * t, t)
        s = _dot_nt(qq, k_ref[0, pl.ds(start, t), :])
        if bias is not None:
            s = s + jnp.concatenate([bias, bias], axis=0)
        m_old = m_sc[...]
        m_new = jnp.maximum(m_old, jnp.max(s, axis=-1, keepdims=True))
        alpha = jnp.exp(m_old - m_new)
        p = jnp.exp(s - m_new)
        l_sc[...] = alpha * l_sc[...] + jnp.sum(p, axis=-1, keepdims=True)
        acc_sc[...] = alpha * acc_sc[...] + _dot(p.astype(BF16), v_ref[0, pl.ds(start, t), :])
        m_sc[...] = m_new

    def far(j, carry):
        tile(j, None)
        return carry

    lax.fori_loop(0, jnp.maximum(qi - 1, 0), far, 0)

    @pl.when(qi >= 1)
    def _():
        tile(qi - 1, tbl_ref[0, :, :t])

    tile(qi, tbl_ref[0, :, t:])
    lam = _lambda(lq1, lk1, lq2, lk2, lam_init)
    o_ref[0] = _diff_finish(acc_sc[...], l_sc[...], lam, gs_ref[...], lam_init).astype(o_ref.dtype)


def _attn_prompt(q, k, v, tbl, lams, gs, lam_init, tq):
    b, s, _ = q.shape
    assert s % tq == 0 and tbl.shape == (H_A, tq, 2 * tq)
    vec = _resident((1, DK_A))
    return pl.pallas_call(
        functools.partial(_attn_prompt_kernel, lam_init=lam_init),
        grid=(b, H_A, s // tq),
        in_specs=[pl.BlockSpec((1, tq, HEAD_W), lambda bi, h, qi: (bi, qi, h)),
                  pl.BlockSpec((1, s, HEAD_W), lambda bi, h, qi: (bi, 0, h)),
                  pl.BlockSpec((1, s, HEAD_W), lambda bi, h, qi: (bi, 0, h)),
                  pl.BlockSpec((1, tq, 2 * tq), lambda bi, h, qi: (h, 0, 0)),
                  vec, vec, vec, vec, _resident((1, HEAD_W))],
        out_specs=pl.BlockSpec((1, tq, HEAD_W), lambda bi, h, qi: (bi, qi, h)),
        out_shape=jax.ShapeDtypeStruct((b, s, SEG_W), BF16),
        scratch_shapes=[pltpu.VMEM((2 * tq, 1), F32), pltpu.VMEM((2 * tq, 1), F32),
                        pltpu.VMEM((2 * tq, HEAD_W), F32)],
        compiler_params=_params(("parallel", "parallel", "arbitrary")),
        name="attn_prompt",
    )(q, k, v, tbl, *lams, gs)


def _log_gamma(h):
    vals = [math.log(1.0 - 2.0 ** (-5.0 - i)) for i in range(H_R)]
    lg = jnp.float32(vals[H_R - 1])
    for i in range(H_R - 2, -1, -1):
        lg = jnp.where(h == i, jnp.float32(vals[i]), lg)
    return lg


def _retention_chunk(q, k, v, state, lg):
    c = q.shape[0]
    ti = lax.broadcasted_iota(jnp.int32, (c, c), 0)
    si = lax.broadcasted_iota(jnp.int32, (c, c), 1)
    dist = (ti - si).astype(F32)
    decay = jnp.where(dist >= 0, jnp.exp(jnp.maximum(dist, 0.0) * lg), 0.0)
    scores = _dot_nt(q, k) * decay
    row = lax.broadcasted_iota(jnp.int32, (c, 1), 0).astype(F32)
    q_dec = (q.astype(F32) * jnp.exp((row + 1.0) * lg)).astype(BF16)
    o = _dot(scores.astype(BF16), v) + _dot(q_dec, state.astype(BF16))
    k_dec = (k.astype(F32) * jnp.exp((c - 1.0 - row) * lg)).astype(BF16)
    new_state = jnp.exp(c * lg) * state + _dot_tn(k_dec, v)
    return o, new_state


def _retention_out(o, sg):
    on = o * lax.rsqrt(jnp.mean(o * o, axis=-1, keepdims=True) + EPS)
    return on * sg.astype(F32)


def _ret_prompt_kernel(q_ref, k_ref, v_ref, sg_ref, o_ref, st_ref, s_sc):
    h = pl.program_id(1)
    ci = pl.program_id(2)

    @pl.when(ci == 0)
    def _():
        s_sc[...] = jnp.zeros(s_sc.shape, F32)

    o, new_state = _retention_chunk(q_ref[0], k_ref[0], v_ref[0], s_sc[...], _log_gamma(h))
    s_sc[...] = new_state
    o_ref[0] = _retention_out(o, sg_ref[0]).astype(o_ref.dtype)

    @pl.when(ci == pl.num_programs(2) - 1)
    def _():
        st_ref[0, 0] = new_state


def _ret_prompt(qr, kr, vr, sg, c_blk):
    b, s, _ = qr.shape
    assert s % c_blk == 0
    blk = pl.BlockSpec((1, c_blk, HEAD_W), lambda bi, h, ci: (bi, ci, h))
    return pl.pallas_call(
        _ret_prompt_kernel,
        grid=(b, H_R, s // c_blk),
        in_specs=[blk, blk, blk, blk],
        out_specs=[blk, pl.BlockSpec((1, 1, HEAD_W, HEAD_W), lambda bi, h, ci: (bi, h, 0, 0))],
        out_shape=[jax.ShapeDtypeStruct((b, s, SEG_W), BF16),
                   jax.ShapeDtypeStruct((b, H_R, HEAD_W, HEAD_W), F32)],
        scratch_shapes=[pltpu.VMEM((HEAD_W, HEAD_W), F32)],
        compiler_params=_params(("parallel", "parallel", "arbitrary")),
        name="ret_prompt",
    )(qr, kr, vr, sg)


def _mix_sample_kernel(qa_ref, ka_ref, va_ref, ck_ref, cv_ref, tbl_ref, lq1, lk1, lq2, lk2, gs_ref,
                       qr_ref, kr_ref, vr_ref, sg_ref, st_ref, oa_ref, or_ref, so_ref, *, lam_init, near):
    p_len = ck_ref.shape[1]
    l = qa_ref.shape[1]
    lam = _lambda(lq1, lk1, lq2, lk2, lam_init)
    for h in range(H_A):
        cols = slice(h * HEAD_W, (h + 1) * HEAD_W)
        qq = _stack_maps(qa_ref[0, :, cols])
        kc = ck_ref[0, :, cols].astype(BF16)
        vc = cv_ref[0, :, cols].astype(BF16)
        bias = tbl_ref[h]
        bias2 = jnp.concatenate([bias, bias], axis=0)
        s_c = _dot_nt(qq, kc)
        s_c = jnp.concatenate([s_c[:, :p_len - near], s_c[:, p_len - near:] + bias2[:, :near]], axis=1)
        s_n = _dot_nt(qq, ka_ref[0, :, cols]) + bias2[:, near:]
        m = jnp.maximum(jnp.max(s_c, axis=-1, keepdims=True), jnp.max(s_n, axis=-1, keepdims=True))
        p_c = jnp.exp(s_c - m)
        p_n = jnp.exp(s_n - m)
        lsum = jnp.sum(p_c, axis=-1, keepdims=True) + jnp.sum(p_n, axis=-1, keepdims=True)
        acc = _dot(p_c.astype(BF16), vc) + _dot(p_n.astype(BF16), va_ref[0, :, cols])
        oa_ref[0, :, cols] = _diff_finish(acc, lsum, lam, gs_ref[...], lam_init).astype(oa_ref.dtype)
    for h in range(H_R):
        cols = slice(h * HEAD_W, (h + 1) * HEAD_W)
        lg = jnp.float32(math.log(1.0 - 2.0 ** (-5.0 - h)))
        o, new_state = _retention_chunk(qr_ref[0, :, cols], kr_ref[0, :, cols], vr_ref[0, :, cols],
                                        st_ref[0, h], lg)
        or_ref[0, :, cols] = _retention_out(o, sg_ref[0, :, cols]).astype(or_ref.dtype)
        so_ref[0, h] = new_state


def _mix_sample(qa, ka, va, cache_k, cache_v, tbl, lams, gs, qr, kr, vr, sg, state, lam_init, near):
    b, l, _ = qa.shape
    p_len = cache_k.shape[1]
    assert p_len % CHUNK == 0 and l <= CHUNK and p_len >= near
    tok = pl.BlockSpec((1, l, SEG_W), lambda bi: (bi, 0, 0))
    cache = pl.BlockSpec((1, p_len, SEG_W), lambda bi: (bi, 0, 0))
    st = pl.BlockSpec((1, H_R, HEAD_W, HEAD_W), lambda bi: (bi, 0, 0, 0))
    vec = _resident((1, DK_A))
    return pl.pallas_call(
        functools.partial(_mix_sample_kernel, lam_init=lam_init, near=near),
        grid=(b,),
        in_specs=[tok, tok, tok, cache, cache, _resident(tbl.shape), vec, vec, vec, vec,
                  _resident((1, HEAD_W)), tok, tok, tok, tok, st],
        out_specs=[tok, tok, st],
        out_shape=[jax.ShapeDtypeStruct((b, l, SEG_W), BF16), jax.ShapeDtypeStruct((b, l, SEG_W), BF16),
                   jax.ShapeDtypeStruct((b, H_R, HEAD_W, HEAD_W), F32)],
        compiler_params=_params(("parallel",)),
        name="mix_sample",
    )(qa, ka, va, cache_k, cache_v, tbl, *lams, gs, qr, kr, vr, sg, state)


def _ffn_kernel(x_ref, oa_ref, or_ref, mod_ref, g2_ref, wo_ref, wg_ref, wu_ref, wd_ref, y_ref, *, ff_blk):
    g, r, d = x_ref.shape
    m = g * r
    f = wg_ref.shape[1]
    proj = (_dot(oa_ref[...].reshape(m, SEG_W), wo_ref[:SEG_W, :])
            + _dot(or_ref[...].reshape(m, SEG_W), wo_ref[SEG_W:, :]))
    x1 = x_ref[...] + mod_ref[:, 2:3, :] * proj.reshape(g, r, d)
    y = x1 * lax.rsqrt(jnp.mean(x1 * x1, axis=-1, keepdims=True) + EPS) * g2_ref[...]
    h2 = (y * (1.0 + mod_ref[:, 4:5, :]) + mod_ref[:, 3:4, :]).astype(BF16).reshape(m, d)
    acc = jnp.zeros((m, d), F32)
    for c0 in range(0, f, ff_blk):
        c1 = min(c0 + ff_blk, f)
        ff = (_silu(_dot(h2, wg_ref[:, c0:c1])) * _dot(h2, wu_ref[:, c0:c1])).astype(BF16)
        acc = acc + _dot(ff, wd_ref[c0:c1, :])
    y_ref[...] = x1 + mod_ref[:, 5:6, :] * acc.reshape(g, r, d)


def _ffn(x, oa, orr, mod, g2, wo, wg, wu, wd, g_blk, r_blk, ff_blk):
    nb, l, d = x.shape
    assert nb % g_blk == 0 and l % r_blk == 0
    nl = l // r_blk
    tok = lambda w: pl.BlockSpec((g_blk, r_blk, w), lambda i: (i // nl, i % nl, 0))
    return pl.pallas_call(
        functools.partial(_ffn_kernel, ff_blk=ff_blk),
        grid=((nb // g_blk) * nl,),
        in_specs=[tok(d), tok(SEG_W), tok(SEG_W),
                  pl.BlockSpec((g_blk, 6, d), lambda i: (i // nl, 0, 0)),
                  _resident((1, d)), _resident(wo.shape), _resident(wg.shape), _resident(wu.shape),
                  _resident(wd.shape)],
        out_specs=tok(d),
        out_shape=jax.ShapeDtypeStruct((nb, l, d), F32),
        compiler_params=_params(("parallel",)),
        name="outproj_ffn",
    )(x, oa, orr, mod, g2, wo, wg, wu, wd)


def _rope_tables(pos):
    inv = 1.0 / (ROPE_BASE ** jnp.linspace(0.0, 1.0, HEAD_W // 2, dtype=F32))
    ang = pos.astype(F32)[:, None] * inv[None, :]
    sin = jnp.repeat(jnp.sin(ang), 2, axis=1)
    cos = jnp.repeat(jnp.cos(ang), 2, axis=1)
    even = (jnp.arange(HEAD_W) % 2 == 0)[None, :]
    return cos, jnp.where(even, -sin, 0.0), jnp.where(even, 0.0, sin)


def _tile_rows(n, target):
    t = min(n, target)
    while n % t:
        t //= 2
    return t


def kernel(x_prompt, x_sample, c_prompt, c_sample, cache_k, cache_v, state_ret, w_ada, b_ada, g_norm1, g_norm2, w_in, g_q, g_k, lam_q1, lam_k1, lam_q2, lam_k2, g_subln, w_out, w_ff_gate, w_ff_up, w_ff_down, rel_bias):
    depth = w_ada.shape[0]
    b, s, d = x_prompt.shape
    bs, l, _ = x_sample.shape
    p_len = cache_k.shape[2]
    assert w_in.shape[2] == N_SEG * SEG_W and d == 2 * SEG_W

    tm = _tile_rows(s, 512)
    tq = _tile_rows(s, 256)
    c_blk = _tile_rows(s, 256)
    near = 128
    assert tq % CHUNK == 0 and tq >= BUCKET_THRESHOLDS[-1]

    tab_p = _rope_tables(jnp.arange(s, dtype=jnp.int32))
    tab_s = _rope_tables(p_len + jnp.arange(l, dtype=jnp.int32))
    tbl_p = _bias_table(rel_bias, tq, 2 * tq, tq)
    tbl_s = _bias_table(rel_bias, l, near + l, near)
    c_all = jnp.concatenate([c_prompt, c_sample], axis=0)

    xp, xs = x_prompt, x_sample
    outs = [[] for _ in range(6)]
    for layer in range(depth):
        lam_init = _lambda_init(layer)
        mod = _modulation(c_all, w_ada[layer], b_ada[layer]).reshape(b + bs, 6, d)
        mod_p, mod_s = mod[:b], mod[b:]
        g1 = g_norm1[layer].reshape(1, d)
        g2 = g_norm2[layer].reshape(1, d)
        gq_t = jnp.tile(g_q[layer], SEG_W // DK_A).reshape(1, SEG_W)
        gk_t = jnp.tile(g_k[layer], SEG_W // DK_A).reshape(1, SEG_W)
        lams = [v[layer].reshape(1, DK_A) for v in (lam_q1, lam_k1, lam_q2, lam_k2)]
        gs = g_subln[layer].reshape(1, HEAD_W)
        wi = w_in[layer].astype(BF16)
        wo = w_out[layer].astype(BF16)
        wg = w_ff_gate[layer].astype(BF16)
        wu = w_ff_up[layer].astype(BF16)
        wd = w_ff_down[layer].astype(BF16)
        ff_blk = 1024

        kf, vf, qa, ka, va, qr, kr, vr, sg = _inproj(xp, mod_p, g1, wi, gq_t, gk_t, tab_p, 1, tm)
        oa = _attn_prompt(qa, ka, va, tbl_p, lams, gs, lam_init, tq)
        orr, st_p = _ret_prompt(qr, kr, vr, sg, c_blk)
        xp = _ffn(xp, oa, orr, mod_p, g2, wo, wg, wu, wd, 1, tm, ff_blk)

        kfs, vfs, qas, kas, vas, qrs, krs, vrs, sgs = _inproj(xs, mod_s, g1, wi, gq_t, gk_t, tab_s, bs, l)
        ck = cache_k[layer].reshape(bs, p_len, SEG_W)
        cv = cache_v[layer].reshape(bs, p_len, SEG_W)
        oas, ors, st_s = _mix_sample(qas, kas, vas, ck, cv, tbl_s, lams, gs, qrs, krs, vrs, sgs,
                                     state_ret[layer], lam_init, near)
        xs = _ffn(xs, oas, ors, mod_s, g2, wo, wg, wu, wd, bs, l, ff_blk)

        for lst, val in zip(outs, (kf.reshape(b, s, H_A, 2, DK_A), vf.reshape(b, s, H_A, HEAD_W), st_p,
                                   kfs.reshape(bs, l, H_A, 2, DK_A), vfs.reshape(bs, l, H_A, HEAD_W), st_s)):
            lst.append(val)
    return (xp, xs) + tuple(jnp.stack(o) for o in outs)
```

```python
import functools
import math

import jax
import jax.numpy as jnp
from jax import lax
from jax.experimental import pallas as pl
from jax.experimental.pallas import tpu as pltpu

F32 = jnp.float32
BF16 = jnp.bfloat16

CHUNK = 64
H_A = 4
DK_A = 64
H_R = 4
HEAD_W = 128
SEG_W = H_A * HEAD_W
N_SEG = 7
N_BUCKETS = 32
ROPE_BASE = 10000.0
EPS = 1e-6
NEG_INF = -1e30
LOG2E = math.log2(math.e)
BUCKET_THRESHOLDS = (12, 16, 23, 32, 46, 64, 91)
FAR_BUCKET = 15

MXU_W = 256
VMEM_LIMIT = 56 * 1024 * 1024


def _lambda_init(layer):
    return 0.8 - 0.6 * math.exp(-0.3 * layer)


def _params(sem):
    return pltpu.CompilerParams(dimension_semantics=sem, vmem_limit_bytes=VMEM_LIMIT)


def _resident(shape):
    return pl.BlockSpec(shape, lambda *_: (0,) * len(shape), pipeline_mode=pl.Buffered(1))


def _dot(a, b):
    return jnp.dot(a, b, preferred_element_type=F32)


def _dot_nt(a, b):
    return lax.dot_general(a, b, (((1,), (1,)), ((), ())), preferred_element_type=F32)


def _dot_tn(a, b):
    return lax.dot_general(a, b, (((0,), (0,)), ((), ())), preferred_element_type=F32)


def _silu(x):
    return x * jax.nn.sigmoid(x)


def _mod_kernel(c_ref, w_ref, b_ref, o_ref):
    s = _silu(c_ref[...]).astype(BF16)
    o_ref[...] = _dot(s, w_ref[...].astype(BF16)) + b_ref[...]


def _modulation(c_all, w_ada, b_ada):
    n, d = c_all.shape
    e = w_ada.shape[1]
    te = 1536
    assert e % te == 0
    return pl.pallas_call(
        _mod_kernel,
        grid=(e // te,),
        in_specs=[pl.BlockSpec((n, d), lambda j: (0, 0)),
                  pl.BlockSpec((d, te), lambda j: (0, j)),
                  pl.BlockSpec((1, te), lambda j: (0, j))],
        out_specs=pl.BlockSpec((n, te), lambda j: (0, j)),
        out_shape=jax.ShapeDtypeStruct((n, e), F32),
        compiler_params=_params(("parallel",)),
        name="modulation",
    )(c_all, w_ada, b_ada.reshape(1, e))


def _bias_table_kernel(rb_ref, o_ref, *, offset):
    h = pl.program_id(0)
    _, rows, cols = o_ref.shape
    r = lax.broadcasted_iota(jnp.int32, (rows, cols), 0)
    kc = lax.broadcasted_iota(jnp.int32, (rows, cols), 1) - offset
    rel = kc - r
    n = jnp.abs(rel)
    large = jnp.full((rows, cols), N_BUCKETS // 4, jnp.int32)
    for thr in BUCKET_THRESHOLDS:
        large = large + (n >= thr).astype(jnp.int32)
    bucket = jnp.where(rel > 0, N_BUCKETS // 2, 0) + jnp.where(n < N_BUCKETS // 4, n, large)
    val = jnp.zeros((rows, cols), F32)
    for b in range(N_BUCKETS):
        val = jnp.where(bucket == b, rb_ref[b, h], val)
    val = (val - rb_ref[FAR_BUCKET, h]) * LOG2E
    shift = int(math.log2(CHUNK))
    visible = lax.shift_right_arithmetic(kc, shift) <= lax.shift_right_arithmetic(r, shift)
    o_ref[0] = jnp.where(visible, val, NEG_INF)


def _bias_table(rel_bias, rows, cols, offset):
    assert offset >= BUCKET_THRESHOLDS[-1] and offset % CHUNK == 0
    return pl.pallas_call(
        functools.partial(_bias_table_kernel, offset=offset),
        grid=(H_A,),
        in_specs=[pl.BlockSpec(memory_space=pltpu.SMEM)],
        out_specs=pl.BlockSpec((1, rows, cols), lambda h: (h, 0, 0)),
        out_shape=jax.ShapeDtypeStruct((H_A, rows, cols), F32),
        compiler_params=_params(("parallel",)),
        name="bias_table",
    )(rel_bias)


def _inproj_kernel(x_ref, mod_ref, g1_ref, w_ref, gq_ref, gk_ref, cos_ref, sine_ref, sino_ref,
                   kf_ref, vf_ref, qa_ref, ka_ref, va_ref, qr_ref, kr_ref, vr_ref, sg_ref):
    g, r, d = x_ref.shape
    m = g * r
    x = x_ref[...]
    y = x * lax.rsqrt(jnp.mean(x * x, axis=-1, keepdims=True) + EPS) * g1_ref[...]
    hmod = y * (1.0 + mod_ref[:, 1:2, :]) + mod_ref[:, 0:1, :]
    hb = hmod.astype(BF16).reshape(m, d)

    def seg(j):
        return _dot(hb, w_ref[:, j * SEG_W:(j + 1) * SEG_W])

    def put(ref, val):
        ref[...] = val.reshape(g, r, SEG_W).astype(ref.dtype)

    gi = lax.broadcasted_iota(jnp.int32, (MXU_W, MXU_W), 0) // DK_A
    gj = lax.broadcasted_iota(jnp.int32, (MXU_W, MXU_W), 1) // DK_A
    ones_bd = (gi == gj).astype(BF16)

    def group_rms(z):
        sq = (z * z).astype(BF16)
        ms = jnp.concatenate([_dot(sq[:, c:c + MXU_W], ones_bd) for c in range(0, SEG_W, MXU_W)], axis=1)
        return z * lax.rsqrt(ms * (1.0 / DK_A) + EPS)

    def table(ref):
        return jnp.broadcast_to(ref[...][None], (g, r, HEAD_W)).reshape(m, HEAD_W)

    cos, sine, sino = table(cos_ref), table(sine_ref), table(sino_ref)

    def rotate(z):
        outs = []
        for hh in range(H_R):
            zh = z[:, hh * HEAD_W:(hh + 1) * HEAD_W]
            nxt = pltpu.roll(zh, HEAD_W - 1, axis=1)
            prv = pltpu.roll(zh, 1, axis=1)
            outs.append(zh * cos + nxt * sine + prv * sino)
        return jnp.concatenate(outs, axis=1)

    put(qa_ref, group_rms(seg(0)) * gq_ref[...] * (DK_A ** -0.5 * LOG2E))
    ka = group_rms(seg(1)) * gk_ref[...]
    put(kf_ref, ka)
    put(ka_ref, ka)
    va = seg(2)
    put(vf_ref, va)
    put(va_ref, va)
    put(qr_ref, rotate(seg(3)))
    put(kr_ref, rotate(seg(4)) * (HEAD_W ** -0.5))
    put(vr_ref, seg(5))
    put(sg_ref, _silu(seg(6)))


def _inproj(x, mod, g1, w_in, gq_t, gk_t, tables, g_blk, r_blk):
    nb, l, d = x.shape
    assert nb % g_blk == 0 and l % r_blk == 0
    nl = l // r_blk
    tok = lambda w: pl.BlockSpec((g_blk, r_blk, w), lambda i: (i // nl, i % nl, 0))
    tab = pl.BlockSpec((r_blk, HEAD_W), lambda i: (i % nl, 0))
    out = lambda dt: jax.ShapeDtypeStruct((nb, l, SEG_W), dt)
    return pl.pallas_call(
        _inproj_kernel,
        grid=((nb // g_blk) * nl,),
        in_specs=[tok(d),
                  pl.BlockSpec((g_blk, 6, d), lambda i: (i // nl, 0, 0)),
                  _resident((1, d)), _resident(w_in.shape), _resident((1, SEG_W)), _resident((1, SEG_W)),
                  tab, tab, tab],
        out_specs=[tok(SEG_W)] * 9,
        out_shape=[out(F32), out(F32)] + [out(BF16)] * 7,
        compiler_params=_params(("parallel",)),
        name="inproj",
    )(x, mod, g1, w_in, gq_t, gk_t, *tables)


def _stack_maps(q):
    lane = lax.broadcasted_iota(jnp.int32, q.shape, 1)
    zero = jnp.zeros_like(q)
    return jnp.concatenate([jnp.where(lane < DK_A, q, zero), jnp.where(lane >= DK_A, q, zero)], axis=0)


def _lambda(lq1, lk1, lq2, lk2, lam_init):
    s1 = jnp.sum(lq1[...] * lk1[...], axis=-1, keepdims=True)
    s2 = jnp.sum(lq2[...] * lk2[...], axis=-1, keepdims=True)
    return jnp.exp(s1) - jnp.exp(s2) + lam_init


def _diff_finish(acc, l, lam, gs, lam_init):
    t = acc.shape[0] // 2
    o = acc[:t] * (1.0 / l[:t]) - lam * (acc[t:] * (1.0 / l[t:]))
    on = o * lax.rsqrt(jnp.mean(o * o, axis=-1, keepdims=True) + EPS)
    return on * gs * (1.0 - lam_init)


def _attn_prompt_kernel(q_ref, k_ref, v_ref, tbl_ref, lq1, lk1, lq2, lk2, gs_ref, o_ref,
                        m_sc, l_sc, acc_sc, *, lam_init):
    qi = pl.program_id(2)
    t = q_ref.shape[1]
    nblk = t // HEAD_W
    qq = _stack_maps(q_ref[0])
    m_sc[...] = jnp.full(m_sc.shape, NEG_INF, F32)
    l_sc[...] = jnp.zeros(l_sc.shape, F32)
    acc_sc[...] = jnp.zeros(acc_sc.shape, F32)

    def tile(j, bias):
        start = pl.multiple_of(j * t, t)
        s = _dot_nt(qq, k_ref[0, pl.ds(start, t), :])
        if bias is not None:
            s = s + jnp.concatenate([bias, bias], axis=0)
        blocks = [s[:, c * HEAD_W:(c + 1) * HEAD_W] for c in range(nblk)]
        mx = functools.reduce(jnp.maximum, blocks)
        m_old = m_sc[...]
        m_new = jnp.maximum(m_old, jnp.max(mx, axis=-1, keepdims=True))
        alpha = jnp.exp2(m_old - m_new)
        ps = [jnp.exp2(blk - m_new) for blk in blocks]
        l_sc[...] = alpha * l_sc[...] + functools.reduce(jnp.add, ps)
        p = jnp.concatenate([pc.astype(BF16) for pc in ps], axis=1)
        acc_sc[...] = alpha * acc_sc[...] + _dot(p, v_ref[0, pl.ds(start, t), :])
        m_sc[...] = m_new

    def far(j, carry):
        tile(j, None)
        return carry

    lax.fori_loop(0, jnp.maximum(qi - 1, 0), far, 0)

    @pl.when(qi >= 1)
    def _():
        tile(qi - 1, tbl_ref[0, :, :t])

    tile(qi, tbl_ref[0, :, t:])
    lam = _lambda(lq1, lk1, lq2, lk2, lam_init)
    l = jnp.sum(l_sc[...], axis=-1, keepdims=True)
    o_ref[0] = _diff_finish(acc_sc[...], l, lam, gs_ref[...], lam_init).astype(o_ref.dtype)


def _attn_prompt(q, k, v, tbl, lams, gs, lam_init, tq):
    b, s, _ = q.shape
    assert s % tq == 0 and tbl.shape == (H_A, tq, 2 * tq)
    vec = _resident((1, DK_A))
    return pl.pallas_call(
        functools.partial(_attn_prompt_kernel, lam_init=lam_init),
        grid=(b, H_A, s // tq),
        in_specs=[pl.BlockSpec((1, tq, HEAD_W), lambda bi, h, qi: (bi, qi, h)),
                  pl.BlockSpec((1, s, HEAD_W), lambda bi, h, qi: (bi, 0, h)),
                  pl.BlockSpec((1, s, HEAD_W), lambda bi, h, qi: (bi, 0, h)),
                  pl.BlockSpec((1, tq, 2 * tq), lambda bi, h, qi: (h, 0, 0)),
                  vec, vec, vec, vec, _resident((1, HEAD_W))],
        out_specs=pl.BlockSpec((1, tq, HEAD_W), lambda bi, h, qi: (bi, qi, h)),
        out_shape=jax.ShapeDtypeStruct((b, s, SEG_W), BF16),
        scratch_shapes=[pltpu.VMEM((2 * tq, HEAD_W), F32)] * 3,
        compiler_params=_params(("parallel", "parallel", "arbitrary")),
        name="attn_prompt",
    )(q, k, v, tbl, *lams, gs)


def _log_gamma(h):
    vals = [math.log(1.0 - 2.0 ** (-5.0 - i)) for i in range(H_R)]
    lg = jnp.float32(vals[H_R - 1])
    for i in range(H_R - 2, -1, -1):
        lg = jnp.where(h == i, jnp.float32(vals[i]), lg)
    return lg


def _retention_chunk(q, k, v, state, lg):
    c = q.shape[0]
    ti = lax.broadcasted_iota(jnp.int32, (c, c), 0)
    si = lax.broadcasted_iota(jnp.int32, (c, c), 1)
    dist = (ti - si).astype(F32)
    decay = jnp.where(dist >= 0, jnp.exp(jnp.maximum(dist, 0.0) * lg), 0.0)
    scores = _dot_nt(q, k) * decay
    row = lax.broadcasted_iota(jnp.int32, (c, 1), 0).astype(F32)
    q_dec = (q.astype(F32) * jnp.exp((row + 1.0) * lg)).astype(BF16)
    o = _dot(scores.astype(BF16), v) + _dot(q_dec, state.astype(BF16))
    k_dec = (k.astype(F32) * jnp.exp((c - 1.0 - row) * lg)).astype(BF16)
    new_state = jnp.exp(c * lg) * state + _dot_tn(k_dec, v)
    return o, new_state


def _retention_out(o, sg):
    on = o * lax.rsqrt(jnp.mean(o * o, axis=-1, keepdims=True) + EPS)
    return on * sg.astype(F32)


def _ret_prompt_kernel(q_ref, k_ref, v_ref, sg_ref, o_ref, st_ref, s_sc):
    h = pl.program_id(1)
    ci = pl.program_id(2)

    @pl.when(ci == 0)
    def _():
        s_sc[...] = jnp.zeros(s_sc.shape, F32)

    o, new_state = _retention_chunk(q_ref[0], k_ref[0], v_ref[0], s_sc[...], _log_gamma(h))
    s_sc[...] = new_state
    o_ref[0] = _retention_out(o, sg_ref[0]).astype(o_ref.dtype)

    @pl.when(ci == pl.num_programs(2) - 1)
    def _():
        st_ref[0, 0] = new_state


def _ret_prompt(qr, kr, vr, sg, c_blk):
    b, s, _ = qr.shape
    assert s % c_blk == 0
    blk = pl.BlockSpec((1, c_blk, HEAD_W), lambda bi, h, ci: (bi, ci, h))
    return pl.pallas_call(
        _ret_prompt_kernel,
        grid=(b, H_R, s // c_blk),
        in_specs=[blk, blk, blk, blk],
        out_specs=[blk, pl.BlockSpec((1, 1, HEAD_W, HEAD_W), lambda bi, h, ci: (bi, h, 0, 0))],
        out_shape=[jax.ShapeDtypeStruct((b, s, SEG_W), BF16),
                   jax.ShapeDtypeStruct((b, H_R, HEAD_W, HEAD_W), F32)],
        scratch_shapes=[pltpu.VMEM((HEAD_W, HEAD_W), F32)],
        compiler_params=_params(("parallel", "parallel", "arbitrary")),
        name="ret_prompt",
    )(qr, kr, vr, sg)


def _mix_sample_kernel(qa_ref, ka_ref, va_ref, ck_ref, cv_ref, tbl_ref, lq1, lk1, lq2, lk2, gs_ref,
                       qr_ref, kr_ref, vr_ref, sg_ref, st_ref, oa_ref, or_ref, so_ref, *, lam_init, near):
    p_len = ck_ref.shape[1]
    l = qa_ref.shape[1]
    lam = _lambda(lq1, lk1, lq2, lk2, lam_init)
    for h in range(H_A):
        cols = slice(h * HEAD_W, (h + 1) * HEAD_W)
        qq = _stack_maps(qa_ref[0, :, cols])
        kc = ck_ref[0, :, cols].astype(BF16)
        vc = cv_ref[0, :, cols].astype(BF16)
        bias = tbl_ref[h]
        bias2 = jnp.concatenate([bias, bias], axis=0)
        s_c = _dot_nt(qq, kc)
        s_c = jnp.concatenate([s_c[:, :p_len - near], s_c[:, p_len - near:] + bias2[:, :near]], axis=1)
        s_n = _dot_nt(qq, ka_ref[0, :, cols]) + bias2[:, near:]
        m = jnp.maximum(jnp.max(s_c, axis=-1, keepdims=True), jnp.max(s_n, axis=-1, keepdims=True))
        p_c = jnp.exp2(s_c - m)
        p_n = jnp.exp2(s_n - m)
        lsum = jnp.sum(p_c, axis=-1, keepdims=True) + jnp.sum(p_n, axis=-1, keepdims=True)
        acc = _dot(p_c.astype(BF16), vc) + _dot(p_n.astype(BF16), va_ref[0, :, cols])
        oa_ref[0, :, cols] = _diff_finish(acc, lsum, lam, gs_ref[...], lam_init).astype(oa_ref.dtype)
    for h in range(H_R):
        cols = slice(h * HEAD_W, (h + 1) * HEAD_W)
        lg = jnp.float32(math.log(1.0 - 2.0 ** (-5.0 - h)))
        o, new_state = _retention_chunk(qr_ref[0, :, cols], kr_ref[0, :, cols], vr_ref[0, :, cols],
                                        st_ref[0, h], lg)
        or_ref[0, :, cols] = _retention_out(o, sg_ref[0, :, cols]).astype(or_ref.dtype)
        so_ref[0, h] = new_state


def _mix_sample(qa, ka, va, cache_k, cache_v, tbl, lams, gs, qr, kr, vr, sg, state, lam_init, near):
    b, l, _ = qa.shape
    p_len = cache_k.shape[1]
    assert p_len % CHUNK == 0 and l <= CHUNK and p_len >= near
    tok = pl.BlockSpec((1, l, SEG_W), lambda bi: (bi, 0, 0))
    cache = pl.BlockSpec((1, p_len, SEG_W), lambda bi: (bi, 0, 0))
    st = pl.BlockSpec((1, H_R, HEAD_W, HEAD_W), lambda bi: (bi, 0, 0, 0))
    vec = _resident((1, DK_A))
    return pl.pallas_call(
        functools.partial(_mix_sample_kernel, lam_init=lam_init, near=near),
        grid=(b,),
        in_specs=[tok, tok, tok, cache, cache, _resident(tbl.shape), vec, vec, vec, vec,
                  _resident((1, HEAD_W)), tok, tok, tok, tok, st],
        out_specs=[tok, tok, st],
        out_shape=[jax.ShapeDtypeStruct((b, l, SEG_W), BF16), jax.ShapeDtypeStruct((b, l, SEG_W), BF16),
                   jax.ShapeDtypeStruct((b, H_R, HEAD_W, HEAD_W), F32)],
        compiler_params=_params(("parallel",)),
        name="mix_sample",
    )(qa, ka, va, cache_k, cache_v, tbl, *lams, gs, qr, kr, vr, sg, state)


def _ffn_kernel(x_ref, oa_ref, or_ref, mod_ref, g2_ref, wo_ref, wg_ref, wu_ref, wd_ref, y_ref, *, ff_blk):
    g, r, d = x_ref.shape
    m = g * r
    f = wg_ref.shape[1]
    proj = (_dot(oa_ref[...].reshape(m, SEG_W), wo_ref[:SEG_W, :])
            + _dot(or_ref[...].reshape(m, SEG_W), wo_ref[SEG_W:, :]))
    x1 = x_ref[...] + mod_ref[:, 2:3, :] * proj.reshape(g, r, d)
    y = x1 * lax.rsqrt(jnp.mean(x1 * x1, axis=-1, keepdims=True) + EPS) * g2_ref[...]
    h2 = (y * (1.0 + mod_ref[:, 4:5, :]) + mod_ref[:, 3:4, :]).astype(BF16).reshape(m, d)
    acc = jnp.zeros((m, d), F32)
    for c0 in range(0, f, ff_blk):
        c1 = min(c0 + ff_blk, f)
        ff = (_silu(_dot(h2, wg_ref[:, c0:c1])) * _dot(h2, wu_ref[:, c0:c1])).astype(BF16)
        acc = acc + _dot(ff, wd_ref[c0:c1, :])
    y_ref[...] = x1 + mod_ref[:, 5:6, :] * acc.reshape(g, r, d)


def _ffn(x, oa, orr, mod, g2, wo, wg, wu, wd, g_blk, r_blk, ff_blk):
    nb, l, d = x.shape
    assert nb % g_blk == 0 and l % r_blk == 0
    nl = l // r_blk
    tok = lambda w: pl.BlockSpec((g_blk, r_blk, w), lambda i: (i // nl, i % nl, 0))
    return pl.pallas_call(
        functools.partial(_ffn_kernel, ff_blk=ff_blk),
        grid=((nb // g_blk) * nl,),
        in_specs=[tok(d), tok(SEG_W), tok(SEG_W),
                  pl.BlockSpec((g_blk, 6, d), lambda i: (i // nl, 0, 0)),
                  _resident((1, d)), _resident(wo.shape), _resident(wg.shape), _resident(wu.shape),
                  _resident(wd.shape)],
        out_specs=tok(d),
        out_shape=jax.ShapeDtypeStruct((nb, l, d), F32),
        compiler_params=_params(("parallel",)),
        name="outproj_ffn",
    )(x, oa, orr, mod, g2, wo, wg, wu, wd)


def _rope_tables(pos):
    inv = 1.0 / (ROPE_BASE ** jnp.linspace(0.0, 1.0, HEAD_W // 2, dtype=F32))
    ang = pos.astype(F32)[:, None] * inv[None, :]
    sin = jnp.repeat(jnp.sin(ang), 2, axis=1)
    cos = jnp.repeat(jnp.cos(ang), 2, axis=1)
    even = (jnp.arange(HEAD_W) % 2 == 0)[None, :]
    return cos, jnp.where(even, -sin, 0.0), jnp.where(even, 0.0, sin)


def _tile_rows(n, target):
    t = min(n, target)
    while n % t:
        t //= 2
    return t


def kernel(x_prompt, x_sample, c_prompt, c_sample, cache_k, cache_v, state_ret, w_ada, b_ada, g_norm1, g_norm2, w_in, g_q, g_k, lam_q1, lam_k1, lam_q2, lam_k2, g_subln, w_out, w_ff_gate, w_ff_up, w_ff_down, rel_bias):
    depth = w_ada.shape[0]
    b, s, d = x_prompt.shape
    bs, l, _ = x_sample.shape
    p_len = cache_k.shape[2]
    assert w_in.shape[2] == N_SEG * SEG_W and d == 2 * SEG_W

    tm = _tile_rows(s, 512)
    tq = _tile_rows(s, 512)
    c_blk = _tile_rows(s, 256)
    near = 128
    assert tq % CHUNK == 0 and tq >= BUCKET_THRESHOLDS[-1]

    tab_p = _rope_tables(jnp.arange(s, dtype=jnp.int32))
    tab_s = _rope_tables(p_len + jnp.arange(l, dtype=jnp.int32))
    tbl_p = _bias_table(rel_bias, tq, 2 * tq, tq)
    tbl_s = _bias_table(rel_bias, l, near + l, near)
    c_all = jnp.concatenate([c_prompt, c_sample], axis=0)

    xp, xs = x_prompt, x_sample
    outs = [[] for _ in range(6)]
    for layer in range(depth):
        lam_init = _lambda_init(layer)
        mod = _modulation(c_all, w_ada[layer], b_ada[layer]).reshape(b + bs, 6, d)
        mod_p, mod_s = mod[:b], mod[b:]
        g1 = g_norm1[layer].reshape(1, d)
        g2 = g_norm2[layer].reshape(1, d)
        gq_t = jnp.tile(g_q[layer], SEG_W // DK_A).reshape(1, SEG_W)
        gk_t = jnp.tile(g_k[layer], SEG_W // DK_A).reshape(1, SEG_W)
        lams = [v[layer].reshape(1, DK_A) for v in (lam_q1, lam_k1, lam_q2, lam_k2)]
        gs = g_subln[layer].reshape(1, HEAD_W)
        wi = w_in[layer].astype(BF16)
        wo = w_out[layer].astype(BF16)
        wg = w_ff_gate[layer].astype(BF16)
        wu = w_ff_up[layer].astype(BF16)
        wd = w_ff_down[layer].astype(BF16)
        ff_blk = 1024

        kf, vf, qa, ka, va, qr, kr, vr, sg = _inproj(xp, mod_p, g1, wi, gq_t, gk_t, tab_p, 1, tm)
        oa = _attn_prompt(qa, ka, va, tbl_p, lams, gs, lam_init, tq)
        orr, st_p = _ret_prompt(qr, kr, vr, sg, c_blk)
        xp = _ffn(xp, oa, orr, mod_p, g2, wo, wg, wu, wd, 1, tm, ff_blk)

        kfs, vfs, qas, kas, vas, qrs, krs, vrs, sgs = _inproj(xs, mod_s, g1, wi, gq_t, gk_t, tab_s, bs, l)
        ck = cache_k[layer].reshape(bs, p_len, SEG_W)
        cv = cache_v[layer].reshape(bs, p_len, SEG_W)
        oas, ors, st_s = _mix_sample(qas, kas, vas, ck, cv, tbl_s, lams, gs, qrs, krs, vrs, sgs,
                                     state_ret[layer], lam_init, near)
        xs = _ffn(xs, oas, ors, mod_s, g2, wo, wg, wu, wd, bs, l, ff_blk)

        for lst, val in zip(outs, (kf.reshape(b, s, H_A, 2, DK_A), vf.reshape(b, s, H_A, HEAD_W), st_p,
                                   kfs.reshape(bs, l, H_A, 2, DK_A), vfs.reshape(bs, l, H_A, HEAD_W), st_s)):
            lst.append(val)
    return (xp, xs) + tuple(jnp.stack(o) for o in outs)
```

```python
import functools
import math

import jax
import jax.numpy as jnp
from jax import lax
from jax.experimental import pallas as pl
from jax.experimental.pallas import tpu as pltpu

F32 = jnp.float32
BF16 = jnp.bfloat16

CHUNK = 64
H_A = 4
DK_A = 64
H_R = 4
HEAD_W = 128
SEG_W = H_A * HEAD_W
N_SEG = 7
N_BUCKETS = 32
ROPE_BASE = 10000.0
EPS = 1e-6
NEG_INF = -1e30
LOG2E = math.log2(math.e)
BUCKET_THRESHOLDS = (12, 16, 23, 32, 46, 64, 91)
FAR_BUCKET = 15

MXU_W = 256
VMEM_LIMIT = 56 * 1024 * 1024


def _lambda_init(layer):
    return 0.8 - 0.6 * math.exp(-0.3 * layer)


def _params(sem):
    return pltpu.CompilerParams(dimension_semantics=sem, vmem_limit_bytes=VMEM_LIMIT)


def _resident(shape):
    return pl.BlockSpec(shape, lambda *_: (0,) * len(shape), pipeline_mode=pl.Buffered(1))


def _dot(a, b):
    return jnp.dot(a, b, preferred_element_type=F32)


def _dot_nt(a, b):
    return lax.dot_general(a, b, (((1,), (1,)), ((), ())), preferred_element_type=F32)


def _dot_tn(a, b):
    return lax.dot_general(a, b, (((0,), (0,)), ((), ())), preferred_element_type=F32)


def _silu(x):
    return x * jax.nn.sigmoid(x)


def _mod_kernel(c_ref, w_ref, b_ref, o_ref):
    s = _silu(c_ref[...]).astype(BF16)
    o_ref[...] = _dot(s, w_ref[...].astype(BF16)) + b_ref[...]


def _modulation(c_all, w_ada, b_ada):
    n, d = c_all.shape
    e = w_ada.shape[1]
    te = 1536
    assert e % te == 0
    return pl.pallas_call(
        _mod_kernel,
        grid=(e // te,),
        in_specs=[pl.BlockSpec((n, d), lambda j: (0, 0)),
                  pl.BlockSpec((d, te), lambda j: (0, j)),
                  pl.BlockSpec((1, te), lambda j: (0, j))],
        out_specs=pl.BlockSpec((n, te), lambda j: (0, j)),
        out_shape=jax.ShapeDtypeStruct((n, e), F32),
        compiler_params=_params(("parallel",)),
        name="modulation",
    )(c_all, w_ada, b_ada.reshape(1, e))


def _bias_table_kernel(rb_ref, o_ref, *, offset):
    h = pl.program_id(0)
    _, rows, cols = o_ref.shape
    r = lax.broadcasted_iota(jnp.int32, (rows, cols), 0)
    kc = lax.broadcasted_iota(jnp.int32, (rows, cols), 1) - offset
    rel = kc - r
    n = jnp.abs(rel)
    large = jnp.full((rows, cols), N_BUCKETS // 4, jnp.int32)
    for thr in BUCKET_THRESHOLDS:
        large = large + (n >= thr).astype(jnp.int32)
    bucket = jnp.where(rel > 0, N_BUCKETS // 2, 0) + jnp.where(n < N_BUCKETS // 4, n, large)
    val = jnp.zeros((rows, cols), F32)
    for b in range(N_BUCKETS):
        val = jnp.where(bucket == b, rb_ref[b, h], val)
    val = (val - rb_ref[FAR_BUCKET, h]) * LOG2E
    shift = int(math.log2(CHUNK))
    visible = lax.shift_right_arithmetic(kc, shift) <= lax.shift_right_arithmetic(r, shift)
    o_ref[0] = jnp.where(visible, val, NEG_INF)


def _bias_table(rel_bias, rows, cols, offset):
    assert offset >= BUCKET_THRESHOLDS[-1] and offset % CHUNK == 0
    return pl.pallas_call(
        functools.partial(_bias_table_kernel, offset=offset),
        grid=(H_A,),
        in_specs=[pl.BlockSpec(memory_space=pltpu.SMEM)],
        out_specs=pl.BlockSpec((1, rows, cols), lambda h: (h, 0, 0)),
        out_shape=jax.ShapeDtypeStruct((H_A, rows, cols), F32),
        compiler_params=_params(("parallel",)),
        name="bias_table",
    )(rel_bias)


def _inproj_kernel(x_ref, mod_ref, g1_ref, w_ref, gq_ref, gk_ref, cos_ref, sine_ref, sino_ref,
                   kf_ref, vf_ref, qa_ref, ka_ref, va_ref, qr_ref, kr_ref, vr_ref, sg_ref):
    g, r, d = x_ref.shape
    m = g * r
    x = x_ref[...]
    y = x * lax.rsqrt(jnp.mean(x * x, axis=-1, keepdims=True) + EPS) * g1_ref[...]
    hmod = y * (1.0 + mod_ref[:, 1:2, :]) + mod_ref[:, 0:1, :]
    hb = hmod.astype(BF16).reshape(m, d)

    def seg(j):
        return _dot(hb, w_ref[:, j * SEG_W:(j + 1) * SEG_W])

    def put(ref, val):
        ref[...] = val.reshape(g, r, SEG_W).astype(ref.dtype)

    gi = lax.broadcasted_iota(jnp.int32, (MXU_W, MXU_W), 0) // DK_A
    gj = lax.broadcasted_iota(jnp.int32, (MXU_W, MXU_W), 1) // DK_A
    ones_bd = (gi == gj).astype(BF16)

    def group_rms(z):
        sq = (z * z).astype(BF16)
        ms = jnp.concatenate([_dot(sq[:, c:c + MXU_W], ones_bd) for c in range(0, SEG_W, MXU_W)], axis=1)
        return z * lax.rsqrt(ms * (1.0 / DK_A) + EPS)

    def table(ref):
        return jnp.broadcast_to(ref[...][None], (g, r, HEAD_W)).reshape(m, HEAD_W)

    cos, sine, sino = table(cos_ref), table(sine_ref), table(sino_ref)

    def rotate(z):
        outs = []
        for hh in range(H_R):
            zh = z[:, hh * HEAD_W:(hh + 1) * HEAD_W]
            nxt = pltpu.roll(zh, HEAD_W - 1, axis=1)
            prv = pltpu.roll(zh, 1, axis=1)
            outs.append(zh * cos + nxt * sine + prv * sino)
        return jnp.concatenate(outs, axis=1)

    put(qa_ref, group_rms(seg(0)) * gq_ref[...] * (DK_A ** -0.5 * LOG2E))
    ka = group_rms(seg(1)) * gk_ref[...]
    put(kf_ref, ka)
    put(ka_ref, ka)
    va = seg(2)
    put(vf_ref, va)
    put(va_ref, va)
    put(qr_ref, rotate(seg(3)))
    put(kr_ref, rotate(seg(4)) * (HEAD_W ** -0.5))
    put(vr_ref, seg(5))
    put(sg_ref, _silu(seg(6)))


def _inproj(x, mod, g1, w_in, gq_t, gk_t, tables, g_blk, r_blk):
    nb, l, d = x.shape
    assert nb % g_blk == 0 and l % r_blk == 0
    nl = l // r_blk
    tok = lambda w: pl.BlockSpec((g_blk, r_blk, w), lambda i: (i // nl, i % nl, 0))
    tab = pl.BlockSpec((r_blk, HEAD_W), lambda i: (i % nl, 0))
    out = lambda dt: jax.ShapeDtypeStruct((nb, l, SEG_W), dt)
    return pl.pallas_call(
        _inproj_kernel,
        grid=((nb // g_blk) * nl,),
        in_specs=[tok(d),
                  pl.BlockSpec((g_blk, 6, d), lambda i: (i // nl, 0, 0)),
                  _resident((1, d)), _resident(w_in.shape), _resident((1, SEG_W)), _resident((1, SEG_W)),
                  tab, tab, tab],
        out_specs=[tok(SEG_W)] * 9,
        out_shape=[out(F32), out(F32)] + [out(BF16)] * 7,
        compiler_params=_params(("parallel",)),
        name="inproj",
    )(x, mod, g1, w_in, gq_t, gk_t, *tables)


def _stack_maps(q):
    lane = lax.broadcasted_iota(jnp.int32, q.shape, 1)
    zero = jnp.zeros_like(q)
    return jnp.concatenate([jnp.where(lane < DK_A, q, zero), jnp.where(lane >= DK_A, q, zero)], axis=0)


def _lambda(lq1, lk1, lq2, lk2, lam_init):
    s1 = jnp.sum(lq1[...] * lk1[...], axis=-1, keepdims=True)
    s2 = jnp.sum(lq2[...] * lk2[...], axis=-1, keepdims=True)
    return jnp.exp(s1) - jnp.exp(s2) + lam_init


def _diff_finish(acc, l, lam, gs, lam_init):
    t = acc.shape[0] // 2
    o = acc[:t] * (1.0 / l[:t]) - lam * (acc[t:] * (1.0 / l[t:]))
    on = o * lax.rsqrt(jnp.mean(o * o, axis=-1, keepdims=True) + EPS)
    return on * gs * (1.0 - lam_init)


def _attn_prompt_kernel(q_ref, k_ref, v_ref, tbl_ref, lq1, lk1, lq2, lk2, gs_ref, o_ref,
                        qq_sc, m_sc, l_sc, acc_sc, *, lam_init, tk, tkd, near):
    qi = pl.program_id(2)
    tq = q_ref.shape[1]
    q = q_ref[0]
    lane = lax.broadcasted_iota(jnp.int32, q.shape, 1)
    zero = jnp.zeros_like(q)
    qq_sc[0] = jnp.where(lane < DK_A, q, zero)
    qq_sc[1] = jnp.where(lane >= DK_A, q, zero)
    m_sc[...] = jnp.full(m_sc.shape, NEG_INF, F32)
    l_sc[...] = jnp.zeros(l_sc.shape, F32)
    acc_sc[...] = jnp.zeros(acc_sc.shape, F32)

    def tile(start, width, row0, bias=None, corner=None):
        nr = tq - row0
        nblk = width // HEAD_W
        rows = slice(row0, tq)
        flat = lambda x: x.reshape(2 * nr, x.shape[-1])
        kt = k_ref[0, pl.ds(start, width), :]
        vt = v_ref[0, pl.ds(start, width), :]
        s = _dot_nt(flat(qq_sc[:, rows, :]), kt)
        if bias is not None:
            s = s + jnp.concatenate([bias, bias], axis=0)
        blocks = [s[:, c * HEAD_W:(c + 1) * HEAD_W] for c in range(nblk)]
        if corner is not None:
            last = blocks[-1]
            blocks[-1] = jnp.concatenate([last[:near] + corner, last[near:nr],
                                          last[nr:nr + near] + corner, last[nr + near:]], axis=0)
        m_old = flat(m_sc[:, rows, :])
        mx = functools.reduce(jnp.maximum, blocks)
        m_new = jnp.maximum(m_old, jnp.max(mx, axis=-1, keepdims=True))
        alpha = jnp.exp2(m_old - m_new)
        ps = [jnp.exp2(blk - m_new) for blk in blocks]
        l_new = alpha * flat(l_sc[:, rows, :]) + functools.reduce(jnp.add, ps)
        p = jnp.concatenate([pc.astype(BF16) for pc in ps], axis=1)
        acc_new = alpha * flat(acc_sc[:, rows, :]) + _dot(p, vt)
        m_sc[:, rows, :] = m_new.reshape(2, nr, HEAD_W)
        l_sc[:, rows, :] = l_new.reshape(2, nr, HEAD_W)
        acc_sc[:, rows, :] = acc_new.reshape(2, nr, HEAD_W)

    n_far = qi * (tq // tk)
    corner_tbl = tbl_ref[0, :near, :near]

    def far(j, carry):
        flag = (j == n_far - 1).astype(F32)
        tile(pl.multiple_of(j * tk, tk), tk, 0, corner=corner_tbl * flag)
        return carry

    lax.fori_loop(0, n_far, far, 0)
    for d in range(tq // tkd):
        tile(pl.multiple_of(qi * tq + d * tkd, tkd), tkd, d * tkd,
             bias=tbl_ref[0, d * tkd:, near + d * tkd:near + (d + 1) * tkd])
    lam = _lambda(lq1, lk1, lq2, lk2, lam_init)
    l = jnp.sum(l_sc[...], axis=-1, keepdims=True).reshape(2 * tq, 1)
    acc = acc_sc[...].reshape(2 * tq, HEAD_W)
    o_ref[0] = _diff_finish(acc, l, lam, gs_ref[...], lam_init).astype(o_ref.dtype)


def _attn_prompt(q, k, v, tbl, lams, gs, lam_init, tq, tk, tkd, near):
    b, s, _ = q.shape
    assert s % tq == 0 and tq % tk == 0 and tq % tkd == 0 and tbl.shape == (H_A, tq, near + tq)
    assert near == HEAD_W and tk % HEAD_W == 0 and tkd % HEAD_W == 0
    vec = _resident((1, DK_A))
    return pl.pallas_call(
        functools.partial(_attn_prompt_kernel, lam_init=lam_init, tk=tk, tkd=tkd, near=near),
        grid=(b, H_A, s // tq),
        in_specs=[pl.BlockSpec((1, tq, HEAD_W), lambda bi, h, qi: (bi, qi, h)),
                  pl.BlockSpec((1, s, HEAD_W), lambda bi, h, qi: (bi, 0, h)),
                  pl.BlockSpec((1, s, HEAD_W), lambda bi, h, qi: (bi, 0, h)),
                  pl.BlockSpec((1, tq, near + tq), lambda bi, h, qi: (h, 0, 0)),
                  vec, vec, vec, vec, _resident((1, HEAD_W))],
        out_specs=pl.BlockSpec((1, tq, HEAD_W), lambda bi, h, qi: (bi, qi, h)),
        out_shape=jax.ShapeDtypeStruct((b, s, SEG_W), BF16),
        scratch_shapes=[pltpu.VMEM((2, tq, HEAD_W), BF16)] + [pltpu.VMEM((2, tq, HEAD_W), F32)] * 3,
        compiler_params=_params(("parallel", "parallel", "arbitrary")),
        name="attn_prompt",
    )(q, k, v, tbl, *lams, gs)


def _log_gamma(h):
    vals = [math.log(1.0 - 2.0 ** (-5.0 - i)) for i in range(H_R)]
    lg = jnp.float32(vals[H_R - 1])
    for i in range(H_R - 2, -1, -1):
        lg = jnp.where(h == i, jnp.float32(vals[i]), lg)
    return lg


def _retention_chunk(q, k, v, state, lg):
    c = q.shape[0]
    ti = lax.broadcasted_iota(jnp.int32, (c, c), 0)
    si = lax.broadcasted_iota(jnp.int32, (c, c), 1)
    dist = (ti - si).astype(F32)
    decay = jnp.where(dist >= 0, jnp.exp(jnp.maximum(dist, 0.0) * lg), 0.0)
    scores = _dot_nt(q, k) * decay
    row = lax.broadcasted_iota(jnp.int32, (c, 1), 0).astype(F32)
    q_dec = (q.astype(F32) * jnp.exp((row + 1.0) * lg)).astype(BF16)
    o = _dot(scores.astype(BF16), v) + _dot(q_dec, state.astype(BF16))
    k_dec = (k.astype(F32) * jnp.exp((c - 1.0 - row) * lg)).astype(BF16)
    new_state = jnp.exp(c * lg) * state + _dot_tn(k_dec, v)
    return o, new_state


def _retention_out(o, sg):
    on = o * lax.rsqrt(jnp.mean(o * o, axis=-1, keepdims=True) + EPS)
    return on * sg.astype(F32)


def _ret_prompt_kernel(q_ref, k_ref, v_ref, sg_ref, o_ref, st_ref, s_sc, d_sc, qd_sc, kd_sc, *, c):
    h = pl.program_id(1)
    ti = pl.program_id(2)
    tb = q_ref.shape[1]

    @pl.when(ti == 0)
    def _():
        lg = _log_gamma(h)
        s_sc[...] = jnp.zeros(s_sc.shape, F32)
        tt = lax.broadcasted_iota(jnp.int32, (c, c), 0)
        ss = lax.broadcasted_iota(jnp.int32, (c, c), 1)
        dist = (tt - ss).astype(F32)
        d_sc[...] = jnp.where(dist >= 0, jnp.exp(jnp.maximum(dist, 0.0) * lg), 0.0)
        row = lax.broadcasted_iota(jnp.int32, (c, HEAD_W), 0).astype(F32)
        qd_sc[...] = jnp.exp((row + 1.0) * lg)
        kd_sc[...] = jnp.exp((c - 1.0 - row) * lg)

    state = s_sc[...]
    g_c = qd_sc[c - 1:c, :]
    for i in range(tb // c):
        rows = slice(i * c, (i + 1) * c)
        q, k, v = q_ref[0, rows, :], k_ref[0, rows, :], v_ref[0, rows, :]
        scores = _dot_nt(q, k) * d_sc[...]
        q_dec = (q.astype(F32) * qd_sc[...]).astype(BF16)
        o = _dot(scores.astype(BF16), v) + _dot(q_dec, state.astype(BF16))
        k_dec = (k.astype(F32) * kd_sc[...]).astype(BF16)
        state = g_c * state + _dot_tn(k_dec, v)
        o_ref[0, rows, :] = _retention_out(o, sg_ref[0, rows, :]).astype(o_ref.dtype)
    s_sc[...] = state

    @pl.when(ti == pl.num_programs(2) - 1)
    def _():
        st_ref[0, 0] = state


def _ret_prompt(qr, kr, vr, sg, t_blk, c):
    b, s, _ = qr.shape
    assert s % t_blk == 0 and t_blk % c == 0
    blk = pl.BlockSpec((1, t_blk, HEAD_W), lambda bi, h, ti: (bi, ti, h))
    return pl.pallas_call(
        functools.partial(_ret_prompt_kernel, c=c),
        grid=(b, H_R, s // t_blk),
        in_specs=[blk, blk, blk, blk],
        out_specs=[blk, pl.BlockSpec((1, 1, HEAD_W, HEAD_W), lambda bi, h, ti: (bi, h, 0, 0))],
        out_shape=[jax.ShapeDtypeStruct((b, s, SEG_W), BF16),
                   jax.ShapeDtypeStruct((b, H_R, HEAD_W, HEAD_W), F32)],
        scratch_shapes=[pltpu.VMEM((HEAD_W, HEAD_W), F32), pltpu.VMEM((c, c), F32),
                        pltpu.VMEM((c, HEAD_W), F32), pltpu.VMEM((c, HEAD_W), F32)],
        compiler_params=_params(("parallel", "parallel", "arbitrary")),
        name="ret_prompt",
    )(qr, kr, vr, sg)


def _mix_sample_kernel(qa_ref, ka_ref, va_ref, ck_ref, cv_ref, tbl_ref, lq1, lk1, lq2, lk2, gs_ref,
                       qr_ref, kr_ref, vr_ref, sg_ref, st_ref, oa_ref, or_ref, so_ref, *, lam_init, near):
    p_len = ck_ref.shape[1]
    l = qa_ref.shape[1]
    lam = _lambda(lq1, lk1, lq2, lk2, lam_init)
    for h in range(H_A):
        cols = slice(h * HEAD_W, (h + 1) * HEAD_W)
        qq = _stack_maps(qa_ref[0, :, cols])
        kc = ck_ref[0, :, cols].astype(BF16)
        vc = cv_ref[0, :, cols].astype(BF16)
        bias = tbl_ref[h]
        bias2 = jnp.concatenate([bias, bias], axis=0)
        s_c = _dot_nt(qq, kc)
        s_c = jnp.concatenate([s_c[:, :p_len - near], s_c[:, p_len - near:] + bias2[:, :near]], axis=1)
        s_n = _dot_nt(qq, ka_ref[0, :, cols]) + bias2[:, near:]
        m = jnp.maximum(jnp.max(s_c, axis=-1, keepdims=True), jnp.max(s_n, axis=-1, keepdims=True))
        p_c = jnp.exp2(s_c - m)
        p_n = jnp.exp2(s_n - m)
        lsum = jnp.sum(p_c, axis=-1, keepdims=True) + jnp.sum(p_n, axis=-1, keepdims=True)
        acc = _dot(p_c.astype(BF16), vc) + _dot(p_n.astype(BF16), va_ref[0, :, cols])
        oa_ref[0, :, cols] = _diff_finish(acc, lsum, lam, gs_ref[...], lam_init).astype(oa_ref.dtype)
    for h in range(H_R):
        cols = slice(h * HEAD_W, (h + 1) * HEAD_W)
        lg = jnp.float32(math.log(1.0 - 2.0 ** (-5.0 - h)))
        o, new_state = _retention_chunk(qr_ref[0, :, cols], kr_ref[0, :, cols], vr_ref[0, :, cols],
                                        st_ref[0, h], lg)
        or_ref[0, :, cols] = _retention_out(o, sg_ref[0, :, cols]).astype(or_ref.dtype)
        so_ref[0, h] = new_state


def _mix_sample(qa, ka, va, cache_k, cache_v, tbl, lams, gs, qr, kr, vr, sg, state, lam_init, near):
    b, l, _ = qa.shape
    p_len = cache_k.shape[1]
    assert p_len % CHUNK == 0 and l <= CHUNK and p_len >= near
    tok = pl.BlockSpec((1, l, SEG_W), lambda bi: (bi, 0, 0))
    cache = pl.BlockSpec((1, p_len, SEG_W), lambda bi: (bi, 0, 0))
    st = pl.BlockSpec((1, H_R, HEAD_W, HEAD_W), lambda bi: (bi, 0, 0, 0))
    vec = _resident((1, DK_A))
    return pl.pallas_call(
        functools.partial(_mix_sample_kernel, lam_init=lam_init, near=near),
        grid=(b,),
        in_specs=[tok, tok, tok, cache, cache, _resident(tbl.shape), vec, vec, vec, vec,
                  _resident((1, HEAD_W)), tok, tok, tok, tok, st],
        out_specs=[tok, tok, st],
        out_shape=[jax.ShapeDtypeStruct((b, l, SEG_W), BF16), jax.ShapeDtypeStruct((b, l, SEG_W), BF16),
                   jax.ShapeDtypeStruct((b, H_R, HEAD_W, HEAD_W), F32)],
        compiler_params=_params(("parallel",)),
        name="mix_sample",
    )(qa, ka, va, cache_k, cache_v, tbl, *lams, gs, qr, kr, vr, sg, state)


def _ffn_kernel(x_ref, oa_ref, or_ref, mod_ref, g2_ref, wo_ref, wg_ref, wu_ref, wd_ref, y_ref, *, ff_blk):
    g, r, d = x_ref.shape
    m = g * r
    f = wg_ref.shape[1]
    proj = (_dot(oa_ref[...].reshape(m, SEG_W), wo_ref[:SEG_W, :])
            + _dot(or_ref[...].reshape(m, SEG_W), wo_ref[SEG_W:, :]))
    x1 = x_ref[...] + mod_ref[:, 2:3, :] * proj.reshape(g, r, d)
    y = x1 * lax.rsqrt(jnp.mean(x1 * x1, axis=-1, keepdims=True) + EPS) * g2_ref[...]
    h2 = (y * (1.0 + mod_ref[:, 4:5, :]) + mod_ref[:, 3:4, :]).astype(BF16).reshape(m, d)
    acc = jnp.zeros((m, d), F32)
    for c0 in range(0, f, ff_blk):
        c1 = min(c0 + ff_blk, f)
        ff = (_silu(_dot(h2, wg_ref[:, c0:c1])) * _dot(h2, wu_ref[:, c0:c1])).astype(BF16)
        acc = acc + _dot(ff, wd_ref[c0:c1, :])
    y_ref[...] = x1 + mod_ref[:, 5:6, :] * acc.reshape(g, r, d)


def _ffn(x, oa, orr, mod, g2, wo, wg, wu, wd, g_blk, r_blk, ff_blk):
    nb, l, d = x.shape
    assert nb % g_blk == 0 and l % r_blk == 0
    nl = l // r_blk
    tok = lambda w: pl.BlockSpec((g_blk, r_blk, w), lambda i: (i // nl, i % nl, 0))
    return pl.pallas_call(
        functools.partial(_ffn_kernel, ff_blk=ff_blk),
        grid=((nb // g_blk) * nl,),
        in_specs=[tok(d), tok(SEG_W), tok(SEG_W),
                  pl.BlockSpec((g_blk, 6, d), lambda i: (i // nl, 0, 0)),
                  _resident((1, d)), _resident(wo.shape), _resident(wg.shape), _resident(wu.shape),
                  _resident(wd.shape)],
        out_specs=tok(d),
        out_shape=jax.ShapeDtypeStruct((nb, l, d), F32),
        compiler_params=_params(("parallel",)),
        name="outproj_ffn",
    )(x, oa, orr, mod, g2, wo, wg, wu, wd)


def _rope_tables(pos):
    inv = 1.0 / (ROPE_BASE ** jnp.linspace(0.0, 1.0, HEAD_W // 2, dtype=F32))
    ang = pos.astype(F32)[:, None] * inv[None, :]
    sin = jnp.repeat(jnp.sin(ang), 2, axis=1)
    cos = jnp.repeat(jnp.cos(ang), 2, axis=1)
    even = (jnp.arange(HEAD_W) % 2 == 0)[None, :]
    return cos, jnp.where(even, -sin, 0.0), jnp.where(even, 0.0, sin)


def _tile_rows(n, target):
    t = min(n, target)
    while n % t:
        t //= 2
    return t


def kernel(x_prompt, x_sample, c_prompt, c_sample, cache_k, cache_v, state_ret, w_ada, b_ada, g_norm1, g_norm2, w_in, g_q, g_k, lam_q1, lam_k1, lam_q2, lam_k2, g_subln, w_out, w_ff_gate, w_ff_up, w_ff_down, rel_bias):
    depth = w_ada.shape[0]
    b, s, d = x_prompt.shape
    bs, l, _ = x_sample.shape
    p_len = cache_k.shape[2]
    assert w_in.shape[2] == N_SEG * SEG_W and d == 2 * SEG_W

    tm = _tile_rows(s, 512)
    tq = _tile_rows(s, 1024)
    tk = _tile_rows(tq, 1024)
    tkd = _tile_rows(tq, 512)
    ret_blk = _tile_rows(s, 1024)
    ret_c = _tile_rows(ret_blk, 256)
    near = HEAD_W
    assert tq % CHUNK == 0 and tk % CHUNK == 0 and tkd % CHUNK == 0 and near >= BUCKET_THRESHOLDS[-1]

    tab_p = _rope_tables(jnp.arange(s, dtype=jnp.int32))
    tab_s = _rope_tables(p_len + jnp.arange(l, dtype=jnp.int32))
    tbl_p = _bias_table(rel_bias, tq, near + tq, near)
    tbl_s = _bias_table(rel_bias, l, near + l, near)
    c_all = jnp.concatenate([c_prompt, c_sample], axis=0)

    xp, xs = x_prompt, x_sample
    outs = [[] for _ in range(6)]
    for layer in range(depth):
        lam_init = _lambda_init(layer)
        mod = _modulation(c_all, w_ada[layer], b_ada[layer]).reshape(b + bs, 6, d)
        mod_p, mod_s = mod[:b], mod[b:]
        g1 = g_norm1[layer].reshape(1, d)
        g2 = g_norm2[layer].reshape(1, d)
        gq_t = jnp.tile(g_q[layer], SEG_W // DK_A).reshape(1, SEG_W)
        gk_t = jnp.tile(g_k[layer], SEG_W // DK_A).reshape(1, SEG_W)
        lams = [v[layer].reshape(1, DK_A) for v in (lam_q1, lam_k1, lam_q2, lam_k2)]
        gs = g_subln[layer].reshape(1, HEAD_W)
        wi = w_in[layer].astype(BF16)
        wo = w_out[layer].astype(BF16)
        wg = w_ff_gate[layer].astype(BF16)
        wu = w_ff_up[layer].astype(BF16)
        wd = w_ff_down[layer].astype(BF16)
        ff_blk = 1024

        kf, vf, qa, ka, va, qr, kr, vr, sg = _inproj(xp, mod_p, g1, wi, gq_t, gk_t, tab_p, 1, tm)
        oa = _attn_prompt(qa, ka, va, tbl_p, lams, gs, lam_init, tq, tk, tkd, near)
        orr, st_p = _ret_prompt(qr, kr, vr, sg, ret_blk, ret_c)
        xp = _ffn(xp, oa, orr, mod_p, g2, wo, wg, wu, wd, 1, tm, ff_blk)

        kfs, vfs, qas, kas, vas, qrs, krs, vrs, sgs = _inproj(xs, mod_s, g1, wi, gq_t, gk_t, tab_s, bs, l)
        ck = cache_k[layer].reshape(bs, p_len, SEG_W)
        cv = cache_v[layer].reshape(bs, p_len, SEG_W)
        oas, ors, st_s = _mix_sample(qas, kas, vas, ck, cv, tbl_s, lams, gs, qrs, krs, vrs, sgs,
                                     state_ret[layer], lam_init, near)
        xs = _ffn(xs, oas, ors, mod_s, g2, wo, wg, wu, wd, bs, l, ff_blk)

        for lst, val in zip(outs, (kf.reshape(b, s, H_A, 2, DK_A), vf.reshape(b, s, H_A, HEAD_W), st_p,
                                   kfs.reshape(bs, l, H_A, 2, DK_A), vfs.reshape(bs, l, H_A, HEAD_W), st_s)):
            lst.append(val)
    return (xp, xs) + tuple(jnp.stack(o) for o in outs)
```

```python
import functools
import math

import jax
import jax.numpy as jnp
from jax import lax
from jax.experimental import pallas as pl
from jax.experimental.pallas import tpu as pltpu

F32 = jnp.float32
BF16 = jnp.bfloat16

CHUNK = 64
H_A = 4
DK_A = 64
H_R = 4
HEAD_W = 128
SEG_W = H_A * HEAD_W
N_SEG = 7
N_BUCKETS = 32
ROPE_BASE = 10000.0
EPS = 1e-6
NEG_INF = -1e30
LOG2E = math.log2(math.e)
BUCKET_THRESHOLDS = (12, 16, 23, 32, 46, 64, 91)
FAR_BUCKET = 15

MXU_W = 256
VMEM_LIMIT = 56 * 1024 * 1024


def _lambda_init(layer):
    return 0.8 - 0.6 * math.exp(-0.3 * layer)


def _params(sem):
    return pltpu.CompilerParams(dimension_semantics=sem, vmem_limit_bytes=VMEM_LIMIT)


def _resident(shape):
    return pl.BlockSpec(shape, lambda *_: (0,) * len(shape), pipeline_mode=pl.Buffered(1))


def _dot(a, b):
    return jnp.dot(a, b, preferred_element_type=F32)


def _dot_nt(a, b):
    return lax.dot_general(a, b, (((1,), (1,)), ((), ())), preferred_element_type=F32)


def _dot_tn(a, b):
    return lax.dot_general(a, b, (((0,), (0,)), ((), ())), preferred_element_type=F32)


def _silu(x):
    return x * jax.nn.sigmoid(x)


def _mod_kernel(c_ref, w_ref, b_ref, o_ref):
    s = _silu(c_ref[...]).astype(BF16)
    o_ref[...] = _dot(s, w_ref[...].astype(BF16)) + b_ref[...]


def _modulation(c_all, w_ada, b_ada):
    n, d = c_all.shape
    e = w_ada.shape[1]
    te = 1536
    assert e % te == 0
    return pl.pallas_call(
        _mod_kernel,
        grid=(e // te,),
        in_specs=[pl.BlockSpec((n, d), lambda j: (0, 0)),
                  pl.BlockSpec((d, te), lambda j: (0, j)),
                  pl.BlockSpec((1, te), lambda j: (0, j))],
        out_specs=pl.BlockSpec((n, te), lambda j: (0, j)),
        out_shape=jax.ShapeDtypeStruct((n, e), F32),
        compiler_params=_params(("parallel",)),
        name="modulation",
    )(c_all, w_ada, b_ada.reshape(1, e))


def _bias_table_kernel(rb_ref, o_ref, *, offset):
    h = pl.program_id(0)
    _, rows, cols = o_ref.shape
    r = lax.broadcasted_iota(jnp.int32, (rows, cols), 0)
    kc = lax.broadcasted_iota(jnp.int32, (rows, cols), 1) - offset
    rel = kc - r
    n = jnp.abs(rel)
    large = jnp.full((rows, cols), N_BUCKETS // 4, jnp.int32)
    for thr in BUCKET_THRESHOLDS:
        large = large + (n >= thr).astype(jnp.int32)
    bucket = jnp.where(rel > 0, N_BUCKETS // 2, 0) + jnp.where(n < N_BUCKETS // 4, n, large)
    val = jnp.zeros((rows, cols), F32)
    for b in range(N_BUCKETS):
        val = jnp.where(bucket == b, rb_ref[b, h], val)
    val = (val - rb_ref[FAR_BUCKET, h]) * LOG2E
    shift = int(math.log2(CHUNK))
    visible = lax.shift_right_arithmetic(kc, shift) <= lax.shift_right_arithmetic(r, shift)
    o_ref[0] = jnp.where(visible, val, NEG_INF)


def _bias_table(rel_bias, rows, cols, offset):
    assert offset >= BUCKET_THRESHOLDS[-1] and offset % CHUNK == 0
    return pl.pallas_call(
        functools.partial(_bias_table_kernel, offset=offset),
        grid=(H_A,),
        in_specs=[pl.BlockSpec(memory_space=pltpu.SMEM)],
        out_specs=pl.BlockSpec((1, rows, cols), lambda h: (h, 0, 0)),
        out_shape=jax.ShapeDtypeStruct((H_A, rows, cols), F32),
        compiler_params=_params(("parallel",)),
        name="bias_table",
    )(rel_bias)


def _inproj_kernel(x_ref, mod_ref, g1_ref, w_ref, gq_ref, gk_ref, cos_ref, sine_ref, sino_ref,
                   kf_ref, vf_ref, qa_ref, ka_ref, va_ref, qr_ref, kr_ref, vr_ref, sg_ref):
    g, r, d = x_ref.shape
    m = g * r
    x = x_ref[...]
    y = x * lax.rsqrt(jnp.mean(x * x, axis=-1, keepdims=True) + EPS) * g1_ref[...]
    hmod = y * (1.0 + mod_ref[:, 1:2, :]) + mod_ref[:, 0:1, :]
    hb = hmod.astype(BF16).reshape(m, d)

    def seg(j):
        return _dot(hb, w_ref[:, j * SEG_W:(j + 1) * SEG_W])

    def put(ref, val):
        ref[...] = val.reshape(g, r, SEG_W).astype(ref.dtype)

    gi = lax.broadcasted_iota(jnp.int32, (MXU_W, MXU_W), 0) // DK_A
    gj = lax.broadcasted_iota(jnp.int32, (MXU_W, MXU_W), 1) // DK_A
    ones_bd = (gi == gj).astype(BF16)

    def group_rms(z):
        sq = (z * z).astype(BF16)
        ms = jnp.concatenate([_dot(sq[:, c:c + MXU_W], ones_bd) for c in range(0, SEG_W, MXU_W)], axis=1)
        return z * lax.rsqrt(ms * (1.0 / DK_A) + EPS)

    def table(ref):
        return jnp.broadcast_to(ref[...][None], (g, r, HEAD_W)).reshape(m, HEAD_W)

    cos, sine, sino = table(cos_ref), table(sine_ref), table(sino_ref)

    def rotate(z):
        outs = []
        for hh in range(H_R):
            zh = z[:, hh * HEAD_W:(hh + 1) * HEAD_W]
            nxt = pltpu.roll(zh, HEAD_W - 1, axis=1)
            prv = pltpu.roll(zh, 1, axis=1)
            outs.append(zh * cos + nxt * sine + prv * sino)
        return jnp.concatenate(outs, axis=1)

    put(qa_ref, group_rms(seg(0)) * gq_ref[...] * (DK_A ** -0.5 * LOG2E))
    ka = group_rms(seg(1)) * gk_ref[...]
    put(kf_ref, ka)
    put(ka_ref, ka)
    va = seg(2)
    for hh in range(H_A):
        vf_ref[:, pl.ds(hh, r, stride=H_A), :] = va[:, hh * HEAD_W:(hh + 1) * HEAD_W].reshape(g, r, HEAD_W)
    put(va_ref, va)
    put(qr_ref, rotate(seg(3)))
    put(kr_ref, rotate(seg(4)) * (HEAD_W ** -0.5))
    put(vr_ref, seg(5))
    put(sg_ref, _silu(seg(6)))


def _inproj(x, mod, g1, w_in, gq_t, gk_t, tables, g_blk, r_blk):
    nb, l, d = x.shape
    assert nb % g_blk == 0 and l % r_blk == 0
    nl = l // r_blk
    tok = lambda w: pl.BlockSpec((g_blk, r_blk, w), lambda i: (i // nl, i % nl, 0))
    tab = pl.BlockSpec((r_blk, HEAD_W), lambda i: (i % nl, 0))
    out = lambda dt: jax.ShapeDtypeStruct((nb, l, SEG_W), dt)
    return pl.pallas_call(
        _inproj_kernel,
        grid=((nb // g_blk) * nl,),
        in_specs=[tok(d),
                  pl.BlockSpec((g_blk, 6, d), lambda i: (i // nl, 0, 0)),
                  _resident((1, d)), _resident(w_in.shape), _resident((1, SEG_W)), _resident((1, SEG_W)),
                  tab, tab, tab],
        out_specs=[tok(SEG_W), pl.BlockSpec((g_blk, r_blk * H_A, HEAD_W), lambda i: (i // nl, i % nl, 0))]
        + [tok(SEG_W)] * 7,
        out_shape=[out(F32), jax.ShapeDtypeStruct((nb, l * H_A, HEAD_W), F32)] + [out(BF16)] * 7,
        compiler_params=_params(("parallel",)),
        name="inproj",
    )(x, mod, g1, w_in, gq_t, gk_t, *tables)


def _stack_maps(q):
    lane = lax.broadcasted_iota(jnp.int32, q.shape, 1)
    zero = jnp.zeros_like(q)
    return jnp.concatenate([jnp.where(lane < DK_A, q, zero), jnp.where(lane >= DK_A, q, zero)], axis=0)


def _lambda(lq1, lk1, lq2, lk2, lam_init):
    s1 = jnp.sum(lq1[...] * lk1[...], axis=-1, keepdims=True)
    s2 = jnp.sum(lq2[...] * lk2[...], axis=-1, keepdims=True)
    return jnp.exp(s1) - jnp.exp(s2) + lam_init


def _diff_finish(acc, l, lam, gs, lam_init):
    t = acc.shape[0] // 2
    o = acc[:t] * (1.0 / l[:t]) - lam * (acc[t:] * (1.0 / l[t:]))
    on = o * lax.rsqrt(jnp.mean(o * o, axis=-1, keepdims=True) + EPS)
    return on * gs * (1.0 - lam_init)


def _attn_prompt_kernel(q_ref, k_ref, v_ref, tbl_ref, lq1, lk1, lq2, lk2, gs_ref, o_ref,
                        qq_sc, m_sc, l_sc, acc_sc, *, lam_init, tk, tkd, near):
    qi = pl.program_id(2)
    tq = q_ref.shape[1]
    q = q_ref[0]
    lane = lax.broadcasted_iota(jnp.int32, q.shape, 1)
    zero = jnp.zeros_like(q)
    qq_sc[0] = jnp.where(lane < DK_A, q, zero)
    qq_sc[1] = jnp.where(lane >= DK_A, q, zero)
    m_sc[...] = jnp.full(m_sc.shape, NEG_INF, F32)
    l_sc[...] = jnp.zeros(l_sc.shape, F32)
    acc_sc[...] = jnp.zeros(acc_sc.shape, F32)

    def tile(start, width, row0, corner, diag):
        nr = tq - row0
        nblk = width // HEAD_W
        rows = slice(row0, tq)
        flat = lambda x: x.reshape(2 * nr, x.shape[-1])
        kt = k_ref[0, pl.ds(start, width), :]
        vt = v_ref[0, pl.ds(start, width), :]
        s = _dot_nt(flat(qq_sc[:, rows, :]), kt)
        blocks = [s[:, c * HEAD_W:(c + 1) * HEAD_W] for c in range(nblk)]

        def add_rows(blk, r0, val):
            r1 = r0 + val.shape[0]
            cuts = [(0, r0, False), (r0, r1, True), (r1, nr + r0, False), (nr + r0, nr + r1, True),
                    (nr + r1, 2 * nr, False)]
            return jnp.concatenate([blk[a:b] + val if hit else blk[a:b] for a, b, hit in cuts if b > a], axis=0)

        if diag:
            blocks = [add_rows(blk, 0, tbl_ref[0, :, near + c * HEAD_W:near + (c + 1) * HEAD_W])
                      for c, blk in enumerate(blocks)]
        corner_row0 = width if diag else 0
        if corner_row0 < nr:
            blocks[-1] = add_rows(blocks[-1], corner_row0, corner)
        m_old = flat(m_sc[:, rows, :])
        mx = functools.reduce(jnp.maximum, blocks)
        m_new = jnp.maximum(m_old, jnp.max(mx, axis=-1, keepdims=True))
        alpha = jnp.exp2(m_old - m_new)
        ps = [jnp.exp2(blk - m_new) for blk in blocks]
        l_new = alpha * flat(l_sc[:, rows, :]) + functools.reduce(jnp.add, ps)
        p = jnp.concatenate([pc.astype(BF16) for pc in ps], axis=1)
        acc_new = alpha * flat(acc_sc[:, rows, :]) + _dot(p, vt)
        m_sc[:, rows, :] = m_new.reshape(2, nr, HEAD_W)
        l_sc[:, rows, :] = l_new.reshape(2, nr, HEAD_W)
        acc_sc[:, rows, :] = acc_new.reshape(2, nr, HEAD_W)

    n_far = qi * (tq // tk)
    corner_tbl = tbl_ref[0, :near, :near]

    def far(j, carry):
        flag = (j == n_far - 1).astype(F32)
        tile(pl.multiple_of(j * tk, tk), tk, 0, corner_tbl * flag, False)
        return carry

    lax.fori_loop(0, n_far, far, 0)
    for d in range(tq // tkd):
        tile(pl.multiple_of(qi * tq + d * tkd, tkd), tkd, d * tkd, corner_tbl, True)
    lam = _lambda(lq1, lk1, lq2, lk2, lam_init)
    l = jnp.sum(l_sc[...], axis=-1, keepdims=True).reshape(2 * tq, 1)
    acc = acc_sc[...].reshape(2 * tq, HEAD_W)
    o_ref[0] = _diff_finish(acc, l, lam, gs_ref[...], lam_init).astype(o_ref.dtype)


def _attn_prompt(q, k, v, tbl, lams, gs, lam_init, tq, tk, tkd, near):
    b, s, _ = q.shape
    assert s % tq == 0 and tq % tk == 0 and tq % tkd == 0 and tbl.shape == (H_A, tkd, near + tkd)
    assert near == HEAD_W and tk % HEAD_W == 0 and tkd % HEAD_W == 0
    vec = _resident((1, DK_A))
    return pl.pallas_call(
        functools.partial(_attn_prompt_kernel, lam_init=lam_init, tk=tk, tkd=tkd, near=near),
        grid=(b, H_A, s // tq),
        in_specs=[pl.BlockSpec((1, tq, HEAD_W), lambda bi, h, qi: (bi, qi, h)),
                  pl.BlockSpec((1, s, HEAD_W), lambda bi, h, qi: (bi, 0, h)),
                  pl.BlockSpec((1, s, HEAD_W), lambda bi, h, qi: (bi, 0, h)),
                  pl.BlockSpec((1, tkd, near + tkd), lambda bi, h, qi: (h, 0, 0)),
                  vec, vec, vec, vec, _resident((1, HEAD_W))],
        out_specs=pl.BlockSpec((1, tq, HEAD_W), lambda bi, h, qi: (bi, qi, h)),
        out_shape=jax.ShapeDtypeStruct((b, s, SEG_W), BF16),
        scratch_shapes=[pltpu.VMEM((2, tq, HEAD_W), BF16)] + [pltpu.VMEM((2, tq, HEAD_W), F32)] * 3,
        compiler_params=_params(("parallel", "parallel", "arbitrary")),
        name="attn_prompt",
    )(q, k, v, tbl, *lams, gs)


def _log_gamma(h):
    vals = [math.log(1.0 - 2.0 ** (-5.0 - i)) for i in range(H_R)]
    lg = jnp.float32(vals[H_R - 1])
    for i in range(H_R - 2, -1, -1):
        lg = jnp.where(h == i, jnp.float32(vals[i]), lg)
    return lg


def _retention_chunk(q, k, v, state, lg):
    c = q.shape[0]
    ti = lax.broadcasted_iota(jnp.int32, (c, c), 0)
    si = lax.broadcasted_iota(jnp.int32, (c, c), 1)
    dist = (ti - si).astype(F32)
    decay = jnp.where(dist >= 0, jnp.exp(jnp.maximum(dist, 0.0) * lg), 0.0)
    scores = _dot_nt(q, k) * decay
    row = lax.broadcasted_iota(jnp.int32, (c, 1), 0).astype(F32)
    q_dec = (q.astype(F32) * jnp.exp((row + 1.0) * lg)).astype(BF16)
    o = _dot(scores.astype(BF16), v) + _dot(q_dec, state.astype(BF16))
    k_dec = (k.astype(F32) * jnp.exp((c - 1.0 - row) * lg)).astype(BF16)
    new_state = jnp.exp(c * lg) * state + _dot_tn(k_dec, v)
    return o, new_state


def _retention_out(o, sg):
    on = o * lax.rsqrt(jnp.mean(o * o, axis=-1, keepdims=True) + EPS)
    return on * sg.astype(F32)


def _ret_prompt_kernel(q_ref, k_ref, v_ref, sg_ref, o_ref, st_ref, s_sc, d_sc, qd_sc, kd_sc, *, c):
    h = pl.program_id(1)
    ti = pl.program_id(2)
    tb = q_ref.shape[1]

    @pl.when(ti == 0)
    def _():
        lg = _log_gamma(h)
        s_sc[...] = jnp.zeros(s_sc.shape, F32)
        tt = lax.broadcasted_iota(jnp.int32, (c, c), 0)
        ss = lax.broadcasted_iota(jnp.int32, (c, c), 1)
        dist = (tt - ss).astype(F32)
        d_sc[...] = jnp.where(dist >= 0, jnp.exp(jnp.maximum(dist, 0.0) * lg), 0.0)
        row = lax.broadcasted_iota(jnp.int32, (c, HEAD_W), 0).astype(F32)
        qd_sc[...] = jnp.exp((row + 1.0) * lg)
        kd_sc[...] = jnp.exp((c - 1.0 - row) * lg)

    state = s_sc[...]
    g_c = qd_sc[c - 1:c, :]
    for i in range(tb // c):
        rows = slice(i * c, (i + 1) * c)
        q, k, v = q_ref[0, rows, :], k_ref[0, rows, :], v_ref[0, rows, :]
        scores = _dot_nt(q, k) * d_sc[...]
        q_dec = (q.astype(F32) * qd_sc[...]).astype(BF16)
        o = _dot(scores.astype(BF16), v) + _dot(q_dec, state.astype(BF16))
        k_dec = (k.astype(F32) * kd_sc[...]).astype(BF16)
        state = g_c * state + _dot_tn(k_dec, v)
        o_ref[0, rows, :] = _retention_out(o, sg_ref[0, rows, :]).astype(o_ref.dtype)
    s_sc[...] = state

    @pl.when(ti == pl.num_programs(2) - 1)
    def _():
        st_ref[0, 0] = state


def _ret_prompt(qr, kr, vr, sg, t_blk, c):
    b, s, _ = qr.shape
    assert s % t_blk == 0 and t_blk % c == 0
    blk = pl.BlockSpec((1, t_blk, HEAD_W), lambda bi, h, ti: (bi, ti, h))
    return pl.pallas_call(
        functools.partial(_ret_prompt_kernel, c=c),
        grid=(b, H_R, s // t_blk),
        in_specs=[blk, blk, blk, blk],
        out_specs=[blk, pl.BlockSpec((1, 1, HEAD_W, HEAD_W), lambda bi, h, ti: (bi, h, 0, 0))],
        out_shape=[jax.ShapeDtypeStruct((b, s, SEG_W), BF16),
                   jax.ShapeDtypeStruct((b, H_R, HEAD_W, HEAD_W), F32)],
        scratch_shapes=[pltpu.VMEM((HEAD_W, HEAD_W), F32), pltpu.VMEM((c, c), F32),
                        pltpu.VMEM((c, HEAD_W), F32), pltpu.VMEM((c, HEAD_W), F32)],
        compiler_params=_params(("parallel", "parallel", "arbitrary")),
        name="ret_prompt",
    )(qr, kr, vr, sg)


def _mix_sample_kernel(qa_ref, ka_ref, va_ref, ck_ref, cv_ref, tbl_ref, lq1, lk1, lq2, lk2, gs_ref,
                       qr_ref, kr_ref, vr_ref, sg_ref, st_ref, oa_ref, or_ref, so_ref, *, lam_init, near):
    p_len = ck_ref.shape[2]
    l = qa_ref.shape[1]
    lam = _lambda(lq1, lk1, lq2, lk2, lam_init)
    for h in range(H_A):
        cols = slice(h * HEAD_W, (h + 1) * HEAD_W)
        qq = _stack_maps(qa_ref[0, :, cols])
        kc = ck_ref[0, cols, :].astype(BF16)
        vc = cv_ref[0, pl.ds(h, p_len, stride=H_A), :].astype(BF16)
        bias = tbl_ref[h]
        bias2 = jnp.concatenate([bias, bias], axis=0)
        s_c = _dot(qq, kc)
        s_c = jnp.concatenate([s_c[:, :p_len - near], s_c[:, p_len - near:] + bias2[:, :near]], axis=1)
        s_n = _dot_nt(qq, ka_ref[0, :, cols]) + bias2[:, near:]
        m = jnp.maximum(jnp.max(s_c, axis=-1, keepdims=True), jnp.max(s_n, axis=-1, keepdims=True))
        p_c = jnp.exp2(s_c - m)
        p_n = jnp.exp2(s_n - m)
        lsum = jnp.sum(p_c, axis=-1, keepdims=True) + jnp.sum(p_n, axis=-1, keepdims=True)
        acc = _dot(p_c.astype(BF16), vc) + _dot(p_n.astype(BF16), va_ref[0, :, cols])
        oa_ref[0, :, cols] = _diff_finish(acc, lsum, lam, gs_ref[...], lam_init).astype(oa_ref.dtype)
    for h in range(H_R):
        cols = slice(h * HEAD_W, (h + 1) * HEAD_W)
        lg = jnp.float32(math.log(1.0 - 2.0 ** (-5.0 - h)))
        o, new_state = _retention_chunk(qr_ref[0, :, cols], kr_ref[0, :, cols], vr_ref[0, :, cols],
                                        st_ref[0, h], lg)
        or_ref[0, :, cols] = _retention_out(o, sg_ref[0, :, cols]).astype(or_ref.dtype)
        so_ref[0, h] = new_state


def _mix_sample(qa, ka, va, cache_k, cache_v, tbl, lams, gs, qr, kr, vr, sg, state, lam_init, near):
    b, l, _ = qa.shape
    p_len = cache_k.shape[2]
    assert p_len % CHUNK == 0 and l <= CHUNK and p_len >= near
    tok = pl.BlockSpec((1, l, SEG_W), lambda bi: (bi, 0, 0))
    cache_kt = pl.BlockSpec((1, SEG_W, p_len), lambda bi: (bi, 0, 0))
    cache_vh = pl.BlockSpec((1, p_len * H_A, HEAD_W), lambda bi: (bi, 0, 0))
    st = pl.BlockSpec((1, H_R, HEAD_W, HEAD_W), lambda bi: (bi, 0, 0, 0))
    vec = _resident((1, DK_A))
    return pl.pallas_call(
        functools.partial(_mix_sample_kernel, lam_init=lam_init, near=near),
        grid=(b,),
        in_specs=[tok, tok, tok, cache_kt, cache_vh, _resident(tbl.shape), vec, vec, vec, vec,
                  _resident((1, HEAD_W)), tok, tok, tok, tok, st],
        out_specs=[tok, tok, st],
        out_shape=[jax.ShapeDtypeStruct((b, l, SEG_W), BF16), jax.ShapeDtypeStruct((b, l, SEG_W), BF16),
                   jax.ShapeDtypeStruct((b, H_R, HEAD_W, HEAD_W), F32)],
        compiler_params=_params(("parallel",)),
        name="mix_sample",
    )(qa, ka, va, cache_k, cache_v, tbl, *lams, gs, qr, kr, vr, sg, state)


def _ffn_kernel(x_ref, oa_ref, or_ref, mod_ref, g2_ref, wo_ref, wg_ref, wu_ref, wd_ref, y_ref, *, ff_blk):
    g, r, d = x_ref.shape
    m = g * r
    f = wg_ref.shape[1]
    proj = (_dot(oa_ref[...].reshape(m, SEG_W), wo_ref[:SEG_W, :])
            + _dot(or_ref[...].reshape(m, SEG_W), wo_ref[SEG_W:, :]))
    x1 = x_ref[...] + mod_ref[:, 2:3, :] * proj.reshape(g, r, d)
    y = x1 * lax.rsqrt(jnp.mean(x1 * x1, axis=-1, keepdims=True) + EPS) * g2_ref[...]
    h2 = (y * (1.0 + mod_ref[:, 4:5, :]) + mod_ref[:, 3:4, :]).astype(BF16).reshape(m, d)
    acc = jnp.zeros((m, d), F32)
    for c0 in range(0, f, ff_blk):
        c1 = min(c0 + ff_blk, f)
        ff = (_silu(_dot(h2, wg_ref[:, c0:c1])) * _dot(h2, wu_ref[:, c0:c1])).astype(BF16)
        acc = acc + _dot(ff, wd_ref[c0:c1, :])
    y_ref[...] = x1 + mod_ref[:, 5:6, :] * acc.reshape(g, r, d)


def _ffn(x, oa, orr, mod, g2, wo, wg, wu, wd, g_blk, r_blk, ff_blk):
    nb, l, d = x.shape
    assert nb % g_blk == 0 and l % r_blk == 0
    nl = l // r_blk
    tok = lambda w: pl.BlockSpec((g_blk, r_blk, w), lambda i: (i // nl, i % nl, 0))
    return pl.pallas_call(
        functools.partial(_ffn_kernel, ff_blk=ff_blk),
        grid=((nb // g_blk) * nl,),
        in_specs=[tok(d), tok(SEG_W), tok(SEG_W),
                  pl.BlockSpec((g_blk, 6, d), lambda i: (i // nl, 0, 0)),
                  _resident((1, d)), _resident(wo.shape), _resident(wg.shape), _resident(wu.shape),
                  _resident(wd.shape)],
        out_specs=tok(d),
        out_shape=jax.ShapeDtypeStruct((nb, l, d), F32),
        compiler_params=_params(("parallel",)),
        name="outproj_ffn",
    )(x, oa, orr, mod, g2, wo, wg, wu, wd)


def _rope_tables(pos):
    inv = 1.0 / (ROPE_BASE ** jnp.linspace(0.0, 1.0, HEAD_W // 2, dtype=F32))
    ang = pos.astype(F32)[:, None] * inv[None, :]
    sin = jnp.repeat(jnp.sin(ang), 2, axis=1)
    cos = jnp.repeat(jnp.cos(ang), 2, axis=1)
    even = (jnp.arange(HEAD_W) % 2 == 0)[None, :]
    return cos, jnp.where(even, -sin, 0.0), jnp.where(even, 0.0, sin)


def _tile_rows(n, target):
    t = min(n, target)
    while n % t:
        t //= 2
    return t


def kernel(x_prompt, x_sample, c_prompt, c_sample, cache_k, cache_v, state_ret, w_ada, b_ada, g_norm1, g_norm2, w_in, g_q, g_k, lam_q1, lam_k1, lam_q2, lam_k2, g_subln, w_out, w_ff_gate, w_ff_up, w_ff_down, rel_bias):
    depth = w_ada.shape[0]
    b, s, d = x_prompt.shape
    bs, l, _ = x_sample.shape
    p_len = cache_k.shape[2]
    assert w_in.shape[2] == N_SEG * SEG_W and d == 2 * SEG_W

    tm = _tile_rows(s, 512)
    tq = _tile_rows(s, 1024)
    tk = _tile_rows(tq, 1024)
    tkd = _tile_rows(tq, 512)
    ret_blk = _tile_rows(s, 1024)
    ret_c = _tile_rows(ret_blk, 256)
    near = HEAD_W
    assert tq % CHUNK == 0 and tk % CHUNK == 0 and tkd % CHUNK == 0 and near >= BUCKET_THRESHOLDS[-1]

    tab_p = _rope_tables(jnp.arange(s, dtype=jnp.int32))
    tab_s = _rope_tables(p_len + jnp.arange(l, dtype=jnp.int32))
    tbl_p = _bias_table(rel_bias, tkd, near + tkd, near)
    tbl_s = _bias_table(rel_bias, l, near + l, near)
    c_all = jnp.concatenate([c_prompt, c_sample], axis=0)

    xp, xs = x_prompt, x_sample
    outs = [[] for _ in range(6)]
    for layer in range(depth):
        lam_init = _lambda_init(layer)
        mod = _modulation(c_all, w_ada[layer], b_ada[layer]).reshape(b + bs, 6, d)
        mod_p, mod_s = mod[:b], mod[b:]
        g1 = g_norm1[layer].reshape(1, d)
        g2 = g_norm2[layer].reshape(1, d)
        gq_t = jnp.tile(g_q[layer], SEG_W // DK_A).reshape(1, SEG_W)
        gk_t = jnp.tile(g_k[layer], SEG_W // DK_A).reshape(1, SEG_W)
        lams = [v[layer].reshape(1, DK_A) for v in (lam_q1, lam_k1, lam_q2, lam_k2)]
        gs = g_subln[layer].reshape(1, HEAD_W)
        wi = w_in[layer].astype(BF16)
        wo = w_out[layer].astype(BF16)
        wg = w_ff_gate[layer].astype(BF16)
        wu = w_ff_up[layer].astype(BF16)
        wd = w_ff_down[layer].astype(BF16)
        ff_blk = 1024

        kf, vf, qa, ka, va, qr, kr, vr, sg = _inproj(xp, mod_p, g1, wi, gq_t, gk_t, tab_p, 1, tm)
        oa = _attn_prompt(qa, ka, va, tbl_p, lams, gs, lam_init, tq, tk, tkd, near)
        orr, st_p = _ret_prompt(qr, kr, vr, sg, ret_blk, ret_c)
        xp = _ffn(xp, oa, orr, mod_p, g2, wo, wg, wu, wd, 1, tm, ff_blk)

        kfs, vfs, qas, kas, vas, qrs, krs, vrs, sgs = _inproj(xs, mod_s, g1, wi, gq_t, gk_t, tab_s, bs, l)
        ck = jnp.swapaxes(cache_k[layer].reshape(bs, p_len, SEG_W), 1, 2)
        cv = cache_v[layer].reshape(bs, p_len * H_A, HEAD_W)
        oas, ors, st_s = _mix_sample(qas, kas, vas, ck, cv, tbl_s, lams, gs, qrs, krs, vrs, sgs,
                                     state_ret[layer], lam_init, near)
        xs = _ffn(xs, oas, ors, mod_s, g2, wo, wg, wu, wd, bs, l, ff_blk)

        for lst, val in zip(outs, (kf.reshape(b, s, H_A, 2, DK_A), vf.reshape(b, s, H_A, HEAD_W), st_p,
                                   kfs.reshape(bs, l, H_A, 2, DK_A), vfs.reshape(bs, l, H_A, HEAD_W), st_s)):
            lst.append(val)
    return (xp, xs) + tuple(jnp.stack(o) for o in outs)
```

```python
import functools
import math

import jax
import jax.numpy as jnp
from jax import lax
from jax.experimental import pallas as pl
from jax.experimental.pallas import tpu as pltpu

F32 = jnp.float32
BF16 = jnp.bfloat16

CHUNK = 64
H_A = 4
DK_A = 64
H_R = 4
HEAD_W = 128
SEG_W = H_A * HEAD_W
N_SEG = 7
N_BUCKETS = 32
ROPE_BASE = 10000.0
EPS = 1e-6
NEG_INF = -1e30
LOG2E = math.log2(math.e)
BUCKET_THRESHOLDS = (12, 16, 23, 32, 46, 64, 91)
FAR_BUCKET = 15

MXU_W = 256
VMEM_LIMIT = 56 * 1024 * 1024


def _lambda_init(layer):
    return 0.8 - 0.6 * math.exp(-0.3 * layer)


def _params(sem):
    return pltpu.CompilerParams(dimension_semantics=sem, vmem_limit_bytes=VMEM_LIMIT)


def _resident(shape):
    return pl.BlockSpec(shape, lambda *_: (0,) * len(shape), pipeline_mode=pl.Buffered(1))


def _dot(a, b):
    return jnp.dot(a, b, preferred_element_type=F32)


def _dot_nt(a, b):
    return lax.dot_general(a, b, (((1,), (1,)), ((), ())), preferred_element_type=F32)


def _dot_tn(a, b):
    return lax.dot_general(a, b, (((0,), (0,)), ((), ())), preferred_element_type=F32)


def _silu(x):
    return x * jax.nn.sigmoid(x)


def _mod_kernel(c_ref, w_ref, b_ref, o_ref):
    s = _silu(c_ref[...]).astype(BF16)
    o_ref[...] = _dot(s, w_ref[...].astype(BF16)) + b_ref[...]


def _modulation(c_all, w_ada, b_ada):
    n, d = c_all.shape
    e = w_ada.shape[1]
    te = 1536
    assert e % te == 0
    return pl.pallas_call(
        _mod_kernel,
        grid=(e // te,),
        in_specs=[pl.BlockSpec((n, d), lambda j: (0, 0)),
                  pl.BlockSpec((d, te), lambda j: (0, j)),
                  pl.BlockSpec((1, te), lambda j: (0, j))],
        out_specs=pl.BlockSpec((n, te), lambda j: (0, j)),
        out_shape=jax.ShapeDtypeStruct((n, e), F32),
        compiler_params=_params(("parallel",)),
        name="modulation",
    )(c_all, w_ada, b_ada.reshape(1, e))


def _bias_table_kernel(rb_ref, o_ref, *, offset):
    h = pl.program_id(0)
    _, rows, cols = o_ref.shape
    r = lax.broadcasted_iota(jnp.int32, (rows, cols), 0)
    kc = lax.broadcasted_iota(jnp.int32, (rows, cols), 1) - offset
    rel = kc - r
    n = jnp.abs(rel)
    large = jnp.full((rows, cols), N_BUCKETS // 4, jnp.int32)
    for thr in BUCKET_THRESHOLDS:
        large = large + (n >= thr).astype(jnp.int32)
    bucket = jnp.where(rel > 0, N_BUCKETS // 2, 0) + jnp.where(n < N_BUCKETS // 4, n, large)
    val = jnp.zeros((rows, cols), F32)
    for b in range(N_BUCKETS):
        val = jnp.where(bucket == b, rb_ref[b, h], val)
    val = (val - rb_ref[FAR_BUCKET, h]) * LOG2E
    shift = int(math.log2(CHUNK))
    visible = lax.shift_right_arithmetic(kc, shift) <= lax.shift_right_arithmetic(r, shift)
    o_ref[0] = jnp.where(visible, val, NEG_INF)


def _bias_table(rel_bias, rows, cols, offset):
    assert offset >= BUCKET_THRESHOLDS[-1] and offset % CHUNK == 0
    return pl.pallas_call(
        functools.partial(_bias_table_kernel, offset=offset),
        grid=(H_A,),
        in_specs=[pl.BlockSpec(memory_space=pltpu.SMEM)],
        out_specs=pl.BlockSpec((1, rows, cols), lambda h: (h, 0, 0)),
        out_shape=jax.ShapeDtypeStruct((H_A, rows, cols), F32),
        compiler_params=_params(("parallel",)),
        name="bias_table",
    )(rel_bias)


def _inproj_kernel(x_ref, mod_ref, g1_ref, w_ref, gq_ref, gk_ref, cos_ref, sine_ref, sino_ref,
                   kf_ref, vf_ref, qa_ref, ka_ref, va_ref, qr_ref, kr_ref, vr_ref, sg_ref):
    g, r, d = x_ref.shape
    m = g * r
    x = x_ref[...]
    y = x * lax.rsqrt(jnp.mean(x * x, axis=-1, keepdims=True) + EPS) * g1_ref[...]
    hmod = y * (1.0 + mod_ref[:, 1:2, :]) + mod_ref[:, 0:1, :]
    hb = hmod.astype(BF16).reshape(m, d)

    def seg(j):
        return _dot(hb, w_ref[:, j * SEG_W:(j + 1) * SEG_W])

    def put(ref, val):
        ref[...] = val.reshape(g, r, SEG_W).astype(ref.dtype)

    gi = lax.broadcasted_iota(jnp.int32, (MXU_W, MXU_W), 0) // DK_A
    gj = lax.broadcasted_iota(jnp.int32, (MXU_W, MXU_W), 1) // DK_A
    ones_bd = (gi == gj).astype(BF16)

    def group_rms(z):
        sq = (z * z).astype(BF16)
        ms = jnp.concatenate([_dot(sq[:, c:c + MXU_W], ones_bd) for c in range(0, SEG_W, MXU_W)], axis=1)
        return z * lax.rsqrt(ms * (1.0 / DK_A) + EPS)

    def table(ref):
        return jnp.broadcast_to(ref[...][None], (g, r, HEAD_W)).reshape(m, HEAD_W)

    cos, sine, sino = table(cos_ref), table(sine_ref), table(sino_ref)

    def rotate(z):
        outs = []
        for hh in range(H_R):
            zh = z[:, hh * HEAD_W:(hh + 1) * HEAD_W]
            nxt = pltpu.roll(zh, HEAD_W - 1, axis=1)
            prv = pltpu.roll(zh, 1, axis=1)
            outs.append(zh * cos + nxt * sine + prv * sino)
        return jnp.concatenate(outs, axis=1)

    put(qa_ref, group_rms(seg(0)) * gq_ref[...] * (DK_A ** -0.5 * LOG2E))
    ka = group_rms(seg(1)) * gk_ref[...]
    put(kf_ref, ka)
    put(ka_ref, ka)
    va = seg(2)
    for hh in range(H_A):
        vf_ref[:, pl.ds(hh, r, stride=H_A), :] = va[:, hh * HEAD_W:(hh + 1) * HEAD_W].reshape(g, r, HEAD_W)
    put(va_ref, va)
    put(qr_ref, rotate(seg(3)))
    put(kr_ref, rotate(seg(4)) * (HEAD_W ** -0.5))
    put(vr_ref, seg(5))
    put(sg_ref, _silu(seg(6)))


def _inproj(x, mod, g1, w_in, gq_t, gk_t, tables, g_blk, r_blk):
    nb, l, d = x.shape
    assert nb % g_blk == 0 and l % r_blk == 0
    nl = l // r_blk
    tok = lambda w: pl.BlockSpec((g_blk, r_blk, w), lambda i: (i // nl, i % nl, 0))
    tab = pl.BlockSpec((r_blk, HEAD_W), lambda i: (i % nl, 0))
    out = lambda dt: jax.ShapeDtypeStruct((nb, l, SEG_W), dt)
    return pl.pallas_call(
        _inproj_kernel,
        grid=((nb // g_blk) * nl,),
        in_specs=[tok(d),
                  pl.BlockSpec((g_blk, 6, d), lambda i: (i // nl, 0, 0)),
                  _resident((1, d)), _resident(w_in.shape), _resident((1, SEG_W)), _resident((1, SEG_W)),
                  tab, tab, tab],
        out_specs=[tok(SEG_W), pl.BlockSpec((g_blk, r_blk * H_A, HEAD_W), lambda i: (i // nl, i % nl, 0))]
        + [tok(SEG_W)] * 7,
        out_shape=[out(F32), jax.ShapeDtypeStruct((nb, l * H_A, HEAD_W), F32)] + [out(BF16)] * 7,
        compiler_params=_params(("parallel",)),
        name="inproj",
    )(x, mod, g1, w_in, gq_t, gk_t, *tables)


def _stack_maps(q):
    lane = lax.broadcasted_iota(jnp.int32, q.shape, 1)
    zero = jnp.zeros_like(q)
    return jnp.concatenate([jnp.where(lane < DK_A, q, zero), jnp.where(lane >= DK_A, q, zero)], axis=0)


def _lambda(lq1, lk1, lq2, lk2, lam_init):
    s1 = jnp.sum(lq1[...] * lk1[...], axis=-1, keepdims=True)
    s2 = jnp.sum(lq2[...] * lk2[...], axis=-1, keepdims=True)
    return jnp.exp(s1) - jnp.exp(s2) + lam_init


def _diff_finish(acc, l, lam, gs, lam_init):
    t = acc.shape[0] // 2
    o = acc[:t] * (1.0 / l[:t]) - lam * (acc[t:] * (1.0 / l[t:]))
    on = o * lax.rsqrt(jnp.mean(o * o, axis=-1, keepdims=True) + EPS)
    return on * gs * (1.0 - lam_init)


def _attn_prompt_kernel(q_ref, k_ref, v_ref, tbl_ref, lq1, lk1, lq2, lk2, gs_ref, o_ref,
                        qq_sc, m_sc, l_sc, acc_sc, *, lam_init, tk, tkd, near):
    qi = pl.program_id(2)
    tq = q_ref.shape[1]
    q = q_ref[0]
    lane = lax.broadcasted_iota(jnp.int32, q.shape, 1)
    zero = jnp.zeros_like(q)
    qq_sc[0] = jnp.where(lane < DK_A, q, zero)
    qq_sc[1] = jnp.where(lane >= DK_A, q, zero)
    m_sc[...] = jnp.full(m_sc.shape, NEG_INF, F32)
    l_sc[...] = jnp.zeros(l_sc.shape, F32)
    acc_sc[...] = jnp.zeros(acc_sc.shape, F32)

    def tile(start, width, row0, corner, diag):
        nr = tq - row0
        nblk = width // HEAD_W
        rows = slice(row0, tq)
        flat = lambda x: x.reshape(2 * nr, x.shape[-1])
        kt = k_ref[0, pl.ds(start, width), :]
        vt = v_ref[0, pl.ds(start, width), :]
        s = _dot_nt(flat(qq_sc[:, rows, :]), kt)
        blocks = [s[:, c * HEAD_W:(c + 1) * HEAD_W] for c in range(nblk)]

        def add_rows(blk, r0, val):
            r1 = r0 + val.shape[0]
            cuts = [(0, r0, False), (r0, r1, True), (r1, nr + r0, False), (nr + r0, nr + r1, True),
                    (nr + r1, 2 * nr, False)]
            return jnp.concatenate([blk[a:b] + val if hit else blk[a:b] for a, b, hit in cuts if b > a], axis=0)

        if diag:
            blocks = [add_rows(blk, 0, tbl_ref[0, :, near + c * HEAD_W:near + (c + 1) * HEAD_W])
                      for c, blk in enumerate(blocks)]
        corner_row0 = width if diag else 0
        if corner_row0 < nr:
            blocks[-1] = add_rows(blocks[-1], corner_row0, corner)
        m_old = flat(m_sc[:, rows, :])
        mx = functools.reduce(jnp.maximum, blocks)
        m_new = jnp.maximum(m_old, jnp.max(mx, axis=-1, keepdims=True))
        alpha = jnp.exp2(m_old - m_new)
        ps = [jnp.exp2(blk - m_new) for blk in blocks]
        l_new = alpha * flat(l_sc[:, rows, :]) + functools.reduce(jnp.add, ps)
        p = jnp.concatenate([pc.astype(BF16) for pc in ps], axis=1)
        acc_new = alpha * flat(acc_sc[:, rows, :]) + _dot(p, vt)
        m_sc[:, rows, :] = m_new.reshape(2, nr, HEAD_W)
        l_sc[:, rows, :] = l_new.reshape(2, nr, HEAD_W)
        acc_sc[:, rows, :] = acc_new.reshape(2, nr, HEAD_W)

    n_far = qi * (tq // tk)
    corner_tbl = tbl_ref[0, :near, :near]

    def far_tile(j):
        flag = (j == n_far - 1).astype(F32)
        tile(pl.multiple_of(j * tk, tk), tk, 0, corner_tbl * flag, False)

    odd = jnp.bitwise_and(n_far, 1)

    @pl.when(odd == 1)
    def _():
        far_tile(0)

    def far_pair(i, carry):
        far_tile(odd + 2 * i)
        far_tile(odd + 2 * i + 1)
        return carry

    lax.fori_loop(0, lax.shift_right_logical(n_far, 1), far_pair, 0)
    for d in range(tq // tkd):
        tile(pl.multiple_of(qi * tq + d * tkd, tkd), tkd, d * tkd, corner_tbl, True)
    lam = _lambda(lq1, lk1, lq2, lk2, lam_init)
    l = jnp.sum(l_sc[...], axis=-1, keepdims=True).reshape(2 * tq, 1)
    acc = acc_sc[...].reshape(2 * tq, HEAD_W)
    o_ref[0] = _diff_finish(acc, l, lam, gs_ref[...], lam_init).astype(o_ref.dtype)


def _attn_prompt(q, k, v, tbl, lams, gs, lam_init, tq, tk, tkd, near):
    b, s, _ = q.shape
    assert s % tq == 0 and tq % tk == 0 and tq % tkd == 0 and tbl.shape == (H_A, tkd, near + tkd)
    assert near == HEAD_W and tk % HEAD_W == 0 and tkd % HEAD_W == 0
    vec = _resident((1, DK_A))
    return pl.pallas_call(
        functools.partial(_attn_prompt_kernel, lam_init=lam_init, tk=tk, tkd=tkd, near=near),
        grid=(b, H_A, s // tq),
        in_specs=[pl.BlockSpec((1, tq, HEAD_W), lambda bi, h, qi: (bi, qi, h)),
                  pl.BlockSpec((1, s, HEAD_W), lambda bi, h, qi: (bi, 0, h)),
                  pl.BlockSpec((1, s, HEAD_W), lambda bi, h, qi: (bi, 0, h)),
                  pl.BlockSpec((1, tkd, near + tkd), lambda bi, h, qi: (h, 0, 0)),
                  vec, vec, vec, vec, _resident((1, HEAD_W))],
        out_specs=pl.BlockSpec((1, tq, HEAD_W), lambda bi, h, qi: (bi, qi, h)),
        out_shape=jax.ShapeDtypeStruct((b, s, SEG_W), BF16),
        scratch_shapes=[pltpu.VMEM((2, tq, HEAD_W), BF16)] + [pltpu.VMEM((2, tq, HEAD_W), F32)] * 3,
        compiler_params=_params(("parallel", "parallel", "arbitrary")),
        name="attn_prompt",
    )(q, k, v, tbl, *lams, gs)


def _log_gamma(h):
    vals = [math.log(1.0 - 2.0 ** (-5.0 - i)) for i in range(H_R)]
    lg = jnp.float32(vals[H_R - 1])
    for i in range(H_R - 2, -1, -1):
        lg = jnp.where(h == i, jnp.float32(vals[i]), lg)
    return lg


def _retention_chunk(q, k, v, state, lg):
    c = q.shape[0]
    ti = lax.broadcasted_iota(jnp.int32, (c, c), 0)
    si = lax.broadcasted_iota(jnp.int32, (c, c), 1)
    dist = (ti - si).astype(F32)
    decay = jnp.where(dist >= 0, jnp.exp(jnp.maximum(dist, 0.0) * lg), 0.0)
    scores = _dot_nt(q, k) * decay
    row = lax.broadcasted_iota(jnp.int32, (c, 1), 0).astype(F32)
    q_dec = (q.astype(F32) * jnp.exp((row + 1.0) * lg)).astype(BF16)
    o = _dot(scores.astype(BF16), v) + _dot(q_dec, state.astype(BF16))
    k_dec = (k.astype(F32) * jnp.exp((c - 1.0 - row) * lg)).astype(BF16)
    new_state = jnp.exp(c * lg) * state + _dot_tn(k_dec, v)
    return o, new_state


def _retention_out(o, sg):
    on = o * lax.rsqrt(jnp.mean(o * o, axis=-1, keepdims=True) + EPS)
    return on * sg.astype(F32)


def _ret_prompt_kernel(q_ref, k_ref, v_ref, sg_ref, o_ref, st_ref, s_sc, d_sc, qd_sc, kd_sc, *, c):
    h = pl.program_id(1)
    ti = pl.program_id(2)
    tb = q_ref.shape[1]

    @pl.when(ti == 0)
    def _():
        lg = _log_gamma(h)
        s_sc[...] = jnp.zeros(s_sc.shape, F32)
        tt = lax.broadcasted_iota(jnp.int32, (c, c), 0)
        ss = lax.broadcasted_iota(jnp.int32, (c, c), 1)
        dist = (tt - ss).astype(F32)
        d_sc[...] = jnp.where(dist >= 0, jnp.exp(jnp.maximum(dist, 0.0) * lg), 0.0)
        row = lax.broadcasted_iota(jnp.int32, (c, HEAD_W), 0).astype(F32)
        qd_sc[...] = jnp.exp((row + 1.0) * lg)
        kd_sc[...] = jnp.exp((c - 1.0 - row) * lg)

    state = s_sc[...]
    g_c = qd_sc[c - 1:c, :]
    for i in range(tb // c):
        rows = slice(i * c, (i + 1) * c)
        q, k, v = q_ref[0, rows, :], k_ref[0, rows, :], v_ref[0, rows, :]
        scores = _dot_nt(q, k) * d_sc[...]
        q_dec = (q.astype(F32) * qd_sc[...]).astype(BF16)
        o = _dot(scores.astype(BF16), v) + _dot(q_dec, state.astype(BF16))
        k_dec = (k.astype(F32) * kd_sc[...]).astype(BF16)
        state = g_c * state + _dot_tn(k_dec, v)
        o_ref[0, rows, :] = _retention_out(o, sg_ref[0, rows, :]).astype(o_ref.dtype)
    s_sc[...] = state

    @pl.when(ti == pl.num_programs(2) - 1)
    def _():
        st_ref[0, 0] = state


def _ret_prompt(qr, kr, vr, sg, t_blk, c):
    b, s, _ = qr.shape
    assert s % t_blk == 0 and t_blk % c == 0
    blk = pl.BlockSpec((1, t_blk, HEAD_W), lambda bi, h, ti: (bi, ti, h))
    return pl.pallas_call(
        functools.partial(_ret_prompt_kernel, c=c),
        grid=(b, H_R, s // t_blk),
        in_specs=[blk, blk, blk, blk],
        out_specs=[blk, pl.BlockSpec((1, 1, HEAD_W, HEAD_W), lambda bi, h, ti: (bi, h, 0, 0))],
        out_shape=[jax.ShapeDtypeStruct((b, s, SEG_W), BF16),
                   jax.ShapeDtypeStruct((b, H_R, HEAD_W, HEAD_W), F32)],
        scratch_shapes=[pltpu.VMEM((HEAD_W, HEAD_W), F32), pltpu.VMEM((c, c), F32),
                        pltpu.VMEM((c, HEAD_W), F32), pltpu.VMEM((c, HEAD_W), F32)],
        compiler_params=_params(("parallel", "parallel", "arbitrary")),
        name="ret_prompt",
    )(qr, kr, vr, sg)


def _mix_sample_kernel(qa_ref, ka_ref, va_ref, ck_ref, cv_ref, tbl_ref, lq1, lk1, lq2, lk2, gs_ref,
                       qr_ref, kr_ref, vr_ref, sg_ref, st_ref, oa_ref, or_ref, so_ref, *, lam_init, near):
    p_len = ck_ref.shape[2]
    l = qa_ref.shape[1]
    lam = _lambda(lq1, lk1, lq2, lk2, lam_init)
    for h in range(H_A):
        cols = slice(h * HEAD_W, (h + 1) * HEAD_W)
        qq = _stack_maps(qa_ref[0, :, cols])
        kc = ck_ref[0, cols, :].astype(BF16)
        vc = cv_ref[0, pl.ds(h, p_len, stride=H_A), :].astype(BF16)
        bias = tbl_ref[h]
        bias2 = jnp.concatenate([bias, bias], axis=0)
        s_c = _dot(qq, kc)
        s_c = jnp.concatenate([s_c[:, :p_len - near], s_c[:, p_len - near:] + bias2[:, :near]], axis=1)
        s_n = _dot_nt(qq, ka_ref[0, :, cols]) + bias2[:, near:]
        m = jnp.maximum(jnp.max(s_c, axis=-1, keepdims=True), jnp.max(s_n, axis=-1, keepdims=True))
        p_c = jnp.exp2(s_c - m)
        p_n = jnp.exp2(s_n - m)
        lsum = jnp.sum(p_c, axis=-1, keepdims=True) + jnp.sum(p_n, axis=-1, keepdims=True)
        acc = _dot(p_c.astype(BF16), vc) + _dot(p_n.astype(BF16), va_ref[0, :, cols])
        oa_ref[0, :, cols] = _diff_finish(acc, lsum, lam, gs_ref[...], lam_init).astype(oa_ref.dtype)
    for h in range(H_R):
        cols = slice(h * HEAD_W, (h + 1) * HEAD_W)
        lg = jnp.float32(math.log(1.0 - 2.0 ** (-5.0 - h)))
        o, new_state = _retention_chunk(qr_ref[0, :, cols], kr_ref[0, :, cols], vr_ref[0, :, cols],
                                        st_ref[0, h], lg)
        or_ref[0, :, cols] = _retention_out(o, sg_ref[0, :, cols]).astype(or_ref.dtype)
        so_ref[0, h] = new_state


def _mix_sample(qa, ka, va, cache_k, cache_v, tbl, lams, gs, qr, kr, vr, sg, state, lam_init, near):
    b, l, _ = qa.shape
    p_len = cache_k.shape[2]
    assert p_len % CHUNK == 0 and l <= CHUNK and p_len >= near
    tok = pl.BlockSpec((1, l, SEG_W), lambda bi: (bi, 0, 0))
    cache_kt = pl.BlockSpec((1, SEG_W, p_len), lambda bi: (bi, 0, 0))
    cache_vh = pl.BlockSpec((1, p_len * H_A, HEAD_W), lambda bi: (bi, 0, 0))
    st = pl.BlockSpec((1, H_R, HEAD_W, HEAD_W), lambda bi: (bi, 0, 0, 0))
    vec = _resident((1, DK_A))
    return pl.pallas_call(
        functools.partial(_mix_sample_kernel, lam_init=lam_init, near=near),
        grid=(b,),
        in_specs=[tok, tok, tok, cache_kt, cache_vh, _resident(tbl.shape), vec, vec, vec, vec,
                  _resident((1, HEAD_W)), tok, tok, tok, tok, st],
        out_specs=[tok, tok, st],
        out_shape=[jax.ShapeDtypeStruct((b, l, SEG_W), BF16), jax.ShapeDtypeStruct((b, l, SEG_W), BF16),
                   jax.ShapeDtypeStruct((b, H_R, HEAD_W, HEAD_W), F32)],
        compiler_params=_params(("parallel",)),
        name="mix_sample",
    )(qa, ka, va, cache_k, cache_v, tbl, *lams, gs, qr, kr, vr, sg, state)


def _ffn_kernel(x_ref, oa_ref, or_ref, mod_ref, g2_ref, wo_ref, wg_ref, wu_ref, wd_ref, y_ref, *, ff_blk):
    g, r, d = x_ref.shape
    m = g * r
    f = wg_ref.shape[1]
    proj = (_dot(oa_ref[...].reshape(m, SEG_W), wo_ref[:SEG_W, :])
            + _dot(or_ref[...].reshape(m, SEG_W), wo_ref[SEG_W:, :]))
    x1 = x_ref[...] + mod_ref[:, 2:3, :] * proj.reshape(g, r, d)
    y = x1 * lax.rsqrt(jnp.mean(x1 * x1, axis=-1, keepdims=True) + EPS) * g2_ref[...]
    h2 = (y * (1.0 + mod_ref[:, 4:5, :]) + mod_ref[:, 3:4, :]).astype(BF16).reshape(m, d)
    acc = jnp.zeros((m, d), F32)
    for c0 in range(0, f, ff_blk):
        c1 = min(c0 + ff_blk, f)
        ff = (_silu(_dot(h2, wg_ref[:, c0:c1])) * _dot(h2, wu_ref[:, c0:c1])).astype(BF16)
        acc = acc + _dot(ff, wd_ref[c0:c1, :])
    y_ref[...] = x1 + mod_ref[:, 5:6, :] * acc.reshape(g, r, d)


def _ffn(x, oa, orr, mod, g2, wo, wg, wu, wd, g_blk, r_blk, ff_blk):
    nb, l, d = x.shape
    assert nb % g_blk == 0 and l % r_blk == 0
    nl = l // r_blk
    tok = lambda w: pl.BlockSpec((g_blk, r_blk, w), lambda i: (i // nl, i % nl, 0))
    return pl.pallas_call(
        functools.partial(_ffn_kernel, ff_blk=ff_blk),
        grid=((nb // g_blk) * nl,),
        in_specs=[tok(d), tok(SEG_W), tok(SEG_W),
                  pl.BlockSpec((g_blk, 6, d), lambda i: (i // nl, 0, 0)),
                  _resident((1, d)), _resident(wo.shape), _resident(wg.shape), _resident(wu.shape),
                  _resident(wd.shape)],
        out_specs=tok(d),
        out_shape=jax.ShapeDtypeStruct((nb, l, d), F32),
        compiler_params=_params(("parallel",)),
        name="outproj_ffn",
    )(x, oa, orr, mod, g2, wo, wg, wu, wd)


def _rope_tables(pos):
    inv = 1.0 / (ROPE_BASE ** jnp.linspace(0.0, 1.0, HEAD_W // 2, dtype=F32))
    ang = pos.astype(F32)[:, None] * jnp.repeat(inv, 2)[None, :]
    sin, cos = jnp.sin(ang), jnp.cos(ang)
    even = (jnp.arange(HEAD_W) % 2 == 0)[None, :]
    return cos, jnp.where(even, -sin, 0.0), jnp.where(even, 0.0, sin)


def _tile_rows(n, target):
    t = min(n, target)
    while n % t:
        t //= 2
    return t


def kernel(x_prompt, x_sample, c_prompt, c_sample, cache_k, cache_v, state_ret, w_ada, b_ada, g_norm1, g_norm2, w_in, g_q, g_k, lam_q1, lam_k1, lam_q2, lam_k2, g_subln, w_out, w_ff_gate, w_ff_up, w_ff_down, rel_bias):
    depth = w_ada.shape[0]
    b, s, d = x_prompt.shape
    bs, l, _ = x_sample.shape
    p_len = cache_k.shape[2]
    assert w_in.shape[2] == N_SEG * SEG_W and d == 2 * SEG_W

    tm = _tile_rows(s, 512)
    tq = _tile_rows(s, 1024)
    tk = _tile_rows(tq, 1024)
    tkd = _tile_rows(tq, 512)
    ret_blk = _tile_rows(s, 1024)
    ret_c = _tile_rows(ret_blk, 256)
    near = HEAD_W
    assert tq % CHUNK == 0 and tk % CHUNK == 0 and tkd % CHUNK == 0 and near >= BUCKET_THRESHOLDS[-1]

    tab_p = _rope_tables(jnp.arange(s, dtype=jnp.int32))
    tab_s = _rope_tables(p_len + jnp.arange(l, dtype=jnp.int32))
    tbl_p = _bias_table(rel_bias, tkd, near + tkd, near)
    tbl_s = _bias_table(rel_bias, l, near + l, near)
    c_all = jnp.concatenate([c_prompt, c_sample], axis=0)

    xp, xs = x_prompt, x_sample
    outs = [[] for _ in range(6)]
    for layer in range(depth):
        lam_init = _lambda_init(layer)
        mod = _modulation(c_all, w_ada[layer], b_ada[layer]).reshape(b + bs, 6, d)
        mod_p, mod_s = mod[:b], mod[b:]
        g1 = g_norm1[layer].reshape(1, d)
        g2 = g_norm2[layer].reshape(1, d)
        gq_t = jnp.tile(g_q[layer], SEG_W // DK_A).reshape(1, SEG_W)
        gk_t = jnp.tile(g_k[layer], SEG_W // DK_A).reshape(1, SEG_W)
        lams = [v[layer].reshape(1, DK_A) for v in (lam_q1, lam_k1, lam_q2, lam_k2)]
        gs = g_subln[layer].reshape(1, HEAD_W)
        wi = w_in[layer].astype(BF16)
        wo = w_out[layer].astype(BF16)
        wg = w_ff_gate[layer].astype(BF16)
        wu = w_ff_up[layer].astype(BF16)
        wd = w_ff_down[layer].astype(BF16)
        ff_blk = 1024

        kf, vf, qa, ka, va, qr, kr, vr, sg = _inproj(xp, mod_p, g1, wi, gq_t, gk_t, tab_p, 1, tm)
        oa = _attn_prompt(qa, ka, va, tbl_p, lams, gs, lam_init, tq, tk, tkd, near)
        orr, st_p = _ret_prompt(qr, kr, vr, sg, ret_blk, ret_c)
        xp = _ffn(xp, oa, orr, mod_p, g2, wo, wg, wu, wd, 1, tm, ff_blk)

        kfs, vfs, qas, kas, vas, qrs, krs, vrs, sgs = _inproj(xs, mod_s, g1, wi, gq_t, gk_t, tab_s, bs, l)
        ck = jnp.swapaxes(cache_k[layer].reshape(bs, p_len, SEG_W), 1, 2)
        cv = cache_v[layer].reshape(bs, p_len * H_A, HEAD_W)
        oas, ors, st_s = _mix_sample(qas, kas, vas, ck, cv, tbl_s, lams, gs, qrs, krs, vrs, sgs,
                                     state_ret[layer], lam_init, near)
        xs = _ffn(xs, oas, ors, mod_s, g2, wo, wg, wu, wd, bs, l, ff_blk)

        for lst, val in zip(outs, (kf.reshape(b, s, H_A, 2, DK_A), vf.reshape(b, s, H_A, HEAD_W), st_p,
                                   kfs.reshape(bs, l, H_A, 2, DK_A), vfs.reshape(bs, l, H_A, HEAD_W), st_s)):
            lst.append(val)
    return (xp, xs) + tuple(jnp.stack(o) for o in outs)
```

```python
import functools
import math

import jax
import jax.numpy as jnp
from jax import lax
from jax.experimental import pallas as pl
from jax.experimental.pallas import tpu as pltpu

F32 = jnp.float32
BF16 = jnp.bfloat16

CHUNK = 64
H_A = 4
DK_A = 64
H_R = 4
HEAD_W = 128
SEG_W = H_A * HEAD_W
N_SEG = 7
N_BUCKETS = 32
ROPE_BASE = 10000.0
EPS = 1e-6
NEG_INF = -1e30
LOG2E = math.log2(math.e)
BUCKET_THRESHOLDS = (12, 16, 23, 32, 46, 64, 91)
FAR_BUCKET = 15
MAX_LOGIT_SPREAD = 100.0

MXU_W = 256
VMEM_LIMIT = 56 * 1024 * 1024


def _lambda_init(layer):
    return 0.8 - 0.6 * math.exp(-0.3 * layer)


def _params(sem):
    return pltpu.CompilerParams(dimension_semantics=sem, vmem_limit_bytes=VMEM_LIMIT)


def _resident(shape):
    return pl.BlockSpec(shape, lambda *_: (0,) * len(shape), pipeline_mode=pl.Buffered(1))


def _dot(a, b):
    return jnp.dot(a, b, preferred_element_type=F32)


def _dot_nt(a, b):
    return lax.dot_general(a, b, (((1,), (1,)), ((), ())), preferred_element_type=F32)


def _dot_tn(a, b):
    return lax.dot_general(a, b, (((0,), (0,)), ((), ())), preferred_element_type=F32)


def _silu(x):
    return x * jax.nn.sigmoid(x)


def _mod_kernel(c_ref, w_ref, b_ref, o_ref):
    s = _silu(c_ref[...]).astype(BF16)
    o_ref[...] = _dot(s, w_ref[...].astype(BF16)) + b_ref[...]


def _modulation(c_all, w_ada, b_ada):
    n, d = c_all.shape
    e = w_ada.shape[1]
    te = 1536
    assert e % te == 0
    return pl.pallas_call(
        _mod_kernel,
        grid=(e // te,),
        in_specs=[pl.BlockSpec((n, d), lambda j: (0, 0)),
                  pl.BlockSpec((d, te), lambda j: (0, j)),
                  pl.BlockSpec((1, te), lambda j: (0, j))],
        out_specs=pl.BlockSpec((n, te), lambda j: (0, j)),
        out_shape=jax.ShapeDtypeStruct((n, e), F32),
        compiler_params=_params(("parallel",)),
        name="modulation",
    )(c_all, w_ada, b_ada.reshape(1, e))


def _bias_table_kernel(rb_ref, o_ref, *, offset):
    h = pl.program_id(0)
    _, rows, cols = o_ref.shape
    r = lax.broadcasted_iota(jnp.int32, (rows, cols), 0)
    kc = lax.broadcasted_iota(jnp.int32, (rows, cols), 1) - offset
    rel = kc - r
    n = jnp.abs(rel)
    large = jnp.full((rows, cols), N_BUCKETS // 4, jnp.int32)
    for thr in BUCKET_THRESHOLDS:
        large = large + (n >= thr).astype(jnp.int32)
    bucket = jnp.where(rel > 0, N_BUCKETS // 2, 0) + jnp.where(n < N_BUCKETS // 4, n, large)
    val = jnp.zeros((rows, cols), F32)
    for b in range(N_BUCKETS):
        val = jnp.where(bucket == b, rb_ref[b, h], val)
    val = (val - rb_ref[FAR_BUCKET, h]) * LOG2E
    shift = int(math.log2(CHUNK))
    visible = lax.shift_right_arithmetic(kc, shift) <= lax.shift_right_arithmetic(r, shift)
    o_ref[0] = jnp.where(visible, val, NEG_INF)


def _bias_table(rel_bias, rows, cols, offset):
    assert offset >= BUCKET_THRESHOLDS[-1] and offset % CHUNK == 0
    return pl.pallas_call(
        functools.partial(_bias_table_kernel, offset=offset),
        grid=(H_A,),
        in_specs=[pl.BlockSpec(memory_space=pltpu.SMEM)],
        out_specs=pl.BlockSpec((1, rows, cols), lambda h: (h, 0, 0)),
        out_shape=jax.ShapeDtypeStruct((H_A, rows, cols), F32),
        compiler_params=_params(("parallel",)),
        name="bias_table",
    )(rel_bias)


def _inproj_kernel(x_ref, mod_ref, g1_ref, w_ref, gq_ref, gk_ref, cos_ref, sine_ref, sino_ref,
                   kf_ref, vf_ref, qa_ref, ka_ref, va_ref, qr_ref, kr_ref, vr_ref, sg_ref):
    g, r, d = x_ref.shape
    m = g * r
    x = x_ref[...]
    y = x * lax.rsqrt(jnp.mean(x * x, axis=-1, keepdims=True) + EPS) * g1_ref[...]
    hmod = y * (1.0 + mod_ref[:, 1:2, :]) + mod_ref[:, 0:1, :]
    hb = hmod.astype(BF16).reshape(m, d)

    def seg(j):
        return _dot(hb, w_ref[:, j * SEG_W:(j + 1) * SEG_W])

    def put(ref, val):
        ref[...] = val.reshape(g, r, SEG_W).astype(ref.dtype)

    gi = lax.broadcasted_iota(jnp.int32, (MXU_W, MXU_W), 0) // DK_A
    gj = lax.broadcasted_iota(jnp.int32, (MXU_W, MXU_W), 1) // DK_A
    ones_bd = (gi == gj).astype(BF16)

    def group_rms(z):
        sq = (z * z).astype(BF16)
        ms = jnp.concatenate([_dot(sq[:, c:c + MXU_W], ones_bd) for c in range(0, SEG_W, MXU_W)], axis=1)
        return z * lax.rsqrt(ms * (1.0 / DK_A) + EPS)

    def table(ref):
        return jnp.broadcast_to(ref[...][None], (g, r, HEAD_W)).reshape(m, HEAD_W)

    cos, sine, sino = table(cos_ref), table(sine_ref), table(sino_ref)

    def rotate(z):
        outs = []
        for hh in range(H_R):
            zh = z[:, hh * HEAD_W:(hh + 1) * HEAD_W]
            nxt = pltpu.roll(zh, HEAD_W - 1, axis=1)
            prv = pltpu.roll(zh, 1, axis=1)
            outs.append(zh * cos + nxt * sine + prv * sino)
        return jnp.concatenate(outs, axis=1)

    put(qa_ref, group_rms(seg(0)) * gq_ref[...] * (DK_A ** -0.5 * LOG2E))
    ka = group_rms(seg(1)) * gk_ref[...]
    put(kf_ref, ka)
    put(ka_ref, ka)
    va = seg(2)
    for hh in range(H_A):
        vf_ref[:, pl.ds(hh, r, stride=H_A), :] = va[:, hh * HEAD_W:(hh + 1) * HEAD_W].reshape(g, r, HEAD_W)
    put(va_ref, va)
    put(qr_ref, rotate(seg(3)))
    put(kr_ref, rotate(seg(4)) * (HEAD_W ** -0.5))
    put(vr_ref, seg(5))
    put(sg_ref, _silu(seg(6)))


def _inproj(x, mod, g1, w_in, gq_t, gk_t, tables, g_blk, r_blk):
    nb, l, d = x.shape
    assert nb % g_blk == 0 and l % r_blk == 0
    nl = l // r_blk
    tok = lambda w: pl.BlockSpec((g_blk, r_blk, w), lambda i: (i // nl, i % nl, 0))
    tab = pl.BlockSpec((r_blk, HEAD_W), lambda i: (i % nl, 0))
    out = lambda dt: jax.ShapeDtypeStruct((nb, l, SEG_W), dt)
    return pl.pallas_call(
        _inproj_kernel,
        grid=((nb // g_blk) * nl,),
        in_specs=[tok(d),
                  pl.BlockSpec((g_blk, 6, d), lambda i: (i // nl, 0, 0)),
                  _resident((1, d)), _resident(w_in.shape), _resident((1, SEG_W)), _resident((1, SEG_W)),
                  tab, tab, tab],
        out_specs=[tok(SEG_W), pl.BlockSpec((g_blk, r_blk * H_A, HEAD_W), lambda i: (i // nl, i % nl, 0))]
        + [tok(SEG_W)] * 7,
        out_shape=[out(F32), jax.ShapeDtypeStruct((nb, l * H_A, HEAD_W), F32)] + [out(BF16)] * 7,
        compiler_params=_params(("parallel",)),
        name="inproj",
    )(x, mod, g1, w_in, gq_t, gk_t, *tables)


def _stack_maps(q):
    lane = lax.broadcasted_iota(jnp.int32, q.shape, 1)
    zero = jnp.zeros_like(q)
    return jnp.concatenate([jnp.where(lane < DK_A, q, zero), jnp.where(lane >= DK_A, q, zero)], axis=0)


def _lambda(lq1, lk1, lq2, lk2, lam_init):
    s1 = jnp.sum(lq1[...] * lk1[...], axis=-1, keepdims=True)
    s2 = jnp.sum(lq2[...] * lk2[...], axis=-1, keepdims=True)
    return jnp.exp(s1) - jnp.exp(s2) + lam_init


def _diff_finish(acc, l, lam, gs, lam_init):
    t = acc.shape[0] // 2
    o = acc[:t] * (1.0 / l[:t]) - lam * (acc[t:] * (1.0 / l[t:]))
    on = o * lax.rsqrt(jnp.mean(o * o, axis=-1, keepdims=True) + EPS)
    return on * gs * (1.0 - lam_init)


def _attn_prompt_kernel(bound_ref, q_ref, k_ref, v_ref, tbl_ref, lq1, lk1, lq2, lk2, gs_ref, o_ref,
                        qq_sc, m_sc, l_sc, acc_sc, *, lam_init, tk, tkd, near):
    qi = pl.program_id(2)
    tq = q_ref.shape[1]
    q = q_ref[0]
    lane = lax.broadcasted_iota(jnp.int32, q.shape, 1)
    zero = jnp.zeros_like(q)
    qq_sc[0] = jnp.where(lane < DK_A, q, zero)
    qq_sc[1] = jnp.where(lane >= DK_A, q, zero)
    l_sc[...] = jnp.zeros(l_sc.shape, F32)
    acc_sc[...] = jnp.zeros(acc_sc.shape, F32)

    def tile(start, width, row0, corner, diag, bounded):
        nr = tq - row0
        nblk = width // HEAD_W
        rows = slice(row0, tq)
        flat = lambda x: x.reshape(2 * nr, x.shape[-1])
        kt = k_ref[0, pl.ds(start, width), :]
        vt = v_ref[0, pl.ds(start, width), :]
        s = _dot_nt(flat(qq_sc[:, rows, :]), kt)
        blocks = [s[:, c * HEAD_W:(c + 1) * HEAD_W] for c in range(nblk)]

        def add_rows(blk, r0, val):
            r1 = r0 + val.shape[0]
            cuts = [(0, r0, False), (r0, r1, True), (r1, nr + r0, False), (nr + r0, nr + r1, True),
                    (nr + r1, 2 * nr, False)]
            return jnp.concatenate([blk[a:b] + val if hit else blk[a:b] for a, b, hit in cuts if b > a], axis=0)

        if diag:
            blocks = [add_rows(blk, 0, tbl_ref[0, :, near + c * HEAD_W:near + (c + 1) * HEAD_W])
                      for c, blk in enumerate(blocks)]
        corner_row0 = width if diag else 0
        if corner_row0 < nr:
            blocks[-1] = add_rows(blocks[-1], corner_row0, corner)
        if bounded:
            ps = [jnp.exp2(blk - bound_ref[0]) for blk in blocks]
            l_new = flat(l_sc[:, rows, :]) + functools.reduce(jnp.add, ps)
            p = jnp.concatenate([pc.astype(BF16) for pc in ps], axis=1)
            acc_new = flat(acc_sc[:, rows, :]) + _dot(p, vt)
            l_sc[:, rows, :] = l_new.reshape(2, nr, HEAD_W)
            acc_sc[:, rows, :] = acc_new.reshape(2, nr, HEAD_W)
            return
        m_old = flat(m_sc[:, rows, :])
        mx = functools.reduce(jnp.maximum, blocks)
        m_new = jnp.maximum(m_old, jnp.max(mx, axis=-1, keepdims=True))
        alpha = jnp.exp2(m_old - m_new)
        ps = [jnp.exp2(blk - m_new) for blk in blocks]
        l_new = alpha * flat(l_sc[:, rows, :]) + functools.reduce(jnp.add, ps)
        p = jnp.concatenate([pc.astype(BF16) for pc in ps], axis=1)
        acc_new = alpha * flat(acc_sc[:, rows, :]) + _dot(p, vt)
        m_sc[:, rows, :] = m_new.reshape(2, nr, HEAD_W)
        l_sc[:, rows, :] = l_new.reshape(2, nr, HEAD_W)
        acc_sc[:, rows, :] = acc_new.reshape(2, nr, HEAD_W)

    n_far = qi * (tq // tk)
    corner_tbl = tbl_ref[0, :near, :near]

    def run(bounded):
        def far_tile(j):
            flag = (j == n_far - 1).astype(F32)
            tile(pl.multiple_of(j * tk, tk), tk, 0, corner_tbl * flag, False, bounded)

        odd = jnp.bitwise_and(n_far, 1)

        @pl.when(odd == 1)
        def _():
            far_tile(0)

        def far_pair(i, carry):
            far_tile(odd + 2 * i)
            far_tile(odd + 2 * i + 1)
            return carry

        lax.fori_loop(0, lax.shift_right_logical(n_far, 1), far_pair, 0)
        for d in range(tq // tkd):
            tile(pl.multiple_of(qi * tq + d * tkd, tkd), tkd, d * tkd, corner_tbl, True, bounded)

    @pl.when(bound_ref[1] > 0.5)
    def _():
        run(True)

    @pl.when(bound_ref[1] <= 0.5)
    def _():
        m_sc[...] = jnp.full(m_sc.shape, NEG_INF, F32)
        run(False)

    lam = _lambda(lq1, lk1, lq2, lk2, lam_init)
    l = jnp.sum(l_sc[...], axis=-1, keepdims=True).reshape(2 * tq, 1)
    acc = acc_sc[...].reshape(2 * tq, HEAD_W)
    o_ref[0] = _diff_finish(acc, l, lam, gs_ref[...], lam_init).astype(o_ref.dtype)


def _attn_prompt(bound, q, k, v, tbl, lams, gs, lam_init, tq, tk, tkd, near):
    b, s, _ = q.shape
    assert s % tq == 0 and tq % tk == 0 and tq % tkd == 0 and tbl.shape == (H_A, tkd, near + tkd)
    assert near == HEAD_W and tk % HEAD_W == 0 and tkd % HEAD_W == 0
    vec = _resident((1, DK_A))
    return pl.pallas_call(
        functools.partial(_attn_prompt_kernel, lam_init=lam_init, tk=tk, tkd=tkd, near=near),
        grid=(b, H_A, s // tq),
        in_specs=[pl.BlockSpec(memory_space=pltpu.SMEM),
                  pl.BlockSpec((1, tq, HEAD_W), lambda bi, h, qi: (bi, qi, h)),
                  pl.BlockSpec((1, s, HEAD_W), lambda bi, h, qi: (bi, 0, h)),
                  pl.BlockSpec((1, s, HEAD_W), lambda bi, h, qi: (bi, 0, h)),
                  pl.BlockSpec((1, tkd, near + tkd), lambda bi, h, qi: (h, 0, 0)),
                  vec, vec, vec, vec, _resident((1, HEAD_W))],
        out_specs=pl.BlockSpec((1, tq, HEAD_W), lambda bi, h, qi: (bi, qi, h)),
        out_shape=jax.ShapeDtypeStruct((b, s, SEG_W), BF16),
        scratch_shapes=[pltpu.VMEM((2, tq, HEAD_W), BF16)] + [pltpu.VMEM((2, tq, HEAD_W), F32)] * 3,
        compiler_params=_params(("parallel", "parallel", "arbitrary")),
        name="attn_prompt",
    )(bound, q, k, v, tbl, *lams, gs)


def _log_gamma(h):
    vals = [math.log(1.0 - 2.0 ** (-5.0 - i)) for i in range(H_R)]
    lg = jnp.float32(vals[H_R - 1])
    for i in range(H_R - 2, -1, -1):
        lg = jnp.where(h == i, jnp.float32(vals[i]), lg)
    return lg


def _retention_chunk(q, k, v, state, lg):
    c = q.shape[0]
    ti = lax.broadcasted_iota(jnp.int32, (c, c), 0)
    si = lax.broadcasted_iota(jnp.int32, (c, c), 1)
    dist = (ti - si).astype(F32)
    decay = jnp.where(dist >= 0, jnp.exp(jnp.maximum(dist, 0.0) * lg), 0.0)
    scores = _dot_nt(q, k) * decay
    row = lax.broadcasted_iota(jnp.int32, (c, 1), 0).astype(F32)
    q_dec = (q.astype(F32) * jnp.exp((row + 1.0) * lg)).astype(BF16)
    o = _dot(scores.astype(BF16), v) + _dot(q_dec, state.astype(BF16))
    k_dec = (k.astype(F32) * jnp.exp((c - 1.0 - row) * lg)).astype(BF16)
    new_state = jnp.exp(c * lg) * state + _dot_tn(k_dec, v)
    return o, new_state


def _retention_out(o, sg):
    on = o * lax.rsqrt(jnp.mean(o * o, axis=-1, keepdims=True) + EPS)
    return on * sg.astype(F32)


def _ret_prompt_kernel(q_ref, k_ref, v_ref, sg_ref, o_ref, st_ref, s_sc, d_sc, qd_sc, kd_sc, *, c):
    h = pl.program_id(1)
    ti = pl.program_id(2)
    tb = q_ref.shape[1]

    @pl.when(ti == 0)
    def _():
        lg = _log_gamma(h)
        s_sc[...] = jnp.zeros(s_sc.shape, F32)
        tt = lax.broadcasted_iota(jnp.int32, (c, c), 0)
        ss = lax.broadcasted_iota(jnp.int32, (c, c), 1)
        dist = (tt - ss).astype(F32)
        d_sc[...] = jnp.where(dist >= 0, jnp.exp(jnp.maximum(dist, 0.0) * lg), 0.0)
        row = lax.broadcasted_iota(jnp.int32, (c, HEAD_W), 0).astype(F32)
        qd_sc[...] = jnp.exp((row + 1.0) * lg)
        kd_sc[...] = jnp.exp((c - 1.0 - row) * lg)

    state = s_sc[...]
    g_c = qd_sc[c - 1:c, :]
    for i in range(tb // c):
        rows = slice(i * c, (i + 1) * c)
        q, k, v = q_ref[0, rows, :], k_ref[0, rows, :], v_ref[0, rows, :]
        scores = _dot_nt(q, k) * d_sc[...]
        q_dec = (q.astype(F32) * qd_sc[...]).astype(BF16)
        o = _dot(scores.astype(BF16), v) + _dot(q_dec, state.astype(BF16))
        k_dec = (k.astype(F32) * kd_sc[...]).astype(BF16)
        state = g_c * state + _dot_tn(k_dec, v)
        o_ref[0, rows, :] = _retention_out(o, sg_ref[0, rows, :]).astype(o_ref.dtype)
    s_sc[...] = state

    @pl.when(ti == pl.num_programs(2) - 1)
    def _():
        st_ref[0, 0] = state


def _ret_prompt(qr, kr, vr, sg, t_blk, c):
    b, s, _ = qr.shape
    assert s % t_blk == 0 and t_blk % c == 0
    blk = pl.BlockSpec((1, t_blk, HEAD_W), lambda bi, h, ti: (bi, ti, h))
    return pl.pallas_call(
        functools.partial(_ret_prompt_kernel, c=c),
        grid=(b, H_R, s // t_blk),
        in_specs=[blk, blk, blk, blk],
        out_specs=[blk, pl.BlockSpec((1, 1, HEAD_W, HEAD_W), lambda bi, h, ti: (bi, h, 0, 0))],
        out_shape=[jax.ShapeDtypeStruct((b, s, SEG_W), BF16),
                   jax.ShapeDtypeStruct((b, H_R, HEAD_W, HEAD_W), F32)],
        scratch_shapes=[pltpu.VMEM((HEAD_W, HEAD_W), F32), pltpu.VMEM((c, c), F32),
                        pltpu.VMEM((c, HEAD_W), F32), pltpu.VMEM((c, HEAD_W), F32)],
        compiler_params=_params(("parallel", "parallel", "arbitrary")),
        name="ret_prompt",
    )(qr, kr, vr, sg)


def _mix_sample_kernel(qa_ref, ka_ref, va_ref, ck_ref, cv_ref, tbl_ref, lq1, lk1, lq2, lk2, gs_ref,
                       qr_ref, kr_ref, vr_ref, sg_ref, st_ref, oa_ref, or_ref, so_ref, *, lam_init, near):
    p_len = ck_ref.shape[2]
    l = qa_ref.shape[1]
    lam = _lambda(lq1, lk1, lq2, lk2, lam_init)
    for h in range(H_A):
        cols = slice(h * HEAD_W, (h + 1) * HEAD_W)
        qq = _stack_maps(qa_ref[0, :, cols])
        kc = ck_ref[0, cols, :].astype(BF16)
        vc = cv_ref[0, pl.ds(h, p_len, stride=H_A), :].astype(BF16)
        bias = tbl_ref[h]
        bias2 = jnp.concatenate([bias, bias], axis=0)
        s_c = _dot(qq, kc)
        s_c = jnp.concatenate([s_c[:, :p_len - near], s_c[:, p_len - near:] + bias2[:, :near]], axis=1)
        s_n = _dot_nt(qq, ka_ref[0, :, cols]) + bias2[:, near:]
        m = jnp.maximum(jnp.max(s_c, axis=-1, keepdims=True), jnp.max(s_n, axis=-1, keepdims=True))
        p_c = jnp.exp2(s_c - m)
        p_n = jnp.exp2(s_n - m)
        lsum = jnp.sum(p_c, axis=-1, keepdims=True) + jnp.sum(p_n, axis=-1, keepdims=True)
        acc = _dot(p_c.astype(BF16), vc) + _dot(p_n.astype(BF16), va_ref[0, :, cols])
        oa_ref[0, :, cols] = _diff_finish(acc, lsum, lam, gs_ref[...], lam_init).astype(oa_ref.dtype)
    for h in range(H_R):
        cols = slice(h * HEAD_W, (h + 1) * HEAD_W)
        lg = jnp.float32(math.log(1.0 - 2.0 ** (-5.0 - h)))
        o, new_state = _retention_chunk(qr_ref[0, :, cols], kr_ref[0, :, cols], vr_ref[0, :, cols],
                                        st_ref[0, h], lg)
        or_ref[0, :, cols] = _retention_out(o, sg_ref[0, :, cols]).astype(or_ref.dtype)
        so_ref[0, h] = new_state


def _mix_sample(qa, ka, va, cache_k, cache_v, tbl, lams, gs, qr, kr, vr, sg, state, lam_init, near):
    b, l, _ = qa.shape
    p_len = cache_k.shape[2]
    assert p_len % CHUNK == 0 and l <= CHUNK and p_len >= near
    tok = pl.BlockSpec((1, l, SEG_W), lambda bi: (bi, 0, 0))
    cache_kt = pl.BlockSpec((1, SEG_W, p_len), lambda bi: (bi, 0, 0))
    cache_vh = pl.BlockSpec((1, p_len * H_A, HEAD_W), lambda bi: (bi, 0, 0))
    st = pl.BlockSpec((1, H_R, HEAD_W, HEAD_W), lambda bi: (bi, 0, 0, 0))
    vec = _resident((1, DK_A))
    return pl.pallas_call(
        functools.partial(_mix_sample_kernel, lam_init=lam_init, near=near),
        grid=(b,),
        in_specs=[tok, tok, tok, cache_kt, cache_vh, _resident(tbl.shape), vec, vec, vec, vec,
                  _resident((1, HEAD_W)), tok, tok, tok, tok, st],
        out_specs=[tok, tok, st],
        out_shape=[jax.ShapeDtypeStruct((b, l, SEG_W), BF16), jax.ShapeDtypeStruct((b, l, SEG_W), BF16),
                   jax.ShapeDtypeStruct((b, H_R, HEAD_W, HEAD_W), F32)],
        compiler_params=_params(("parallel",)),
        name="mix_sample",
    )(qa, ka, va, cache_k, cache_v, tbl, *lams, gs, qr, kr, vr, sg, state)


def _ffn_kernel(x_ref, oa_ref, or_ref, mod_ref, g2_ref, wo_ref, wg_ref, wu_ref, wd_ref, y_ref, *, ff_blk):
    g, r, d = x_ref.shape
    m = g * r
    f = wg_ref.shape[1]
    proj = (_dot(oa_ref[...].reshape(m, SEG_W), wo_ref[:SEG_W, :])
            + _dot(or_ref[...].reshape(m, SEG_W), wo_ref[SEG_W:, :]))
    x1 = x_ref[...] + mod_ref[:, 2:3, :] * proj.reshape(g, r, d)
    y = x1 * lax.rsqrt(jnp.mean(x1 * x1, axis=-1, keepdims=True) + EPS) * g2_ref[...]
    h2 = (y * (1.0 + mod_ref[:, 4:5, :]) + mod_ref[:, 3:4, :]).astype(BF16).reshape(m, d)
    acc = jnp.zeros((m, d), F32)
    for c0 in range(0, f, ff_blk):
        c1 = min(c0 + ff_blk, f)
        ff = (_silu(_dot(h2, wg_ref[:, c0:c1])) * _dot(h2, wu_ref[:, c0:c1])).astype(BF16)
        acc = acc + _dot(ff, wd_ref[c0:c1, :])
    y_ref[...] = x1 + mod_ref[:, 5:6, :] * acc.reshape(g, r, d)


def _ffn(x, oa, orr, mod, g2, wo, wg, wu, wd, g_blk, r_blk, ff_blk):
    nb, l, d = x.shape
    assert nb % g_blk == 0 and l % r_blk == 0
    nl = l // r_blk
    tok = lambda w: pl.BlockSpec((g_blk, r_blk, w), lambda i: (i // nl, i % nl, 0))
    return pl.pallas_call(
        functools.partial(_ffn_kernel, ff_blk=ff_blk),
        grid=((nb // g_blk) * nl,),
        in_specs=[tok(d), tok(SEG_W), tok(SEG_W),
                  pl.BlockSpec((g_blk, 6, d), lambda i: (i // nl, 0, 0)),
                  _resident((1, d)), _resident(wo.shape), _resident(wg.shape), _resident(wu.shape),
                  _resident(wd.shape)],
        out_specs=tok(d),
        out_shape=jax.ShapeDtypeStruct((nb, l, d), F32),
        compiler_params=_params(("parallel",)),
        name="outproj_ffn",
    )(x, oa, orr, mod, g2, wo, wg, wu, wd)


def _rope_tables(pos):
    inv = 1.0 / (ROPE_BASE ** jnp.linspace(0.0, 1.0, HEAD_W // 2, dtype=F32))
    ang = pos.astype(F32)[:, None] * jnp.repeat(inv, 2)[None, :]
    sin, cos = jnp.sin(ang), jnp.cos(ang)
    even = (jnp.arange(HEAD_W) % 2 == 0)[None, :]
    return cos, jnp.where(even, -sin, 0.0), jnp.where(even, 0.0, sin)


def _logit_bound(g_q, g_k, rel_bias):
    slack = 1.02
    dot_max = slack * DK_A * (DK_A ** -0.5 * LOG2E) * jnp.max(jnp.abs(g_q)) * jnp.max(jnp.abs(g_k))
    rb = (rel_bias - rel_bias[FAR_BUCKET][None, :]) * LOG2E
    b_hi = jnp.maximum(jnp.max(rb), 0.0)
    b_lo = jnp.minimum(jnp.min(rb), 0.0)
    ok = (2.0 * dot_max + (b_hi - b_lo)) <= MAX_LOGIT_SPREAD
    return jnp.stack([dot_max + b_hi, ok.astype(F32)])


def _tile_rows(n, target):
    t = min(n, target)
    while n % t:
        t //= 2
    return t


def kernel(x_prompt, x_sample, c_prompt, c_sample, cache_k, cache_v, state_ret, w_ada, b_ada, g_norm1, g_norm2, w_in, g_q, g_k, lam_q1, lam_k1, lam_q2, lam_k2, g_subln, w_out, w_ff_gate, w_ff_up, w_ff_down, rel_bias):
    depth = w_ada.shape[0]
    b, s, d = x_prompt.shape
    bs, l, _ = x_sample.shape
    p_len = cache_k.shape[2]
    assert w_in.shape[2] == N_SEG * SEG_W and d == 2 * SEG_W

    tm = _tile_rows(s, 512)
    tq = _tile_rows(s, 1024)
    tk = _tile_rows(tq, 1024)
    tkd = _tile_rows(tq, 512)
    ret_blk = _tile_rows(s, 1024)
    ret_c = _tile_rows(ret_blk, 256)
    near = HEAD_W
    assert tq % CHUNK == 0 and tk % CHUNK == 0 and tkd % CHUNK == 0 and near >= BUCKET_THRESHOLDS[-1]

    tab_p = _rope_tables(jnp.arange(s, dtype=jnp.int32))
    tab_s = _rope_tables(p_len + jnp.arange(l, dtype=jnp.int32))
    tbl_p = _bias_table(rel_bias, tkd, near + tkd, near)
    tbl_s = _bias_table(rel_bias, l, near + l, near)
    c_all = jnp.concatenate([c_prompt, c_sample], axis=0)

    xp, xs = x_prompt, x_sample
    outs = [[] for _ in range(6)]
    for layer in range(depth):
        lam_init = _lambda_init(layer)
        mod = _modulation(c_all, w_ada[layer], b_ada[layer]).reshape(b + bs, 6, d)
        mod_p, mod_s = mod[:b], mod[b:]
        g1 = g_norm1[layer].reshape(1, d)
        g2 = g_norm2[layer].reshape(1, d)
        gq_t = jnp.tile(g_q[layer], SEG_W // DK_A).reshape(1, SEG_W)
        gk_t = jnp.tile(g_k[layer], SEG_W // DK_A).reshape(1, SEG_W)
        lams = [v[layer].reshape(1, DK_A) for v in (lam_q1, lam_k1, lam_q2, lam_k2)]
        gs = g_subln[layer].reshape(1, HEAD_W)
        wi = w_in[layer].astype(BF16)
        wo = w_out[layer].astype(BF16)
        wg = w_ff_gate[layer].astype(BF16)
        wu = w_ff_up[layer].astype(BF16)
        wd = w_ff_down[layer].astype(BF16)
        ff_blk = 1024

        kf, vf, qa, ka, va, qr, kr, vr, sg = _inproj(xp, mod_p, g1, wi, gq_t, gk_t, tab_p, 1, tm)
        bound = _logit_bound(g_q[layer], g_k[layer], rel_bias)
        oa = _attn_prompt(bound, qa, ka, va, tbl_p, lams, gs, lam_init, tq, tk, tkd, near)
        orr, st_p = _ret_prompt(qr, kr, vr, sg, ret_blk, ret_c)
        xp = _ffn(xp, oa, orr, mod_p, g2, wo, wg, wu, wd, 1, tm, ff_blk)

        kfs, vfs, qas, kas, vas, qrs, krs, vrs, sgs = _inproj(xs, mod_s, g1, wi, gq_t, gk_t, tab_s, bs, l)
        ck = jnp.swapaxes(cache_k[layer].reshape(bs, p_len, SEG_W), 1, 2)
        cv = cache_v[layer].reshape(bs, p_len * H_A, HEAD_W)
        oas, ors, st_s = _mix_sample(qas, kas, vas, ck, cv, tbl_s, lams, gs, qrs, krs, vrs, sgs,
                                     state_ret[layer], lam_init, near)
        xs = _ffn(xs, oas, ors, mod_s, g2, wo, wg, wu, wd, bs, l, ff_blk)

        for lst, val in zip(outs, (kf.reshape(b, s, H_A, 2, DK_A), vf.reshape(b, s, H_A, HEAD_W), st_p,
                                   kfs.reshape(bs, l, H_A, 2, DK_A), vfs.reshape(bs, l, H_A, HEAD_W), st_s)):
            lst.append(val)
    return (xp, xs) + tuple(jnp.stack(o) for o in outs)
```

```python
import functools
import math

import jax
import jax.numpy as jnp
from jax import lax
from jax.experimental import pallas as pl
from jax.experimental.pallas import tpu as pltpu

F32 = jnp.float32
BF16 = jnp.bfloat16

CHUNK = 64
H_A = 4
DK_A = 64
H_R = 4
HEAD_W = 128
SEG_W = H_A * HEAD_W
N_SEG = 7
N_BUCKETS = 32
ROPE_BASE = 10000.0
EPS = 1e-6
NEG_INF = -1e30
LOG2E = math.log2(math.e)
BUCKET_THRESHOLDS = (12, 16, 23, 32, 46, 64, 91)
FAR_BUCKET = 15
MAX_LOGIT_SPREAD = 100.0

MXU_W = 256
VMEM_LIMIT = 56 * 1024 * 1024


def _lambda_init(layer):
    return 0.8 - 0.6 * math.exp(-0.3 * layer)


def _params(sem):
    return pltpu.CompilerParams(dimension_semantics=sem, vmem_limit_bytes=VMEM_LIMIT)


def _resident(shape):
    return pl.BlockSpec(shape, lambda *_: (0,) * len(shape), pipeline_mode=pl.Buffered(1))


def _dot(a, b):
    return jnp.dot(a, b, preferred_element_type=F32)


def _dot_nt(a, b):
    return lax.dot_general(a, b, (((1,), (1,)), ((), ())), preferred_element_type=F32)


def _dot_tn(a, b):
    return lax.dot_general(a, b, (((0,), (0,)), ((), ())), preferred_element_type=F32)


def _silu(x):
    return x * jax.nn.sigmoid(x)


def _mod_kernel(c_ref, w_ref, b_ref, o_ref):
    s = _silu(c_ref[...]).astype(BF16)
    o_ref[...] = _dot(s, w_ref[...].astype(BF16)) + b_ref[...]


def _modulation(c_all, w_ada, b_ada):
    n, d = c_all.shape
    e = w_ada.shape[1]
    te = 1536
    assert e % te == 0
    return pl.pallas_call(
        _mod_kernel,
        grid=(e // te,),
        in_specs=[pl.BlockSpec((n, d), lambda j: (0, 0)),
                  pl.BlockSpec((d, te), lambda j: (0, j)),
                  pl.BlockSpec((1, te), lambda j: (0, j))],
        out_specs=pl.BlockSpec((n, te), lambda j: (0, j)),
        out_shape=jax.ShapeDtypeStruct((n, e), F32),
        compiler_params=_params(("parallel",)),
        name="modulation",
    )(c_all, w_ada, b_ada.reshape(1, e))


def _bias_table_kernel(rb_ref, o_ref, *, offset):
    h = pl.program_id(0)
    _, rows, cols = o_ref.shape
    r = lax.broadcasted_iota(jnp.int32, (rows, cols), 0)
    kc = lax.broadcasted_iota(jnp.int32, (rows, cols), 1) - offset
    rel = kc - r
    n = jnp.abs(rel)
    large = jnp.full((rows, cols), N_BUCKETS // 4, jnp.int32)
    for thr in BUCKET_THRESHOLDS:
        large = large + (n >= thr).astype(jnp.int32)
    bucket = jnp.where(rel > 0, N_BUCKETS // 2, 0) + jnp.where(n < N_BUCKETS // 4, n, large)
    val = jnp.zeros((rows, cols), F32)
    for b in range(N_BUCKETS):
        val = jnp.where(bucket == b, rb_ref[b, h], val)
    val = (val - rb_ref[FAR_BUCKET, h]) * LOG2E
    shift = int(math.log2(CHUNK))
    visible = lax.shift_right_arithmetic(kc, shift) <= lax.shift_right_arithmetic(r, shift)
    o_ref[0] = jnp.where(visible, val, NEG_INF)


def _bias_table(rel_bias, rows, cols, offset):
    assert offset >= BUCKET_THRESHOLDS[-1] and offset % CHUNK == 0
    return pl.pallas_call(
        functools.partial(_bias_table_kernel, offset=offset),
        grid=(H_A,),
        in_specs=[pl.BlockSpec(memory_space=pltpu.SMEM)],
        out_specs=pl.BlockSpec((1, rows, cols), lambda h: (h, 0, 0)),
        out_shape=jax.ShapeDtypeStruct((H_A, rows, cols), F32),
        compiler_params=_params(("parallel",)),
        name="bias_table",
    )(rel_bias)


def _inproj_kernel(x_ref, mod_ref, g1_ref, w_ref, gq_ref, gk_ref, cos_ref, sine_ref, sino_ref,
                   kf_ref, vf_ref, qa_ref, ka_ref, va_ref, qr_ref, kr_ref, vr_ref, sg_ref):
    g, r, d = x_ref.shape
    m = g * r
    x = x_ref[...]
    y = x * lax.rsqrt(jnp.mean(x * x, axis=-1, keepdims=True) + EPS) * g1_ref[...]
    hmod = y * (1.0 + mod_ref[:, 1:2, :]) + mod_ref[:, 0:1, :]
    hb = hmod.astype(BF16).reshape(m, d)

    def seg(j):
        return _dot(hb, w_ref[:, j * SEG_W:(j + 1) * SEG_W])

    def put(ref, val):
        ref[...] = val.reshape(g, r, SEG_W).astype(ref.dtype)

    gi = lax.broadcasted_iota(jnp.int32, (MXU_W, MXU_W), 0) // DK_A
    gj = lax.broadcasted_iota(jnp.int32, (MXU_W, MXU_W), 1) // DK_A
    ones_bd = (gi == gj).astype(BF16)

    def group_rms(z):
        sq = (z * z).astype(BF16)
        ms = jnp.concatenate([_dot(sq[:, c:c + MXU_W], ones_bd) for c in range(0, SEG_W, MXU_W)], axis=1)
        return z * lax.rsqrt(ms * (1.0 / DK_A) + EPS)

    def table(ref):
        return jnp.broadcast_to(ref[...][None], (g, r, HEAD_W)).reshape(m, HEAD_W)

    cos, sine, sino = table(cos_ref), table(sine_ref), table(sino_ref)

    def rotate(z):
        outs = []
        for hh in range(H_R):
            zh = z[:, hh * HEAD_W:(hh + 1) * HEAD_W]
            nxt = pltpu.roll(zh, HEAD_W - 1, axis=1)
            prv = pltpu.roll(zh, 1, axis=1)
            outs.append(zh * cos + nxt * sine + prv * sino)
        return jnp.concatenate(outs, axis=1)

    put(qa_ref, group_rms(seg(0)) * gq_ref[...] * (DK_A ** -0.5 * LOG2E))
    ka = group_rms(seg(1)) * gk_ref[...]
    put(kf_ref, ka)
    put(ka_ref, ka)
    va = seg(2)
    for hh in range(H_A):
        vf_ref[:, pl.ds(hh, r, stride=H_A), :] = va[:, hh * HEAD_W:(hh + 1) * HEAD_W].reshape(g, r, HEAD_W)
    put(va_ref, va)
    put(qr_ref, rotate(seg(3)))
    put(kr_ref, rotate(seg(4)) * (HEAD_W ** -0.5))
    put(vr_ref, seg(5))
    put(sg_ref, _silu(seg(6)))


def _inproj(x, mod, g1, w_in, gq_t, gk_t, tables, g_blk, r_blk):
    nb, l, d = x.shape
    assert nb % g_blk == 0 and l % r_blk == 0
    nl = l // r_blk
    tok = lambda w: pl.BlockSpec((g_blk, r_blk, w), lambda i: (i // nl, i % nl, 0))
    tab = pl.BlockSpec((r_blk, HEAD_W), lambda i: (i % nl, 0))
    out = lambda dt: jax.ShapeDtypeStruct((nb, l, SEG_W), dt)
    return pl.pallas_call(
        _inproj_kernel,
        grid=((nb // g_blk) * nl,),
        in_specs=[tok(d),
                  pl.BlockSpec((g_blk, 6, d), lambda i: (i // nl, 0, 0)),
                  _resident((1, d)), _resident(w_in.shape), _resident((1, SEG_W)), _resident((1, SEG_W)),
                  tab, tab, tab],
        out_specs=[tok(SEG_W), pl.BlockSpec((g_blk, r_blk * H_A, HEAD_W), lambda i: (i // nl, i % nl, 0))]
        + [tok(SEG_W)] * 7,
        out_shape=[out(F32), jax.ShapeDtypeStruct((nb, l * H_A, HEAD_W), F32)] + [out(BF16)] * 7,
        compiler_params=_params(("parallel",)),
        name="inproj",
    )(x, mod, g1, w_in, gq_t, gk_t, *tables)


def _stack_maps(q):
    lane = lax.broadcasted_iota(jnp.int32, q.shape, 1)
    zero = jnp.zeros_like(q)
    return jnp.concatenate([jnp.where(lane < DK_A, q, zero), jnp.where(lane >= DK_A, q, zero)], axis=0)


def _lambda(lq1, lk1, lq2, lk2, lam_init):
    s1 = jnp.sum(lq1[...] * lk1[...], axis=-1, keepdims=True)
    s2 = jnp.sum(lq2[...] * lk2[...], axis=-1, keepdims=True)
    return jnp.exp(s1) - jnp.exp(s2) + lam_init


def _diff_finish(acc, l, lam, gs, lam_init):
    t = acc.shape[0] // 2
    o = acc[:t] * (1.0 / l[:t]) - lam * (acc[t:] * (1.0 / l[t:]))
    on = o * lax.rsqrt(jnp.mean(o * o, axis=-1, keepdims=True) + EPS)
    return on * gs * (1.0 - lam_init)


def _attn_prompt_kernel(bound_ref, q_ref, k_ref, v_ref, tbl_ref, lq1, lk1, lq2, lk2, gs_ref, o_ref,
                        qq_sc, m_sc, l_sc, acc_sc, *, lam_init, tk, tkd, near):
    qi = pl.program_id(2)
    tq = q_ref.shape[1]
    q = q_ref[0]
    lane = lax.broadcasted_iota(jnp.int32, q.shape, 1)
    zero = jnp.zeros_like(q)
    qq_sc[0] = jnp.where(lane < DK_A, q, zero)
    qq_sc[1] = jnp.where(lane >= DK_A, q, zero)
    l_sc[...] = jnp.zeros(l_sc.shape, F32)
    acc_sc[...] = jnp.zeros(acc_sc.shape, F32)

    def tile(start, width, row0, corner, diag, bounded):
        nr = tq - row0
        nblk = width // HEAD_W
        rows = slice(row0, tq)
        flat = lambda x: x.reshape(2 * nr, x.shape[-1])
        kt = k_ref[0, pl.ds(start, width), :]
        vt = v_ref[0, pl.ds(start, width), :]
        s = _dot_nt(flat(qq_sc[:, rows, :]), kt)
        blocks = [s[:, c * HEAD_W:(c + 1) * HEAD_W] for c in range(nblk)]

        def add_rows(blk, r0, val):
            r1 = r0 + val.shape[0]
            cuts = [(0, r0, False), (r0, r1, True), (r1, nr + r0, False), (nr + r0, nr + r1, True),
                    (nr + r1, 2 * nr, False)]
            return jnp.concatenate([blk[a:b] + val if hit else blk[a:b] for a, b, hit in cuts if b > a], axis=0)

        if diag:
            blocks = [add_rows(blk, 0, tbl_ref[0, :, near + c * HEAD_W:near + (c + 1) * HEAD_W])
                      for c, blk in enumerate(blocks)]
        corner_row0 = width if diag else 0
        if corner_row0 < nr:
            blocks[-1] = add_rows(blocks[-1], corner_row0, corner)
        if bounded:
            ps = [jnp.exp2(blk - bound_ref[0]) for blk in blocks]
            l_new = flat(l_sc[:, rows, :]) + functools.reduce(jnp.add, ps)
            p = jnp.concatenate([pc.astype(BF16) for pc in ps], axis=1)
            acc_new = flat(acc_sc[:, rows, :]) + _dot(p, vt)
            l_sc[:, rows, :] = l_new.reshape(2, nr, HEAD_W)
            acc_sc[:, rows, :] = acc_new.reshape(2, nr, HEAD_W)
            return
        m_old = flat(m_sc[:, rows, :])
        mx = functools.reduce(jnp.maximum, blocks)
        m_new = jnp.maximum(m_old, jnp.max(mx, axis=-1, keepdims=True))
        alpha = jnp.exp2(m_old - m_new)
        ps = [jnp.exp2(blk - m_new) for blk in blocks]
        l_new = alpha * flat(l_sc[:, rows, :]) + functools.reduce(jnp.add, ps)
        p = jnp.concatenate([pc.astype(BF16) for pc in ps], axis=1)
        acc_new = alpha * flat(acc_sc[:, rows, :]) + _dot(p, vt)
        m_sc[:, rows, :] = m_new.reshape(2, nr, HEAD_W)
        l_sc[:, rows, :] = l_new.reshape(2, nr, HEAD_W)
        acc_sc[:, rows, :] = acc_new.reshape(2, nr, HEAD_W)

    n_far = qi * (tq // tk)
    corner_tbl = tbl_ref[0, :near, :near]

    def run(bounded):
        def far_tile(j):
            flag = (j == n_far - 1).astype(F32)
            tile(pl.multiple_of(j * tk, tk), tk, 0, corner_tbl * flag, False, bounded)

        odd = jnp.bitwise_and(n_far, 1)

        @pl.when(odd == 1)
        def _():
            far_tile(0)

        def far_pair(i, carry):
            far_tile(odd + 2 * i)
            far_tile(odd + 2 * i + 1)
            return carry

        lax.fori_loop(0, lax.shift_right_logical(n_far, 1), far_pair, 0)
        for d in range(tq // tkd):
            tile(pl.multiple_of(qi * tq + d * tkd, tkd), tkd, d * tkd, corner_tbl, True, bounded)

    @pl.when(bound_ref[1] > 0.5)
    def _():
        run(True)

    @pl.when(bound_ref[1] <= 0.5)
    def _():
        m_sc[...] = jnp.full(m_sc.shape, NEG_INF, F32)
        run(False)

    lam = _lambda(lq1, lk1, lq2, lk2, lam_init)
    l = jnp.sum(l_sc[...], axis=-1, keepdims=True).reshape(2 * tq, 1)
    acc = acc_sc[...].reshape(2 * tq, HEAD_W)
    o_ref[0] = _diff_finish(acc, l, lam, gs_ref[...], lam_init).astype(o_ref.dtype)


def _attn_prompt(bound, q, k, v, tbl, lams, gs, lam_init, tq, tk, tkd, near):
    b, s, _ = q.shape
    assert s % tq == 0 and tq % tk == 0 and tq % tkd == 0 and tbl.shape == (H_A, tkd, near + tkd)
    assert near == HEAD_W and tk % HEAD_W == 0 and tkd % HEAD_W == 0
    vec = _resident((1, DK_A))
    return pl.pallas_call(
        functools.partial(_attn_prompt_kernel, lam_init=lam_init, tk=tk, tkd=tkd, near=near),
        grid=(b, H_A, s // tq),
        in_specs=[pl.BlockSpec(memory_space=pltpu.SMEM),
                  pl.BlockSpec((1, tq, HEAD_W), lambda bi, h, qi: (bi, qi, h)),
                  pl.BlockSpec((1, s, HEAD_W), lambda bi, h, qi: (bi, 0, h)),
                  pl.BlockSpec((1, s, HEAD_W), lambda bi, h, qi: (bi, 0, h)),
                  pl.BlockSpec((1, tkd, near + tkd), lambda bi, h, qi: (h, 0, 0)),
                  vec, vec, vec, vec, _resident((1, HEAD_W))],
        out_specs=pl.BlockSpec((1, tq, HEAD_W), lambda bi, h, qi: (bi, qi, h)),
        out_shape=jax.ShapeDtypeStruct((b, s, SEG_W), BF16),
        scratch_shapes=[pltpu.VMEM((2, tq, HEAD_W), BF16)] + [pltpu.VMEM((2, tq, HEAD_W), F32)] * 3,
        compiler_params=_params(("parallel", "parallel", "arbitrary")),
        name="attn_prompt",
    )(bound, q, k, v, tbl, *lams, gs)


def _log_gamma(h):
    vals = [math.log(1.0 - 2.0 ** (-5.0 - i)) for i in range(H_R)]
    lg = jnp.float32(vals[H_R - 1])
    for i in range(H_R - 2, -1, -1):
        lg = jnp.where(h == i, jnp.float32(vals[i]), lg)
    return lg


def _retention_chunk(q, k, v, state, lg):
    c = q.shape[0]
    ti = lax.broadcasted_iota(jnp.int32, (c, c), 0)
    si = lax.broadcasted_iota(jnp.int32, (c, c), 1)
    dist = (ti - si).astype(F32)
    decay = jnp.where(dist >= 0, jnp.exp(jnp.maximum(dist, 0.0) * lg), 0.0)
    scores = _dot_nt(q, k) * decay
    row = lax.broadcasted_iota(jnp.int32, (c, 1), 0).astype(F32)
    q_dec = (q.astype(F32) * jnp.exp((row + 1.0) * lg)).astype(BF16)
    o = _dot(scores.astype(BF16), v) + _dot(q_dec, state.astype(BF16))
    k_dec = (k.astype(F32) * jnp.exp((c - 1.0 - row) * lg)).astype(BF16)
    new_state = jnp.exp(c * lg) * state + _dot_tn(k_dec, v)
    return o, new_state


def _retention_out(o, sg):
    on = o * lax.rsqrt(jnp.mean(o * o, axis=-1, keepdims=True) + EPS)
    return on * sg.astype(F32)


def _ret_prompt_kernel(q_ref, k_ref, v_ref, sg_ref, o_ref, st_ref, s_sc, d_sc, qd_sc, kd_sc, *, c):
    h = pl.program_id(1)
    ti = pl.program_id(2)
    tb = q_ref.shape[1]

    @pl.when(ti == 0)
    def _():
        lg = _log_gamma(h)
        s_sc[...] = jnp.zeros(s_sc.shape, F32)
        tt = lax.broadcasted_iota(jnp.int32, (c, c), 0)
        ss = lax.broadcasted_iota(jnp.int32, (c, c), 1)
        dist = (tt - ss).astype(F32)
        d_sc[...] = jnp.where(dist >= 0, jnp.exp(jnp.maximum(dist, 0.0) * lg), 0.0)
        row = lax.broadcasted_iota(jnp.int32, (c, HEAD_W), 0).astype(F32)
        qd_sc[...] = jnp.exp((row + 1.0) * lg)
        kd_sc[...] = jnp.exp((c - 1.0 - row) * lg)

    state = s_sc[...]
    g_c = qd_sc[c - 1:c, :]
    for i in range(tb // c):
        rows = slice(i * c, (i + 1) * c)
        q, k, v = q_ref[0, rows, :], k_ref[0, rows, :], v_ref[0, rows, :]
        scores = _dot_nt(q, k) * d_sc[...]
        q_dec = (q.astype(F32) * qd_sc[...]).astype(BF16)
        o = _dot(scores.astype(BF16), v) + _dot(q_dec, state.astype(BF16))
        k_dec = (k.astype(F32) * kd_sc[...]).astype(BF16)
        state = g_c * state + _dot_tn(k_dec, v)
        o_ref[0, rows, :] = _retention_out(o, sg_ref[0, rows, :]).astype(o_ref.dtype)
    s_sc[...] = state

    @pl.when(ti == pl.num_programs(2) - 1)
    def _():
        st_ref[0, 0] = state


def _ret_prompt(qr, kr, vr, sg, t_blk, c):
    b, s, _ = qr.shape
    assert s % t_blk == 0 and t_blk % c == 0
    blk = pl.BlockSpec((1, t_blk, HEAD_W), lambda bi, h, ti: (bi, ti, h))
    return pl.pallas_call(
        functools.partial(_ret_prompt_kernel, c=c),
        grid=(b, H_R, s // t_blk),
        in_specs=[blk, blk, blk, blk],
        out_specs=[blk, pl.BlockSpec((1, 1, HEAD_W, HEAD_W), lambda bi, h, ti: (bi, h, 0, 0))],
        out_shape=[jax.ShapeDtypeStruct((b, s, SEG_W), BF16),
                   jax.ShapeDtypeStruct((b, H_R, HEAD_W, HEAD_W), F32)],
        scratch_shapes=[pltpu.VMEM((HEAD_W, HEAD_W), F32), pltpu.VMEM((c, c), F32),
                        pltpu.VMEM((c, HEAD_W), F32), pltpu.VMEM((c, HEAD_W), F32)],
        compiler_params=_params(("parallel", "parallel", "arbitrary")),
        name="ret_prompt",
    )(qr, kr, vr, sg)


def _mix_sample_kernel(qa_ref, ka_ref, va_ref, ck_ref, cv_ref, tbl_ref, lq1, lk1, lq2, lk2, gs_ref,
                       qr_ref, kr_ref, vr_ref, sg_ref, st_ref, oa_ref, or_ref, so_ref, *, lam_init, near):
    p_len = ck_ref.shape[2]
    l = qa_ref.shape[1]
    lam = _lambda(lq1, lk1, lq2, lk2, lam_init)
    for h in range(H_A):
        cols = slice(h * HEAD_W, (h + 1) * HEAD_W)
        qq = _stack_maps(qa_ref[0, :, cols])
        kc = ck_ref[0, cols, :].astype(BF16)
        vc = cv_ref[0, pl.ds(h, p_len, stride=H_A), :].astype(BF16)
        bias = tbl_ref[h]
        bias2 = jnp.concatenate([bias, bias], axis=0)
        s_c = _dot(qq, kc)
        s_c = jnp.concatenate([s_c[:, :p_len - near], s_c[:, p_len - near:] + bias2[:, :near]], axis=1)
        s_n = _dot_nt(qq, ka_ref[0, :, cols]) + bias2[:, near:]
        m = jnp.maximum(jnp.max(s_c, axis=-1, keepdims=True), jnp.max(s_n, axis=-1, keepdims=True))
        p_c = jnp.exp2(s_c - m)
        p_n = jnp.exp2(s_n - m)
        lsum = jnp.sum(p_c, axis=-1, keepdims=True) + jnp.sum(p_n, axis=-1, keepdims=True)
        acc = _dot(p_c.astype(BF16), vc) + _dot(p_n.astype(BF16), va_ref[0, :, cols])
        oa_ref[0, :, cols] = _diff_finish(acc, lsum, lam, gs_ref[...], lam_init).astype(oa_ref.dtype)
    for h in range(H_R):
        cols = slice(h * HEAD_W, (h + 1) * HEAD_W)
        lg = jnp.float32(math.log(1.0 - 2.0 ** (-5.0 - h)))
        o, new_state = _retention_chunk(qr_ref[0, :, cols], kr_ref[0, :, cols], vr_ref[0, :, cols],
                                        st_ref[0, h], lg)
        or_ref[0, :, cols] = _retention_out(o, sg_ref[0, :, cols]).astype(or_ref.dtype)
        so_ref[0, h] = new_state


def _mix_sample(qa, ka, va, cache_k, cache_v, tbl, lams, gs, qr, kr, vr, sg, state, lam_init, near):
    b, l, _ = qa.shape
    p_len = cache_k.shape[2]
    assert p_len % CHUNK == 0 and l <= CHUNK and p_len >= near
    tok = pl.BlockSpec((1, l, SEG_W), lambda bi: (bi, 0, 0))
    cache_kt = pl.BlockSpec((1, SEG_W, p_len), lambda bi: (bi, 0, 0))
    cache_vh = pl.BlockSpec((1, p_len * H_A, HEAD_W), lambda bi: (bi, 0, 0))
    st = pl.BlockSpec((1, H_R, HEAD_W, HEAD_W), lambda bi: (bi, 0, 0, 0))
    vec = _resident((1, DK_A))
    return pl.pallas_call(
        functools.partial(_mix_sample_kernel, lam_init=lam_init, near=near),
        grid=(b,),
        in_specs=[tok, tok, tok, cache_kt, cache_vh, _resident(tbl.shape), vec, vec, vec, vec,
                  _resident((1, HEAD_W)), tok, tok, tok, tok, st],
        out_specs=[tok, tok, st],
        out_shape=[jax.ShapeDtypeStruct((b, l, SEG_W), BF16), jax.ShapeDtypeStruct((b, l, SEG_W), BF16),
                   jax.ShapeDtypeStruct((b, H_R, HEAD_W, HEAD_W), F32)],
        compiler_params=_params(("parallel",)),
        name="mix_sample",
    )(qa, ka, va, cache_k, cache_v, tbl, *lams, gs, qr, kr, vr, sg, state)


def _ffn_kernel(x_ref, oa_ref, or_ref, mod_ref, g2_ref, wo_ref, wg_ref, wu_ref, wd_ref, y_ref, *, ff_blk):
    g, r, d = x_ref.shape
    m = g * r
    f = wg_ref.shape[1]
    proj = (_dot(oa_ref[...].reshape(m, SEG_W), wo_ref[:SEG_W, :])
            + _dot(or_ref[...].reshape(m, SEG_W), wo_ref[SEG_W:, :]))
    x1 = x_ref[...] + mod_ref[:, 2:3, :] * proj.reshape(g, r, d)
    y = x1 * lax.rsqrt(jnp.mean(x1 * x1, axis=-1, keepdims=True) + EPS) * g2_ref[...]
    h2 = (y * (1.0 + mod_ref[:, 4:5, :]) + mod_ref[:, 3:4, :]).astype(BF16).reshape(m, d)
    acc = jnp.zeros((m, d), F32)
    for c0 in range(0, f, ff_blk):
        c1 = min(c0 + ff_blk, f)
        ff = (_silu(_dot(h2, wg_ref[:, c0:c1])) * _dot(h2, wu_ref[:, c0:c1])).astype(BF16)
        acc = acc + _dot(ff, wd_ref[c0:c1, :])
    y_ref[...] = x1 + mod_ref[:, 5:6, :] * acc.reshape(g, r, d)


def _ffn(x, oa, orr, mod, g2, wo, wg, wu, wd, g_blk, r_blk, ff_blk):
    nb, l, d = x.shape
    assert nb % g_blk == 0 and l % r_blk == 0
    nl = l // r_blk
    tok = lambda w: pl.BlockSpec((g_blk, r_blk, w), lambda i: (i // nl, i % nl, 0))
    return pl.pallas_call(
        functools.partial(_ffn_kernel, ff_blk=ff_blk),
        grid=((nb // g_blk) * nl,),
        in_specs=[tok(d), tok(SEG_W), tok(SEG_W),
                  pl.BlockSpec((g_blk, 6, d), lambda i: (i // nl, 0, 0)),
                  _resident((1, d)), _resident(wo.shape), _resident(wg.shape), _resident(wu.shape),
                  _resident(wd.shape)],
        out_specs=tok(d),
        out_shape=jax.ShapeDtypeStruct((nb, l, d), F32),
        compiler_params=_params(("parallel",)),
        name="outproj_ffn",
    )(x, oa, orr, mod, g2, wo, wg, wu, wd)


def _rope_tables(pos):
    inv = 1.0 / (ROPE_BASE ** jnp.linspace(0.0, 1.0, HEAD_W // 2, dtype=F32))
    ang = pos.astype(F32)[:, None] * jnp.repeat(inv, 2)[None, :]
    sin, cos = jnp.sin(ang), jnp.cos(ang)
    even = (jnp.arange(HEAD_W) % 2 == 0)[None, :]
    return cos, jnp.where(even, -sin, 0.0), jnp.where(even, 0.0, sin)


def _logit_bound(g_q, g_k, rel_bias):
    slack = 1.02
    dot_max = slack * DK_A * (DK_A ** -0.5 * LOG2E) * jnp.max(jnp.abs(g_q)) * jnp.max(jnp.abs(g_k))
    rb = (rel_bias - rel_bias[FAR_BUCKET][None, :]) * LOG2E
    b_hi = jnp.maximum(jnp.max(rb), 0.0)
    b_lo = jnp.minimum(jnp.min(rb), 0.0)
    ok = (2.0 * dot_max + (b_hi - b_lo)) <= MAX_LOGIT_SPREAD
    return jnp.stack([dot_max + b_hi, ok.astype(F32)])


def _tile_rows(n, target):
    t = min(n, target)
    while n % t:
        t //= 2
    return t


def kernel(x_prompt, x_sample, c_prompt, c_sample, cache_k, cache_v, state_ret, w_ada, b_ada, g_norm1, g_norm2, w_in, g_q, g_k, lam_q1, lam_k1, lam_q2, lam_k2, g_subln, w_out, w_ff_gate, w_ff_up, w_ff_down, rel_bias):
    depth = w_ada.shape[0]
    b, s, d = x_prompt.shape
    bs, l, _ = x_sample.shape
    p_len = cache_k.shape[2]
    assert w_in.shape[2] == N_SEG * SEG_W and d == 2 * SEG_W

    tm = _tile_rows(s, 512)
    tm_in = _tile_rows(s, 1024)
    tq = _tile_rows(s, 1024)
    tk = _tile_rows(tq, 1024)
    tkd = _tile_rows(tq, 256)
    ret_blk = _tile_rows(s, 4096)
    ret_c = _tile_rows(ret_blk, 256)
    near = HEAD_W
    assert tq % CHUNK == 0 and tk % CHUNK == 0 and tkd % CHUNK == 0 and near >= BUCKET_THRESHOLDS[-1]

    tab_p = _rope_tables(jnp.arange(s, dtype=jnp.int32))
    tab_s = _rope_tables(p_len + jnp.arange(l, dtype=jnp.int32))
    tbl_p = _bias_table(rel_bias, tkd, near + tkd, near)
    tbl_s = _bias_table(rel_bias, l, near + l, near)
    c_all = jnp.concatenate([c_prompt, c_sample], axis=0)

    xp, xs = x_prompt, x_sample
    outs = [[] for _ in range(6)]
    for layer in range(depth):
        lam_init = _lambda_init(layer)
        mod = _modulation(c_all, w_ada[layer], b_ada[layer]).reshape(b + bs, 6, d)
        mod_p, mod_s = mod[:b], mod[b:]
        g1 = g_norm1[layer].reshape(1, d)
        g2 = g_norm2[layer].reshape(1, d)
        gq_t = jnp.tile(g_q[layer], SEG_W // DK_A).reshape(1, SEG_W)
        gk_t = jnp.tile(g_k[layer], SEG_W // DK_A).reshape(1, SEG_W)
        lams = [v[layer].reshape(1, DK_A) for v in (lam_q1, lam_k1, lam_q2, lam_k2)]
        gs = g_subln[layer].reshape(1, HEAD_W)
        wi = w_in[layer].astype(BF16)
        wo = w_out[layer].astype(BF16)
        wg = w_ff_gate[layer].astype(BF16)
        wu = w_ff_up[layer].astype(BF16)
        wd = w_ff_down[layer].astype(BF16)
        ff_blk = 1024

        kf, vf, qa, ka, va, qr, kr, vr, sg = _inproj(xp, mod_p, g1, wi, gq_t, gk_t, tab_p, 1, tm_in)
        bound = _logit_bound(g_q[layer], g_k[layer], rel_bias)
        oa = _attn_prompt(bound, qa, ka, va, tbl_p, lams, gs, lam_init, tq, tk, tkd, near)
        orr, st_p = _ret_prompt(qr, kr, vr, sg, ret_blk, ret_c)
        xp = _ffn(xp, oa, orr, mod_p, g2, wo, wg, wu, wd, 1, tm, ff_blk)

        kfs, vfs, qas, kas, vas, qrs, krs, vrs, sgs = _inproj(xs, mod_s, g1, wi, gq_t, gk_t, tab_s, bs, l)
        ck = jnp.swapaxes(cache_k[layer].reshape(bs, p_len, SEG_W), 1, 2)
        cv = cache_v[layer].reshape(bs, p_len * H_A, HEAD_W)
        oas, ors, st_s = _mix_sample(qas, kas, vas, ck, cv, tbl_s, lams, gs, qrs, krs, vrs, sgs,
                                     state_ret[layer], lam_init, near)
        xs = _ffn(xs, oas, ors, mod_s, g2, wo, wg, wu, wd, bs, l, ff_blk)

        for lst, val in zip(outs, (kf.reshape(b, s, H_A, 2, DK_A), vf.reshape(b, s, H_A, HEAD_W), st_p,
                                   kfs.reshape(bs, l, H_A, 2, DK_A), vfs.reshape(bs, l, H_A, HEAD_W), st_s)):
            lst.append(val)
    return (xp, xs) + tuple(jnp.stack(o) for o in outs)
```

```python
import functools
import math

import jax
import jax.numpy as jnp
from jax import lax
from jax.experimental import pallas as pl
from jax.experimental.pallas import tpu as pltpu

F32 = jnp.float32
BF16 = jnp.bfloat16

CHUNK = 64
H_A = 4
DK_A = 64
H_R = 4
HEAD_W = 128
SEG_W = H_A * HEAD_W
N_SEG = 7
N_BUCKETS = 32
ROPE_BASE = 10000.0
EPS = 1e-6
NEG_INF = -1e30
LOG2E = math.log2(math.e)
BUCKET_THRESHOLDS = (12, 16, 23, 32, 46, 64, 91)
FAR_BUCKET = 15
MAX_LOGIT_SPREAD = 100.0

MXU_W = 256
VMEM_LIMIT = 56 * 1024 * 1024


def _lambda_init(layer):
    return 0.8 - 0.6 * math.exp(-0.3 * layer)


def _params(sem):
    return pltpu.CompilerParams(dimension_semantics=sem, vmem_limit_bytes=VMEM_LIMIT)


def _resident(shape):
    return pl.BlockSpec(shape, lambda *_: (0,) * len(shape), pipeline_mode=pl.Buffered(1))


def _dot(a, b):
    return jnp.dot(a, b, preferred_element_type=F32)


def _dot_nt(a, b):
    return lax.dot_general(a, b, (((1,), (1,)), ((), ())), preferred_element_type=F32)


def _dot_tn(a, b):
    return lax.dot_general(a, b, (((0,), (0,)), ((), ())), preferred_element_type=F32)


def _silu(x):
    return x * jax.nn.sigmoid(x)


def _mod_kernel(c_ref, w_ref, b_ref, o_ref):
    s = _silu(c_ref[...]).astype(BF16)
    o_ref[...] = _dot(s, w_ref[...].astype(BF16)) + b_ref[...]


def _modulation(c_all, w_ada, b_ada):
    n, d = c_all.shape
    e = w_ada.shape[1]
    te = 1536
    assert e % te == 0
    return pl.pallas_call(
        _mod_kernel,
        grid=(e // te,),
        in_specs=[pl.BlockSpec((n, d), lambda j: (0, 0)),
                  pl.BlockSpec((d, te), lambda j: (0, j)),
                  pl.BlockSpec((1, te), lambda j: (0, j))],
        out_specs=pl.BlockSpec((n, te), lambda j: (0, j)),
        out_shape=jax.ShapeDtypeStruct((n, e), F32),
        compiler_params=_params(("parallel",)),
        name="modulation",
    )(c_all, w_ada, b_ada.reshape(1, e))


def _bias_table_kernel(rb_ref, o_ref, *, offset):
    h = pl.program_id(0)
    _, rows, cols = o_ref.shape
    r = lax.broadcasted_iota(jnp.int32, (rows, cols), 0)
    kc = lax.broadcasted_iota(jnp.int32, (rows, cols), 1) - offset
    rel = kc - r
    n = jnp.abs(rel)
    large = jnp.full((rows, cols), N_BUCKETS // 4, jnp.int32)
    for thr in BUCKET_THRESHOLDS:
        large = large + (n >= thr).astype(jnp.int32)
    bucket = jnp.where(rel > 0, N_BUCKETS // 2, 0) + jnp.where(n < N_BUCKETS // 4, n, large)
    val = jnp.zeros((rows, cols), F32)
    for b in range(N_BUCKETS):
        val = jnp.where(bucket == b, rb_ref[b, h], val)
    val = (val - rb_ref[FAR_BUCKET, h]) * LOG2E
    shift = int(math.log2(CHUNK))
    visible = lax.shift_right_arithmetic(kc, shift) <= lax.shift_right_arithmetic(r, shift)
    o_ref[0] = jnp.where(visible, val, NEG_INF)


def _bias_table(rel_bias, rows, cols, offset):
    assert offset >= BUCKET_THRESHOLDS[-1] and offset % CHUNK == 0
    return pl.pallas_call(
        functools.partial(_bias_table_kernel, offset=offset),
        grid=(H_A,),
        in_specs=[pl.BlockSpec(memory_space=pltpu.SMEM)],
        out_specs=pl.BlockSpec((1, rows, cols), lambda h: (h, 0, 0)),
        out_shape=jax.ShapeDtypeStruct((H_A, rows, cols), F32),
        compiler_params=_params(("parallel",)),
        name="bias_table",
    )(rel_bias)


def _inproj_kernel(x_ref, mod_ref, g1_ref, w_ref, gq_ref, gk_ref, cos_ref, sine_ref, sino_ref,
                   kf_ref, vf_ref, qa_ref, ka_ref, va_ref, qr_ref, kr_ref, vr_ref, sg_ref):
    g, r, d = x_ref.shape
    m = g * r
    x = x_ref[...]
    y = x * lax.rsqrt(jnp.mean(x * x, axis=-1, keepdims=True) + EPS) * g1_ref[...]
    hmod = y * (1.0 + mod_ref[:, 1:2, :]) + mod_ref[:, 0:1, :]
    hb = hmod.astype(BF16).reshape(m, d)

    def seg(j):
        return _dot(hb, w_ref[:, j * SEG_W:(j + 1) * SEG_W])

    def put(ref, val):
        ref[...] = val.reshape(g, r, SEG_W).astype(ref.dtype)

    gi = lax.broadcasted_iota(jnp.int32, (MXU_W, MXU_W), 0) // DK_A
    gj = lax.broadcasted_iota(jnp.int32, (MXU_W, MXU_W), 1) // DK_A
    ones_bd = (gi == gj).astype(BF16)

    def group_rms(z):
        sq = (z * z).astype(BF16)
        ms = jnp.concatenate([_dot(sq[:, c:c + MXU_W], ones_bd) for c in range(0, SEG_W, MXU_W)], axis=1)
        return z * lax.rsqrt(ms * (1.0 / DK_A) + EPS)

    def table(ref):
        return jnp.broadcast_to(ref[...][None], (g, r, HEAD_W)).reshape(m, HEAD_W)

    cos, sine, sino = table(cos_ref), table(sine_ref), table(sino_ref)

    def rotate(z):
        outs = []
        for hh in range(H_R):
            zh = z[:, hh * HEAD_W:(hh + 1) * HEAD_W]
            nxt = pltpu.roll(zh, HEAD_W - 1, axis=1)
            prv = pltpu.roll(zh, 1, axis=1)
            outs.append(zh * cos + nxt * sine + prv * sino)
        return jnp.concatenate(outs, axis=1)

    put(qa_ref, group_rms(seg(0)) * gq_ref[...] * (DK_A ** -0.5 * LOG2E))
    ka = group_rms(seg(1)) * gk_ref[...]
    put(kf_ref, ka)
    put(ka_ref, ka)
    put(qr_ref, rotate(seg(3)))
    put(kr_ref, rotate(seg(4)) * (HEAD_W ** -0.5))
    put(sg_ref, _silu(seg(6)))
    va = seg(2)
    for hh in range(H_A):
        vf_ref[:, pl.ds(hh, r, stride=H_A), :] = va[:, hh * HEAD_W:(hh + 1) * HEAD_W].reshape(g, r, HEAD_W)
    put(va_ref, va)
    put(vr_ref, seg(5))


def _inproj(x, mod, g1, w_in, gq_t, gk_t, tables, g_blk, r_blk):
    nb, l, d = x.shape
    assert nb % g_blk == 0 and l % r_blk == 0
    nl = l // r_blk
    tok = lambda w: pl.BlockSpec((g_blk, r_blk, w), lambda i: (i // nl, i % nl, 0))
    tab = pl.BlockSpec((r_blk, HEAD_W), lambda i: (i % nl, 0))
    out = lambda dt: jax.ShapeDtypeStruct((nb, l, SEG_W), dt)
    return pl.pallas_call(
        _inproj_kernel,
        grid=((nb // g_blk) * nl,),
        in_specs=[tok(d),
                  pl.BlockSpec((g_blk, 6, d), lambda i: (i // nl, 0, 0)),
                  _resident((1, d)), _resident(w_in.shape), _resident((1, SEG_W)), _resident((1, SEG_W)),
                  tab, tab, tab],
        out_specs=[tok(SEG_W), pl.BlockSpec((g_blk, r_blk * H_A, HEAD_W), lambda i: (i // nl, i % nl, 0))]
        + [tok(SEG_W)] * 7,
        out_shape=[out(F32), jax.ShapeDtypeStruct((nb, l * H_A, HEAD_W), F32)] + [out(BF16)] * 7,
        compiler_params=_params(("parallel",)),
        name="inproj",
    )(x, mod, g1, w_in, gq_t, gk_t, *tables)


def _stack_maps(q):
    lane = lax.broadcasted_iota(jnp.int32, q.shape, 1)
    zero = jnp.zeros_like(q)
    return jnp.concatenate([jnp.where(lane < DK_A, q, zero), jnp.where(lane >= DK_A, q, zero)], axis=0)


def _lambda(lq1, lk1, lq2, lk2, lam_init):
    s1 = jnp.sum(lq1[...] * lk1[...], axis=-1, keepdims=True)
    s2 = jnp.sum(lq2[...] * lk2[...], axis=-1, keepdims=True)
    return jnp.exp(s1) - jnp.exp(s2) + lam_init


def _diff_finish(acc, l, lam, gs, lam_init):
    t = acc.shape[0] // 2
    o = acc[:t] * (1.0 / l[:t]) - lam * (acc[t:] * (1.0 / l[t:]))
    on = o * lax.rsqrt(jnp.mean(o * o, axis=-1, keepdims=True) + EPS)
    return on * gs * (1.0 - lam_init)


def _attn_prompt_kernel(bound_ref, q_ref, k_ref, v_ref, tbl_ref, lq1, lk1, lq2, lk2, gs_ref, o_ref,
                        qq_sc, m_sc, l_sc, acc_sc, *, lam_init, tk, tkd, near):
    qi = pl.program_id(2)
    tq = q_ref.shape[1]
    q = q_ref[0]
    lane = lax.broadcasted_iota(jnp.int32, q.shape, 1)
    zero = jnp.zeros_like(q)
    qq_sc[0] = jnp.where(lane < DK_A, q, zero)
    qq_sc[1] = jnp.where(lane >= DK_A, q, zero)

    def tile(start, width, row0, corner, diag, bounded, first=False):
        nr = tq - row0
        nblk = width // HEAD_W
        rows = slice(row0, tq)
        flat = lambda x: x.reshape(2 * nr, x.shape[-1])
        kt = k_ref[0, pl.ds(start, width), :]
        vt = v_ref[0, pl.ds(start, width), :]
        s = _dot_nt(flat(qq_sc[:, rows, :]), kt)
        blocks = [s[:, c * HEAD_W:(c + 1) * HEAD_W] for c in range(nblk)]

        def add_rows(blk, r0, val):
            r1 = r0 + val.shape[0]
            cuts = [(0, r0, False), (r0, r1, True), (r1, nr + r0, False), (nr + r0, nr + r1, True),
                    (nr + r1, 2 * nr, False)]
            return jnp.concatenate([blk[a:b] + val if hit else blk[a:b] for a, b, hit in cuts if b > a], axis=0)

        if diag:
            blocks = [add_rows(blk, 0, tbl_ref[0, :, near + c * HEAD_W:near + (c + 1) * HEAD_W])
                      for c, blk in enumerate(blocks)]
        corner_row0 = width if diag else 0
        if corner_row0 < nr:
            blocks[-1] = add_rows(blocks[-1], corner_row0, corner)
        if bounded:
            ps = [jnp.exp2(blk - bound_ref[0]) for blk in blocks]
            l_new = functools.reduce(jnp.add, ps)
            p = jnp.concatenate([pc.astype(BF16) for pc in ps], axis=1)
            acc_new = _dot(p, vt)
            if not first:
                l_new = flat(l_sc[:, rows, :]) + l_new
                acc_new = flat(acc_sc[:, rows, :]) + acc_new
            l_sc[:, rows, :] = l_new.reshape(2, nr, HEAD_W)
            acc_sc[:, rows, :] = acc_new.reshape(2, nr, HEAD_W)
            return
        m_old = flat(m_sc[:, rows, :])
        mx = functools.reduce(jnp.maximum, blocks)
        m_new = jnp.maximum(m_old, jnp.max(mx, axis=-1, keepdims=True))
        alpha = jnp.exp2(m_old - m_new)
        ps = [jnp.exp2(blk - m_new) for blk in blocks]
        l_new = alpha * flat(l_sc[:, rows, :]) + functools.reduce(jnp.add, ps)
        p = jnp.concatenate([pc.astype(BF16) for pc in ps], axis=1)
        acc_new = alpha * flat(acc_sc[:, rows, :]) + _dot(p, vt)
        m_sc[:, rows, :] = m_new.reshape(2, nr, HEAD_W)
        l_sc[:, rows, :] = l_new.reshape(2, nr, HEAD_W)
        acc_sc[:, rows, :] = acc_new.reshape(2, nr, HEAD_W)

    n_far = qi * (tq // tk)
    corner_tbl = tbl_ref[0, :near, :near]

    def diag_tiles(bounded, first):
        for d in range(tq // tkd):
            tile(pl.multiple_of(qi * tq + d * tkd, tkd), tkd, d * tkd, corner_tbl, True, bounded, first and d == 0)

    def run(bounded):
        if bounded:
            diag_tiles(True, True)

        def far_tile(j):
            flag = (j == n_far - 1).astype(F32)
            tile(pl.multiple_of(j * tk, tk), tk, 0, corner_tbl * flag, False, bounded)

        odd = jnp.bitwise_and(n_far, 1)

        @pl.when(odd == 1)
        def _():
            far_tile(0)

        def far_pair(i, carry):
            far_tile(odd + 2 * i)
            far_tile(odd + 2 * i + 1)
            return carry

        lax.fori_loop(0, lax.shift_right_logical(n_far, 1), far_pair, 0)
        if not bounded:
            diag_tiles(False, False)

    @pl.when(bound_ref[1] > 0.5)
    def _():
        run(True)

    @pl.when(bound_ref[1] <= 0.5)
    def _():
        m_sc[...] = jnp.full(m_sc.shape, NEG_INF, F32)
        l_sc[...] = jnp.zeros(l_sc.shape, F32)
        acc_sc[...] = jnp.zeros(acc_sc.shape, F32)
        run(False)

    lam = _lambda(lq1, lk1, lq2, lk2, lam_init)
    l = jnp.sum(l_sc[...], axis=-1, keepdims=True).reshape(2 * tq, 1)
    acc = acc_sc[...].reshape(2 * tq, HEAD_W)
    o_ref[0] = _diff_finish(acc, l, lam, gs_ref[...], lam_init).astype(o_ref.dtype)


def _attn_prompt(bound, q, k, v, tbl, lams, gs, lam_init, tq, tk, tkd, near):
    b, s, _ = q.shape
    assert s % tq == 0 and tq % tk == 0 and tq % tkd == 0 and tbl.shape == (H_A, tkd, near + tkd)
    assert near == HEAD_W and tk % HEAD_W == 0 and tkd % HEAD_W == 0
    vec = _resident((1, DK_A))
    return pl.pallas_call(
        functools.partial(_attn_prompt_kernel, lam_init=lam_init, tk=tk, tkd=tkd, near=near),
        grid=(b, H_A, s // tq),
        in_specs=[pl.BlockSpec(memory_space=pltpu.SMEM),
                  pl.BlockSpec((1, tq, HEAD_W), lambda bi, h, qi: (bi, qi, h)),
                  pl.BlockSpec((1, s, HEAD_W), lambda bi, h, qi: (bi, 0, h)),
                  pl.BlockSpec((1, s, HEAD_W), lambda bi, h, qi: (bi, 0, h)),
                  pl.BlockSpec((1, tkd, near + tkd), lambda bi, h, qi: (h, 0, 0)),
                  vec, vec, vec, vec, _resident((1, HEAD_W))],
        out_specs=pl.BlockSpec((1, tq, HEAD_W), lambda bi, h, qi: (bi, qi, h)),
        out_shape=jax.ShapeDtypeStruct((b, s, SEG_W), BF16),
        scratch_shapes=[pltpu.VMEM((2, tq, HEAD_W), BF16)] + [pltpu.VMEM((2, tq, HEAD_W), F32)] * 3,
        compiler_params=_params(("parallel", "parallel", "arbitrary")),
        name="attn_prompt",
    )(bound, q, k, v, tbl, *lams, gs)


def _log_gamma(h):
    vals = [math.log(1.0 - 2.0 ** (-5.0 - i)) for i in range(H_R)]
    lg = jnp.float32(vals[H_R - 1])
    for i in range(H_R - 2, -1, -1):
        lg = jnp.where(h == i, jnp.float32(vals[i]), lg)
    return lg


def _retention_chunk(q, k, v, state, lg):
    c = q.shape[0]
    ti = lax.broadcasted_iota(jnp.int32, (c, c), 0)
    si = lax.broadcasted_iota(jnp.int32, (c, c), 1)
    dist = (ti - si).astype(F32)
    decay = jnp.where(dist >= 0, jnp.exp(jnp.maximum(dist, 0.0) * lg), 0.0)
    scores = _dot_nt(q, k) * decay
    row = lax.broadcasted_iota(jnp.int32, (c, 1), 0).astype(F32)
    q_dec = (q.astype(F32) * jnp.exp((row + 1.0) * lg)).astype(BF16)
    o = _dot(scores.astype(BF16), v) + _dot(q_dec, state.astype(BF16))
    k_dec = (k.astype(F32) * jnp.exp((c - 1.0 - row) * lg)).astype(BF16)
    new_state = jnp.exp(c * lg) * state + _dot_tn(k_dec, v)
    return o, new_state


def _retention_out(o, sg):
    on = o * lax.rsqrt(jnp.mean(o * o, axis=-1, keepdims=True) + EPS)
    return on * sg.astype(F32)


def _ret_prompt_kernel(q_ref, k_ref, v_ref, sg_ref, o_ref, st_ref, s_sc, d_sc, qd_sc, kd_sc, *, c):
    h = pl.program_id(1)
    ti = pl.program_id(2)
    tb = q_ref.shape[1]

    @pl.when(ti == 0)
    def _():
        lg = _log_gamma(h)
        s_sc[...] = jnp.zeros(s_sc.shape, F32)
        tt = lax.broadcasted_iota(jnp.int32, (c, c), 0)
        ss = lax.broadcasted_iota(jnp.int32, (c, c), 1)
        dist = (tt - ss).astype(F32)
        d_sc[...] = jnp.where(dist >= 0, jnp.exp(jnp.maximum(dist, 0.0) * lg), 0.0)
        row = lax.broadcasted_iota(jnp.int32, (c, HEAD_W), 0).astype(F32)
        qd_sc[...] = jnp.exp((row + 1.0) * lg)
        kd_sc[...] = jnp.exp((c - 1.0 - row) * lg)

    state = s_sc[...]
    g_c = qd_sc[c - 1:c, :]
    for i in range(tb // c):
        rows = slice(i * c, (i + 1) * c)
        q, k, v = q_ref[0, rows, :], k_ref[0, rows, :], v_ref[0, rows, :]
        scores = _dot_nt(q, k) * d_sc[...]
        q_dec = (q.astype(F32) * qd_sc[...]).astype(BF16)
        o = _dot(scores.astype(BF16), v) + _dot(q_dec, state.astype(BF16))
        k_dec = (k.astype(F32) * kd_sc[...]).astype(BF16)
        state = g_c * state + _dot_tn(k_dec, v)
        o_ref[0, rows, :] = _retention_out(o, sg_ref[0, rows, :]).astype(o_ref.dtype)
    s_sc[...] = state

    @pl.when(ti == pl.num_programs(2) - 1)
    def _():
        st_ref[0, 0] = state


def _ret_prompt(qr, kr, vr, sg, t_blk, c):
    b, s, _ = qr.shape
    assert s % t_blk == 0 and t_blk % c == 0
    blk = pl.BlockSpec((1, t_blk, HEAD_W), lambda bi, h, ti: (bi, ti, h))
    return pl.pallas_call(
        functools.partial(_ret_prompt_kernel, c=c),
        grid=(b, H_R, s // t_blk),
        in_specs=[blk, blk, blk, blk],
        out_specs=[blk, pl.BlockSpec((1, 1, HEAD_W, HEAD_W), lambda bi, h, ti: (bi, h, 0, 0))],
        out_shape=[jax.ShapeDtypeStruct((b, s, SEG_W), BF16),
                   jax.ShapeDtypeStruct((b, H_R, HEAD_W, HEAD_W), F32)],
        scratch_shapes=[pltpu.VMEM((HEAD_W, HEAD_W), F32), pltpu.VMEM((c, c), F32),
                        pltpu.VMEM((c, HEAD_W), F32), pltpu.VMEM((c, HEAD_W), F32)],
        compiler_params=_params(("parallel", "parallel", "arbitrary")),
        name="ret_prompt",
    )(qr, kr, vr, sg)


def _mix_sample_kernel(qa_ref, ka_ref, va_ref, ck_ref, cv_ref, tbl_ref, lq1, lk1, lq2, lk2, gs_ref,
                       qr_ref, kr_ref, vr_ref, sg_ref, st_ref, oa_ref, or_ref, so_ref, *, lam_init, near):
    p_len = ck_ref.shape[2]
    l = qa_ref.shape[1]
    lam = _lambda(lq1, lk1, lq2, lk2, lam_init)
    for h in range(H_A):
        cols = slice(h * HEAD_W, (h + 1) * HEAD_W)
        qq = _stack_maps(qa_ref[0, :, cols])
        kc = ck_ref[0, cols, :].astype(BF16)
        vc = cv_ref[0, pl.ds(h, p_len, stride=H_A), :].astype(BF16)
        bias = tbl_ref[h]
        bias2 = jnp.concatenate([bias, bias], axis=0)
        s_c = _dot(qq, kc)
        s_c = jnp.concatenate([s_c[:, :p_len - near], s_c[:, p_len - near:] + bias2[:, :near]], axis=1)
        s_n = _dot_nt(qq, ka_ref[0, :, cols]) + bias2[:, near:]
        m = jnp.maximum(jnp.max(s_c, axis=-1, keepdims=True), jnp.max(s_n, axis=-1, keepdims=True))
        p_c = jnp.exp2(s_c - m)
        p_n = jnp.exp2(s_n - m)
        lsum = jnp.sum(p_c, axis=-1, keepdims=True) + jnp.sum(p_n, axis=-1, keepdims=True)
        acc = _dot(p_c.astype(BF16), vc) + _dot(p_n.astype(BF16), va_ref[0, :, cols])
        oa_ref[0, :, cols] = _diff_finish(acc, lsum, lam, gs_ref[...], lam_init).astype(oa_ref.dtype)
    for h in range(H_R):
        cols = slice(h * HEAD_W, (h + 1) * HEAD_W)
        lg = jnp.float32(math.log(1.0 - 2.0 ** (-5.0 - h)))
        o, new_state = _retention_chunk(qr_ref[0, :, cols], kr_ref[0, :, cols], vr_ref[0, :, cols],
                                        st_ref[0, h], lg)
        or_ref[0, :, cols] = _retention_out(o, sg_ref[0, :, cols]).astype(or_ref.dtype)
        so_ref[0, h] = new_state


def _mix_sample(qa, ka, va, cache_k, cache_v, tbl, lams, gs, qr, kr, vr, sg, state, lam_init, near):
    b, l, _ = qa.shape
    p_len = cache_k.shape[2]
    assert p_len % CHUNK == 0 and l <= CHUNK and p_len >= near
    tok = pl.BlockSpec((1, l, SEG_W), lambda bi: (bi, 0, 0))
    cache_kt = pl.BlockSpec((1, SEG_W, p_len), lambda bi: (bi, 0, 0))
    cache_vh = pl.BlockSpec((1, p_len * H_A, HEAD_W), lambda bi: (bi, 0, 0))
    st = pl.BlockSpec((1, H_R, HEAD_W, HEAD_W), lambda bi: (bi, 0, 0, 0))
    vec = _resident((1, DK_A))
    return pl.pallas_call(
        functools.partial(_mix_sample_kernel, lam_init=lam_init, near=near),
        grid=(b,),
        in_specs=[tok, tok, tok, cache_kt, cache_vh, _resident(tbl.shape), vec, vec, vec, vec,
                  _resident((1, HEAD_W)), tok, tok, tok, tok, st],
        out_specs=[tok, tok, st],
        out_shape=[jax.ShapeDtypeStruct((b, l, SEG_W), BF16), jax.ShapeDtypeStruct((b, l, SEG_W), BF16),
                   jax.ShapeDtypeStruct((b, H_R, HEAD_W, HEAD_W), F32)],
        compiler_params=_params(("parallel",)),
        name="mix_sample",
    )(qa, ka, va, cache_k, cache_v, tbl, *lams, gs, qr, kr, vr, sg, state)


def _ffn_kernel(x_ref, oa_ref, or_ref, mod_ref, g2_ref, wo_ref, wg_ref, wu_ref, wd_ref, y_ref, *, ff_blk):
    g, r, d = x_ref.shape
    m = g * r
    f = wg_ref.shape[1]
    proj = (_dot(oa_ref[...].reshape(m, SEG_W), wo_ref[:SEG_W, :])
            + _dot(or_ref[...].reshape(m, SEG_W), wo_ref[SEG_W:, :]))
    x1 = x_ref[...] + mod_ref[:, 2:3, :] * proj.reshape(g, r, d)
    y = x1 * lax.rsqrt(jnp.mean(x1 * x1, axis=-1, keepdims=True) + EPS) * g2_ref[...]
    h2 = (y * (1.0 + mod_ref[:, 4:5, :]) + mod_ref[:, 3:4, :]).astype(BF16).reshape(m, d)
    acc = jnp.zeros((m, d), F32)
    for c0 in range(0, f, ff_blk):
        c1 = min(c0 + ff_blk, f)
        ff = (_silu(_dot(h2, wg_ref[:, c0:c1])) * _dot(h2, wu_ref[:, c0:c1])).astype(BF16)
        acc = acc + _dot(ff, wd_ref[c0:c1, :])
    y_ref[...] = x1 + mod_ref[:, 5:6, :] * acc.reshape(g, r, d)


def _ffn(x, oa, orr, mod, g2, wo, wg, wu, wd, g_blk, r_blk, ff_blk):
    nb, l, d = x.shape
    assert nb % g_blk == 0 and l % r_blk == 0
    nl = l // r_blk
    tok = lambda w: pl.BlockSpec((g_blk, r_blk, w), lambda i: (i // nl, i % nl, 0))
    return pl.pallas_call(
        functools.partial(_ffn_kernel, ff_blk=ff_blk),
        grid=((nb // g_blk) * nl,),
        in_specs=[tok(d), tok(SEG_W), tok(SEG_W),
                  pl.BlockSpec((g_blk, 6, d), lambda i: (i // nl, 0, 0)),
                  _resident((1, d)), _resident(wo.shape), _resident(wg.shape), _resident(wu.shape),
                  _resident(wd.shape)],
        out_specs=tok(d),
        out_shape=jax.ShapeDtypeStruct((nb, l, d), F32),
        compiler_params=_params(("parallel",)),
        name="outproj_ffn",
    )(x, oa, orr, mod, g2, wo, wg, wu, wd)


def _rope_tables(pos):
    inv = 1.0 / (ROPE_BASE ** jnp.linspace(0.0, 1.0, HEAD_W // 2, dtype=F32))
    ang = pos.astype(F32)[:, None] * jnp.repeat(inv, 2)[None, :]
    sin, cos = jnp.sin(ang), jnp.cos(ang)
    even = (jnp.arange(HEAD_W) % 2 == 0)[None, :]
    return cos, jnp.where(even, -sin, 0.0), jnp.where(even, 0.0, sin)


def _logit_bound(g_q, g_k, rel_bias):
    slack = 1.02
    dot_max = slack * DK_A * (DK_A ** -0.5 * LOG2E) * jnp.max(jnp.abs(g_q)) * jnp.max(jnp.abs(g_k))
    rb = (rel_bias - rel_bias[FAR_BUCKET][None, :]) * LOG2E
    b_hi = jnp.maximum(jnp.max(rb), 0.0)
    b_lo = jnp.minimum(jnp.min(rb), 0.0)
    ok = (2.0 * dot_max + (b_hi - b_lo)) <= MAX_LOGIT_SPREAD
    return jnp.stack([dot_max + b_hi, ok.astype(F32)])


def _tile_rows(n, target):
    t = min(n, target)
    while n % t:
        t //= 2
    return t


def kernel(x_prompt, x_sample, c_prompt, c_sample, cache_k, cache_v, state_ret, w_ada, b_ada, g_norm1, g_norm2, w_in, g_q, g_k, lam_q1, lam_k1, lam_q2, lam_k2, g_subln, w_out, w_ff_gate, w_ff_up, w_ff_down, rel_bias):
    depth = w_ada.shape[0]
    b, s, d = x_prompt.shape
    bs, l, _ = x_sample.shape
    p_len = cache_k.shape[2]
    assert w_in.shape[2] == N_SEG * SEG_W and d == 2 * SEG_W

    tm = _tile_rows(s, 512)
    tm_in = _tile_rows(s, 1024)
    tq = _tile_rows(s, 1024)
    tk = _tile_rows(tq, 1024)
    tkd = _tile_rows(tq, 256)
    ret_blk = _tile_rows(s, 4096)
    ret_c = _tile_rows(ret_blk, 256)
    near = HEAD_W
    assert tq % CHUNK == 0 and tk % CHUNK == 0 and tkd % CHUNK == 0 and near >= BUCKET_THRESHOLDS[-1]

    tab_p = _rope_tables(jnp.arange(s, dtype=jnp.int32))
    tab_s = _rope_tables(p_len + jnp.arange(l, dtype=jnp.int32))
    tbl_p = _bias_table(rel_bias, tkd, near + tkd, near)
    tbl_s = _bias_table(rel_bias, l, near + l, near)
    c_all = jnp.concatenate([c_prompt, c_sample], axis=0)

    xp, xs = x_prompt, x_sample
    outs = [[] for _ in range(6)]
    for layer in range(depth):
        lam_init = _lambda_init(layer)
        mod = _modulation(c_all, w_ada[layer], b_ada[layer]).reshape(b + bs, 6, d)
        mod_p, mod_s = mod[:b], mod[b:]
        g1 = g_norm1[layer].reshape(1, d)
        g2 = g_norm2[layer].reshape(1, d)
        gq_t = jnp.tile(g_q[layer], SEG_W // DK_A).reshape(1, SEG_W)
        gk_t = jnp.tile(g_k[layer], SEG_W // DK_A).reshape(1, SEG_W)
        lams = [v[layer].reshape(1, DK_A) for v in (lam_q1, lam_k1, lam_q2, lam_k2)]
        gs = g_subln[layer].reshape(1, HEAD_W)
        wi = w_in[layer].astype(BF16)
        wo = w_out[layer].astype(BF16)
        wg = w_ff_gate[layer].astype(BF16)
        wu = w_ff_up[layer].astype(BF16)
        wd = w_ff_down[layer].astype(BF16)
        ff_blk = 1024

        kf, vf, qa, ka, va, qr, kr, vr, sg = _inproj(xp, mod_p, g1, wi, gq_t, gk_t, tab_p, 1, tm_in)
        bound = _logit_bound(g_q[layer], g_k[layer], rel_bias)
        oa = _attn_prompt(bound, qa, ka, va, tbl_p, lams, gs, lam_init, tq, tk, tkd, near)
        orr, st_p = _ret_prompt(qr, kr, vr, sg, ret_blk, ret_c)
        xp = _ffn(xp, oa, orr, mod_p, g2, wo, wg, wu, wd, 1, tm, ff_blk)

        kfs, vfs, qas, kas, vas, qrs, krs, vrs, sgs = _inproj(xs, mod_s, g1, wi, gq_t, gk_t, tab_s, bs, l)
        ck = jnp.swapaxes(cache_k[layer].reshape(bs, p_len, SEG_W), 1, 2)
        cv = cache_v[layer].reshape(bs, p_len * H_A, HEAD_W)
        oas, ors, st_s = _mix_sample(qas, kas, vas, ck, cv, tbl_s, lams, gs, qrs, krs, vrs, sgs,
                                     state_ret[layer], lam_init, near)
        xs = _ffn(xs, oas, ors, mod_s, g2, wo, wg, wu, wd, bs, l, ff_blk)

        for lst, val in zip(outs, (kf.reshape(b, s, H_A, 2, DK_A), vf.reshape(b, s, H_A, HEAD_W), st_p,
                                   kfs.reshape(bs, l, H_A, 2, DK_A), vfs.reshape(bs, l, H_A, HEAD_W), st_s)):
            lst.append(val)
    return (xp, xs) + tuple(jnp.stack(o) for o in outs)
```

```python
import functools
import math

import jax
import jax.numpy as jnp
from jax import lax
from jax.experimental import pallas as pl
from jax.experimental.pallas import tpu as pltpu

F32 = jnp.float32
BF16 = jnp.bfloat16

CHUNK = 64
H_A = 4
DK_A = 64
H_R = 4
HEAD_W = 128
SEG_W = H_A * HEAD_W
N_SEG = 7
N_BUCKETS = 32
ROPE_BASE = 10000.0
EPS = 1e-6
NEG_INF = -1e30
LOG2E = math.log2(math.e)
BUCKET_THRESHOLDS = (12, 16, 23, 32, 46, 64, 91)
FAR_BUCKET = 15
MAX_LOGIT_SPREAD = 100.0

MXU_W = 256
VMEM_LIMIT = 56 * 1024 * 1024


def _lambda_init(layer):
    return 0.8 - 0.6 * math.exp(-0.3 * layer)


def _params(sem):
    return pltpu.CompilerParams(dimension_semantics=sem, vmem_limit_bytes=VMEM_LIMIT)


def _resident(shape):
    return pl.BlockSpec(shape, lambda *_: (0,) * len(shape), pipeline_mode=pl.Buffered(1))


def _dot(a, b):
    return jnp.dot(a, b, preferred_element_type=F32)


def _dot_nt(a, b):
    return lax.dot_general(a, b, (((1,), (1,)), ((), ())), preferred_element_type=F32)


def _dot_tn(a, b):
    return lax.dot_general(a, b, (((0,), (0,)), ((), ())), preferred_element_type=F32)


def _silu(x):
    return x * jax.nn.sigmoid(x)


def _mod_kernel(c_ref, w_ref, b_ref, o_ref):
    s = _silu(c_ref[...]).astype(BF16)
    o_ref[...] = _dot(s, w_ref[...].astype(BF16)) + b_ref[...]


def _modulation(c_all, w_ada, b_ada):
    n, d = c_all.shape
    e = w_ada.shape[1]
    te = 1536
    assert e % te == 0
    return pl.pallas_call(
        _mod_kernel,
        grid=(e // te,),
        in_specs=[pl.BlockSpec((n, d), lambda j: (0, 0)),
                  pl.BlockSpec((d, te), lambda j: (0, j)),
                  pl.BlockSpec((1, te), lambda j: (0, j))],
        out_specs=pl.BlockSpec((n, te), lambda j: (0, j)),
        out_shape=jax.ShapeDtypeStruct((n, e), F32),
        compiler_params=_params(("parallel",)),
        name="modulation",
    )(c_all, w_ada, b_ada.reshape(1, e))


def _bias_table_kernel(rb_ref, o_ref, *, offset):
    h = pl.program_id(0)
    _, rows, cols = o_ref.shape
    r = lax.broadcasted_iota(jnp.int32, (rows, cols), 0)
    kc = lax.broadcasted_iota(jnp.int32, (rows, cols), 1) - offset
    rel = kc - r
    n = jnp.abs(rel)
    large = jnp.full((rows, cols), N_BUCKETS // 4, jnp.int32)
    for thr in BUCKET_THRESHOLDS:
        large = large + (n >= thr).astype(jnp.int32)
    bucket = jnp.where(rel > 0, N_BUCKETS // 2, 0) + jnp.where(n < N_BUCKETS // 4, n, large)
    val = jnp.zeros((rows, cols), F32)
    for b in range(N_BUCKETS):
        val = jnp.where(bucket == b, rb_ref[b, h], val)
    val = (val - rb_ref[FAR_BUCKET, h]) * LOG2E
    shift = int(math.log2(CHUNK))
    visible = lax.shift_right_arithmetic(kc, shift) <= lax.shift_right_arithmetic(r, shift)
    o_ref[0] = jnp.where(visible, val, NEG_INF)


def _bias_table(rel_bias, rows, cols, offset):
    assert offset >= BUCKET_THRESHOLDS[-1] and offset % CHUNK == 0
    return pl.pallas_call(
        functools.partial(_bias_table_kernel, offset=offset),
        grid=(H_A,),
        in_specs=[pl.BlockSpec(memory_space=pltpu.SMEM)],
        out_specs=pl.BlockSpec((1, rows, cols), lambda h: (h, 0, 0)),
        out_shape=jax.ShapeDtypeStruct((H_A, rows, cols), F32),
        compiler_params=_params(("parallel",)),
        name="bias_table",
    )(rel_bias)


def _inproj_kernel(x_ref, mod_ref, g1_ref, w_ref, gq_ref, gk_ref, rope_ref, cr_ref, sr_ref,
                   kf_ref, vf_ref, qa_ref, ka_ref, va_ref, qr_ref, kr_ref, vr_ref, sg_ref):
    g, r, d = x_ref.shape
    m = g * r
    x = x_ref[...]
    gain = g1_ref[...] * (1.0 + mod_ref[:, 1:2, :])
    hmod = x * lax.rsqrt(jnp.mean(x * x, axis=-1, keepdims=True) + EPS) * gain + mod_ref[:, 0:1, :]
    hb = hmod.astype(BF16).reshape(m, d)

    def seg(j):
        return _dot(hb, w_ref[:, j * SEG_W:(j + 1) * SEG_W])

    def put(ref, val):
        ref[...] = val.reshape(g, r, SEG_W).astype(ref.dtype)

    gi = lax.broadcasted_iota(jnp.int32, (MXU_W, MXU_W), 0) // DK_A
    gj = lax.broadcasted_iota(jnp.int32, (MXU_W, MXU_W), 1) // DK_A
    ones_bd = (gi == gj).astype(BF16)

    def group_rms(z):
        sq = (z * z).astype(BF16)
        ms = jnp.concatenate([_dot(sq[:, c:c + MXU_W], ones_bd) for c in range(0, SEG_W, MXU_W)], axis=1)
        return z * lax.rsqrt(ms * (1.0 / DK_A) + EPS)

    cr, sr = cr_ref[...], sr_ref[...]

    def table(c_row, s_row, sign):
        t = rope_ref[0, c_row:c_row + 1, :] * cr + sign * (rope_ref[0, s_row:s_row + 1, :] * sr)
        return jnp.broadcast_to(t[None], (g, r, HEAD_W)).reshape(m, HEAD_W)

    cos, sine, sino = table(0, 1, -1.0), table(3, 2, 1.0), table(5, 4, 1.0)

    def rotate(z):
        outs = []
        for hh in range(H_R):
            zh = z[:, hh * HEAD_W:(hh + 1) * HEAD_W]
            nxt = pltpu.roll(zh, HEAD_W - 1, axis=1)
            prv = pltpu.roll(zh, 1, axis=1)
            outs.append(zh * cos + nxt * sine + prv * sino)
        return jnp.concatenate(outs, axis=1)

    put(qa_ref, group_rms(seg(0)) * gq_ref[...] * (DK_A ** -0.5 * LOG2E))
    ka = group_rms(seg(1)) * gk_ref[...]
    put(kf_ref, ka)
    put(ka_ref, ka)
    put(qr_ref, rotate(seg(3)))
    put(kr_ref, rotate(seg(4)) * (HEAD_W ** -0.5))
    put(sg_ref, _silu(seg(6)))
    va = seg(2)
    for hh in range(H_A):
        vf_ref[:, pl.ds(hh, r, stride=H_A), :] = va[:, hh * HEAD_W:(hh + 1) * HEAD_W].reshape(g, r, HEAD_W)
    put(va_ref, va)
    put(vr_ref, seg(5))


def _inproj(x, mod, g1, w_in, gq_t, gk_t, tables, g_blk, r_blk):
    nb, l, d = x.shape
    assert nb % g_blk == 0 and l % r_blk == 0
    nl = l // r_blk
    tok = lambda w: pl.BlockSpec((g_blk, r_blk, w), lambda i: (i // nl, i % nl, 0))
    base = pl.BlockSpec((1, 8, HEAD_W), lambda i: (i % nl, 0, 0))
    tab = _resident((r_blk, HEAD_W))
    out = lambda dt: jax.ShapeDtypeStruct((nb, l, SEG_W), dt)
    return pl.pallas_call(
        _inproj_kernel,
        grid=((nb // g_blk) * nl,),
        in_specs=[tok(d),
                  pl.BlockSpec((g_blk, 6, d), lambda i: (i // nl, 0, 0)),
                  _resident((1, d)), _resident(w_in.shape), _resident((1, SEG_W)), _resident((1, SEG_W)),
                  base, tab, tab],
        out_specs=[tok(SEG_W), pl.BlockSpec((g_blk, r_blk * H_A, HEAD_W), lambda i: (i // nl, i % nl, 0))]
        + [tok(SEG_W)] * 7,
        out_shape=[out(F32), jax.ShapeDtypeStruct((nb, l * H_A, HEAD_W), F32)] + [out(BF16)] * 7,
        compiler_params=_params(("parallel",)),
        name="inproj",
    )(x, mod, g1, w_in, gq_t, gk_t, *tables)


def _stack_maps(q):
    lane = lax.broadcasted_iota(jnp.int32, q.shape, 1)
    zero = jnp.zeros_like(q)
    return jnp.concatenate([jnp.where(lane < DK_A, q, zero), jnp.where(lane >= DK_A, q, zero)], axis=0)


def _lambda(lq1, lk1, lq2, lk2, lam_init):
    s1 = jnp.sum(lq1[...] * lk1[...], axis=-1, keepdims=True)
    s2 = jnp.sum(lq2[...] * lk2[...], axis=-1, keepdims=True)
    return jnp.exp(s1) - jnp.exp(s2) + lam_init


def _diff_finish(acc, l, lam, gs, lam_init):
    t = acc.shape[0] // 2
    o = acc[:t] * (1.0 / l[:t]) - lam * (acc[t:] * (1.0 / l[t:]))
    on = o * lax.rsqrt(jnp.mean(o * o, axis=-1, keepdims=True) + EPS)
    return on * gs * (1.0 - lam_init)


def _attn_prompt_kernel(bound_ref, q_ref, k_ref, v_ref, tbl_ref, lq1, lk1, lq2, lk2, gs_ref, o_ref,
                        qq_sc, m_sc, l_sc, acc_sc, *, lam_init, tk, tkd, near):
    qi = pl.program_id(2)
    tq = q_ref.shape[1]
    q = q_ref[0]
    lane = lax.broadcasted_iota(jnp.int32, q.shape, 1)
    zero = jnp.zeros_like(q)
    qq_sc[0] = jnp.where(lane < DK_A, q, zero)
    qq_sc[1] = jnp.where(lane >= DK_A, q, zero)

    def tile(start, width, row0, corner, diag, bounded, first=False):
        nr = tq - row0
        nblk = width // HEAD_W
        rows = slice(row0, tq)
        flat = lambda x: x.reshape(2 * nr, x.shape[-1])
        kt = k_ref[0, pl.ds(start, width), :]
        vt = v_ref[0, pl.ds(start, width), :]
        s = _dot_nt(flat(qq_sc[:, rows, :]), kt)
        blocks = [s[:, c * HEAD_W:(c + 1) * HEAD_W] for c in range(nblk)]

        def add_rows(blk, r0, val):
            r1 = r0 + val.shape[0]
            cuts = [(0, r0, False), (r0, r1, True), (r1, nr + r0, False), (nr + r0, nr + r1, True),
                    (nr + r1, 2 * nr, False)]
            return jnp.concatenate([blk[a:b] + val if hit else blk[a:b] for a, b, hit in cuts if b > a], axis=0)

        if diag:
            blocks = [add_rows(blk, 0, tbl_ref[0, :, near + c * HEAD_W:near + (c + 1) * HEAD_W])
                      for c, blk in enumerate(blocks)]
        corner_row0 = width if diag else 0
        if corner_row0 < nr:
            blocks[-1] = add_rows(blocks[-1], corner_row0, corner)
        if bounded:
            ps = [jnp.exp2(blk - bound_ref[0]) for blk in blocks]
            l_new = functools.reduce(jnp.add, ps)
            p = jnp.concatenate([pc.astype(BF16) for pc in ps], axis=1)
            acc_new = _dot(p, vt)
            if not first:
                l_new = flat(l_sc[:, rows, :]) + l_new
                acc_new = flat(acc_sc[:, rows, :]) + acc_new
            l_sc[:, rows, :] = l_new.reshape(2, nr, HEAD_W)
            acc_sc[:, rows, :] = acc_new.reshape(2, nr, HEAD_W)
            return
        m_old = flat(m_sc[:, rows, :])
        mx = functools.reduce(jnp.maximum, blocks)
        m_new = jnp.maximum(m_old, jnp.max(mx, axis=-1, keepdims=True))
        alpha = jnp.exp2(m_old - m_new)
        ps = [jnp.exp2(blk - m_new) for blk in blocks]
        l_new = alpha * flat(l_sc[:, rows, :]) + functools.reduce(jnp.add, ps)
        p = jnp.concatenate([pc.astype(BF16) for pc in ps], axis=1)
        acc_new = alpha * flat(acc_sc[:, rows, :]) + _dot(p, vt)
        m_sc[:, rows, :] = m_new.reshape(2, nr, HEAD_W)
        l_sc[:, rows, :] = l_new.reshape(2, nr, HEAD_W)
        acc_sc[:, rows, :] = acc_new.reshape(2, nr, HEAD_W)

    n_far = qi * (tq // tk)
    corner_tbl = tbl_ref[0, :near, :near]

    def diag_tiles(bounded, first):
        for d in range(tq // tkd):
            tile(pl.multiple_of(qi * tq + d * tkd, tkd), tkd, d * tkd, corner_tbl, True, bounded, first and d == 0)

    def run(bounded):
        if bounded:
            diag_tiles(True, True)

        def far_tile(j):
            flag = (j == n_far - 1).astype(F32)
            tile(pl.multiple_of(j * tk, tk), tk, 0, corner_tbl * flag, False, bounded)

        odd = jnp.bitwise_and(n_far, 1)

        @pl.when(odd == 1)
        def _():
            far_tile(0)

        def far_pair(i, carry):
            far_tile(odd + 2 * i)
            far_tile(odd + 2 * i + 1)
            return carry

        lax.fori_loop(0, lax.shift_right_logical(n_far, 1), far_pair, 0)
        if not bounded:
            diag_tiles(False, False)

    @pl.when(bound_ref[1] > 0.5)
    def _():
        run(True)

    @pl.when(bound_ref[1] <= 0.5)
    def _():
        m_sc[...] = jnp.full(m_sc.shape, NEG_INF, F32)
        l_sc[...] = jnp.zeros(l_sc.shape, F32)
        acc_sc[...] = jnp.zeros(acc_sc.shape, F32)
        run(False)

    lam = _lambda(lq1, lk1, lq2, lk2, lam_init)
    l = jnp.sum(l_sc[...], axis=-1, keepdims=True).reshape(2 * tq, 1)
    acc = acc_sc[...].reshape(2 * tq, HEAD_W)
    o_ref[0] = _diff_finish(acc, l, lam, gs_ref[...], lam_init).astype(o_ref.dtype)


def _attn_prompt(bound, q, k, v, tbl, lams, gs, lam_init, tq, tk, tkd, near):
    b, s, _ = q.shape
    assert s % tq == 0 and tq % tk == 0 and tq % tkd == 0 and tbl.shape == (H_A, tkd, near + tkd)
    assert near == HEAD_W and tk % HEAD_W == 0 and tkd % HEAD_W == 0
    vec = _resident((1, DK_A))
    return pl.pallas_call(
        functools.partial(_attn_prompt_kernel, lam_init=lam_init, tk=tk, tkd=tkd, near=near),
        grid=(b, H_A, s // tq),
        in_specs=[pl.BlockSpec(memory_space=pltpu.SMEM),
                  pl.BlockSpec((1, tq, HEAD_W), lambda bi, h, qi: (bi, qi, h)),
                  pl.BlockSpec((1, s, HEAD_W), lambda bi, h, qi: (bi, 0, h)),
                  pl.BlockSpec((1, s, HEAD_W), lambda bi, h, qi: (bi, 0, h)),
                  pl.BlockSpec((1, tkd, near + tkd), lambda bi, h, qi: (h, 0, 0)),
                  vec, vec, vec, vec, _resident((1, HEAD_W))],
        out_specs=pl.BlockSpec((1, tq, HEAD_W), lambda bi, h, qi: (bi, qi, h)),
        out_shape=jax.ShapeDtypeStruct((b, s, SEG_W), BF16),
        scratch_shapes=[pltpu.VMEM((2, tq, HEAD_W), BF16)] + [pltpu.VMEM((2, tq, HEAD_W), F32)] * 3,
        compiler_params=_params(("parallel", "parallel", "arbitrary")),
        name="attn_prompt",
    )(bound, q, k, v, tbl, *lams, gs)


def _log_gamma(h):
    vals = [math.log(1.0 - 2.0 ** (-5.0 - i)) for i in range(H_R)]
    lg = jnp.float32(vals[H_R - 1])
    for i in range(H_R - 2, -1, -1):
        lg = jnp.where(h == i, jnp.float32(vals[i]), lg)
    return lg


def _retention_chunk(q, k, v, state, lg):
    c = q.shape[0]
    ti = lax.broadcasted_iota(jnp.int32, (c, c), 0)
    si = lax.broadcasted_iota(jnp.int32, (c, c), 1)
    dist = (ti - si).astype(F32)
    decay = jnp.where(dist >= 0, jnp.exp(jnp.maximum(dist, 0.0) * lg), 0.0)
    scores = _dot_nt(q, k) * decay
    row = lax.broadcasted_iota(jnp.int32, (c, 1), 0).astype(F32)
    q_dec = (q.astype(F32) * jnp.exp((row + 1.0) * lg)).astype(BF16)
    o = _dot(scores.astype(BF16), v) + _dot(q_dec, state.astype(BF16))
    k_dec = (k.astype(F32) * jnp.exp((c - 1.0 - row) * lg)).astype(BF16)
    new_state = jnp.exp(c * lg) * state + _dot_tn(k_dec, v)
    return o, new_state


def _retention_out(o, sg):
    on = o * lax.rsqrt(jnp.mean(o * o, axis=-1, keepdims=True) + EPS)
    return on * sg.astype(F32)


def _ret_prompt_kernel(q_ref, k_ref, v_ref, sg_ref, o_ref, st_ref, s_sc, d_sc, qd_sc, kd_sc, *, c):
    h = pl.program_id(1)
    ti = pl.program_id(2)
    tb = q_ref.shape[1]

    @pl.when(ti == 0)
    def _():
        lg = _log_gamma(h)
        s_sc[...] = jnp.zeros(s_sc.shape, F32)
        tt = lax.broadcasted_iota(jnp.int32, (c, c), 0)
        ss = lax.broadcasted_iota(jnp.int32, (c, c), 1)
        dist = (tt - ss).astype(F32)
        d_sc[...] = jnp.where(dist >= 0, jnp.exp(jnp.maximum(dist, 0.0) * lg), 0.0)
        row = lax.broadcasted_iota(jnp.int32, (c, HEAD_W), 0).astype(F32)
        qd_sc[...] = jnp.exp((row + 1.0) * lg)
        kd_sc[...] = jnp.exp((c - 1.0 - row) * lg)

    state = s_sc[...]
    g_c = qd_sc[c - 1:c, :]
    for i in range(tb // c):
        rows = slice(i * c, (i + 1) * c)
        q, k, v = q_ref[0, rows, :], k_ref[0, rows, :], v_ref[0, rows, :]
        scores = _dot_nt(q, k) * d_sc[...]
        q_dec = (q.astype(F32) * qd_sc[...]).astype(BF16)
        o = _dot(scores.astype(BF16), v) + _dot(q_dec, state.astype(BF16))
        k_dec = (k.astype(F32) * kd_sc[...]).astype(BF16)
        state = g_c * state + _dot_tn(k_dec, v)
        o_ref[0, rows, :] = _retention_out(o, sg_ref[0, rows, :]).astype(o_ref.dtype)
    s_sc[...] = state

    @pl.when(ti == pl.num_programs(2) - 1)
    def _():
        st_ref[0, 0] = state


def _ret_prompt(qr, kr, vr, sg, t_blk, c):
    b, s, _ = qr.shape
    assert s % t_blk == 0 and t_blk % c == 0
    blk = pl.BlockSpec((1, t_blk, HEAD_W), lambda bi, h, ti: (bi, ti, h))
    return pl.pallas_call(
        functools.partial(_ret_prompt_kernel, c=c),
        grid=(b, H_R, s // t_blk),
        in_specs=[blk, blk, blk, blk],
        out_specs=[blk, pl.BlockSpec((1, 1, HEAD_W, HEAD_W), lambda bi, h, ti: (bi, h, 0, 0))],
        out_shape=[jax.ShapeDtypeStruct((b, s, SEG_W), BF16),
                   jax.ShapeDtypeStruct((b, H_R, HEAD_W, HEAD_W), F32)],
        scratch_shapes=[pltpu.VMEM((HEAD_W, HEAD_W), F32), pltpu.VMEM((c, c), F32),
                        pltpu.VMEM((c, HEAD_W), F32), pltpu.VMEM((c, HEAD_W), F32)],
        compiler_params=_params(("parallel", "parallel", "arbitrary")),
        name="ret_prompt",
    )(qr, kr, vr, sg)


def _mix_sample_kernel(qa_ref, ka_ref, va_ref, ck_ref, cv_ref, tbl_ref, lq1, lk1, lq2, lk2, gs_ref,
                       qr_ref, kr_ref, vr_ref, sg_ref, st_ref, oa_ref, or_ref, so_ref, *, lam_init, near):
    p_len = ck_ref.shape[2]
    l = qa_ref.shape[1]
    lam = _lambda(lq1, lk1, lq2, lk2, lam_init)
    for h in range(H_A):
        cols = slice(h * HEAD_W, (h + 1) * HEAD_W)
        qq = _stack_maps(qa_ref[0, :, cols])
        kc = ck_ref[0, cols, :].astype(BF16)
        vc = cv_ref[0, pl.ds(h, p_len, stride=H_A), :].astype(BF16)
        bias = tbl_ref[h]
        bias2 = jnp.concatenate([bias, bias], axis=0)
        s_c = _dot(qq, kc)
        s_c = jnp.concatenate([s_c[:, :p_len - near], s_c[:, p_len - near:] + bias2[:, :near]], axis=1)
        s_n = _dot_nt(qq, ka_ref[0, :, cols]) + bias2[:, near:]
        m = jnp.maximum(jnp.max(s_c, axis=-1, keepdims=True), jnp.max(s_n, axis=-1, keepdims=True))
        p_c = jnp.exp2(s_c - m)
        p_n = jnp.exp2(s_n - m)
        lsum = jnp.sum(p_c, axis=-1, keepdims=True) + jnp.sum(p_n, axis=-1, keepdims=True)
        acc = _dot(p_c.astype(BF16), vc) + _dot(p_n.astype(BF16), va_ref[0, :, cols])
        oa_ref[0, :, cols] = _diff_finish(acc, lsum, lam, gs_ref[...], lam_init).astype(oa_ref.dtype)
    for h in range(H_R):
        cols = slice(h * HEAD_W, (h + 1) * HEAD_W)
        lg = jnp.float32(math.log(1.0 - 2.0 ** (-5.0 - h)))
        o, new_state = _retention_chunk(qr_ref[0, :, cols], kr_ref[0, :, cols], vr_ref[0, :, cols],
                                        st_ref[0, h], lg)
        or_ref[0, :, cols] = _retention_out(o, sg_ref[0, :, cols]).astype(or_ref.dtype)
        so_ref[0, h] = new_state


def _mix_sample(qa, ka, va, cache_k, cache_v, tbl, lams, gs, qr, kr, vr, sg, state, lam_init, near):
    b, l, _ = qa.shape
    p_len = cache_k.shape[2]
    assert p_len % CHUNK == 0 and l <= CHUNK and p_len >= near
    tok = pl.BlockSpec((1, l, SEG_W), lambda bi: (bi, 0, 0))
    cache_kt = pl.BlockSpec((1, SEG_W, p_len), lambda bi: (bi, 0, 0))
    cache_vh = pl.BlockSpec((1, p_len * H_A, HEAD_W), lambda bi: (bi, 0, 0))
    st = pl.BlockSpec((1, H_R, HEAD_W, HEAD_W), lambda bi: (bi, 0, 0, 0))
    vec = _resident((1, DK_A))
    return pl.pallas_call(
        functools.partial(_mix_sample_kernel, lam_init=lam_init, near=near),
        grid=(b,),
        in_specs=[tok, tok, tok, cache_kt, cache_vh, _resident(tbl.shape), vec, vec, vec, vec,
                  _resident((1, HEAD_W)), tok, tok, tok, tok, st],
        out_specs=[tok, tok, st],
        out_shape=[jax.ShapeDtypeStruct((b, l, SEG_W), BF16), jax.ShapeDtypeStruct((b, l, SEG_W), BF16),
                   jax.ShapeDtypeStruct((b, H_R, HEAD_W, HEAD_W), F32)],
        compiler_params=_params(("parallel",)),
        name="mix_sample",
    )(qa, ka, va, cache_k, cache_v, tbl, *lams, gs, qr, kr, vr, sg, state)


def _ffn_kernel(x_ref, oa_ref, or_ref, mod_ref, g2_ref, wo_ref, wg_ref, wu_ref, wd_ref, y_ref, *, ff_blk):
    g, r, d = x_ref.shape
    m = g * r
    f = wg_ref.shape[1]
    proj = (_dot(oa_ref[...].reshape(m, SEG_W), wo_ref[:SEG_W, :])
            + _dot(or_ref[...].reshape(m, SEG_W), wo_ref[SEG_W:, :]))
    x1 = x_ref[...] + mod_ref[:, 2:3, :] * proj.reshape(g, r, d)
    gain = g2_ref[...] * (1.0 + mod_ref[:, 4:5, :])
    y = x1 * lax.rsqrt(jnp.mean(x1 * x1, axis=-1, keepdims=True) + EPS) * gain
    h2 = (y + mod_ref[:, 3:4, :]).astype(BF16).reshape(m, d)
    acc = jnp.zeros((m, d), F32)
    for c0 in range(0, f, ff_blk):
        c1 = min(c0 + ff_blk, f)
        ff = (_silu(_dot(h2, wg_ref[:, c0:c1])) * _dot(h2, wu_ref[:, c0:c1])).astype(BF16)
        acc = acc + _dot(ff, wd_ref[c0:c1, :])
    y_ref[...] = x1 + mod_ref[:, 5:6, :] * acc.reshape(g, r, d)


def _ffn(x, oa, orr, mod, g2, wo, wg, wu, wd, g_blk, r_blk, ff_blk):
    nb, l, d = x.shape
    assert nb % g_blk == 0 and l % r_blk == 0
    nl = l // r_blk
    tok = lambda w: pl.BlockSpec((g_blk, r_blk, w), lambda i: (i // nl, i % nl, 0))
    return pl.pallas_call(
        functools.partial(_ffn_kernel, ff_blk=ff_blk),
        grid=((nb // g_blk) * nl,),
        in_specs=[tok(d), tok(SEG_W), tok(SEG_W),
                  pl.BlockSpec((g_blk, 6, d), lambda i: (i // nl, 0, 0)),
                  _resident((1, d)), _resident(wo.shape), _resident(wg.shape), _resident(wu.shape),
                  _resident(wd.shape)],
        out_specs=tok(d),
        out_shape=jax.ShapeDtypeStruct((nb, l, d), F32),
        compiler_params=_params(("parallel",)),
        name="outproj_ffn",
    )(x, oa, orr, mod, g2, wo, wg, wu, wd)


def _rope_tables(first_pos, n_blocks, blk):
    inv = 1.0 / (ROPE_BASE ** jnp.linspace(0.0, 1.0, HEAD_W // 2, dtype=F32))
    inv2 = jnp.repeat(inv, 2)[None, :]
    even = (jnp.arange(HEAD_W) % 2 == 0)[None, :]
    me, mo = jnp.where(even, -1.0, 0.0), jnp.where(even, 0.0, 1.0)
    t0 = (first_pos + blk * jnp.arange(n_blocks, dtype=jnp.int32)).astype(F32)[:, None] * inv2
    cb, sb = jnp.cos(t0), jnp.sin(t0)
    zero = jnp.zeros_like(cb)
    base = jnp.stack([cb, sb, cb * me, sb * me, cb * mo, sb * mo, zero, zero], axis=1)
    off = jnp.arange(blk, dtype=jnp.int32).astype(F32)[:, None] * inv2
    return base, jnp.cos(off), jnp.sin(off)


def _logit_bound(g_q, g_k, rel_bias):
    slack = 1.02
    dot_max = slack * DK_A * (DK_A ** -0.5 * LOG2E) * jnp.max(jnp.abs(g_q)) * jnp.max(jnp.abs(g_k))
    rb = (rel_bias - rel_bias[FAR_BUCKET][None, :]) * LOG2E
    b_hi = jnp.maximum(jnp.max(rb), 0.0)
    b_lo = jnp.minimum(jnp.min(rb), 0.0)
    ok = (2.0 * dot_max + (b_hi - b_lo)) <= MAX_LOGIT_SPREAD
    return jnp.stack([dot_max + b_hi, ok.astype(F32)])


def _tile_rows(n, target):
    t = min(n, target)
    while n % t:
        t //= 2
    return t


def kernel(x_prompt, x_sample, c_prompt, c_sample, cache_k, cache_v, state_ret, w_ada, b_ada, g_norm1, g_norm2, w_in, g_q, g_k, lam_q1, lam_k1, lam_q2, lam_k2, g_subln, w_out, w_ff_gate, w_ff_up, w_ff_down, rel_bias):
    depth = w_ada.shape[0]
    b, s, d = x_prompt.shape
    bs, l, _ = x_sample.shape
    p_len = cache_k.shape[2]
    assert w_in.shape[2] == N_SEG * SEG_W and d == 2 * SEG_W

    tm = _tile_rows(s, 512)
    tm_in = _tile_rows(s, 1024)
    tq = _tile_rows(s, 1024)
    tk = _tile_rows(tq, 1024)
    tkd = _tile_rows(tq, 256)
    ret_blk = _tile_rows(s, 4096)
    ret_c = _tile_rows(ret_blk, 256)
    near = HEAD_W
    assert tq % CHUNK == 0 and tk % CHUNK == 0 and tkd % CHUNK == 0 and near >= BUCKET_THRESHOLDS[-1]

    tab_p = _rope_tables(0, s // tm_in, tm_in)
    tab_s = _rope_tables(p_len, 1, l)
    tbl_p = _bias_table(rel_bias, tkd, near + tkd, near)
    tbl_s = _bias_table(rel_bias, l, near + l, near)
    c_all = jnp.concatenate([c_prompt, c_sample], axis=0)

    xp, xs = x_prompt, x_sample
    outs = [[] for _ in range(6)]
    for layer in range(depth):
        lam_init = _lambda_init(layer)
        mod = _modulation(c_all, w_ada[layer], b_ada[layer]).reshape(b + bs, 6, d)
        mod_p, mod_s = mod[:b], mod[b:]
        g1 = g_norm1[layer].reshape(1, d)
        g2 = g_norm2[layer].reshape(1, d)
        gq_t = jnp.tile(g_q[layer], SEG_W // DK_A).reshape(1, SEG_W)
        gk_t = jnp.tile(g_k[layer], SEG_W // DK_A).reshape(1, SEG_W)
        lams = [v[layer].reshape(1, DK_A) for v in (lam_q1, lam_k1, lam_q2, lam_k2)]
        gs = g_subln[layer].reshape(1, HEAD_W)
        wi = w_in[layer].astype(BF16)
        wo = w_out[layer].astype(BF16)
        wg = w_ff_gate[layer].astype(BF16)
        wu = w_ff_up[layer].astype(BF16)
        wd = w_ff_down[layer].astype(BF16)
        ff_blk = 1024

        kf, vf, qa, ka, va, qr, kr, vr, sg = _inproj(xp, mod_p, g1, wi, gq_t, gk_t, tab_p, 1, tm_in)
        bound = _logit_bound(g_q[layer], g_k[layer], rel_bias)
        oa = _attn_prompt(bound, qa, ka, va, tbl_p, lams, gs, lam_init, tq, tk, tkd, near)
        orr, st_p = _ret_prompt(qr, kr, vr, sg, ret_blk, ret_c)
        xp = _ffn(xp, oa, orr, mod_p, g2, wo, wg, wu, wd, 1, tm, ff_blk)

        kfs, vfs, qas, kas, vas, qrs, krs, vrs, sgs = _inproj(xs, mod_s, g1, wi, gq_t, gk_t, tab_s, bs, l)
        ck = jnp.swapaxes(cache_k[layer].reshape(bs, p_len, SEG_W), 1, 2)
        cv = cache_v[layer].reshape(bs, p_len * H_A, HEAD_W)
        oas, ors, st_s = _mix_sample(qas, kas, vas, ck, cv, tbl_s, lams, gs, qrs, krs, vrs, sgs,
                                     state_ret[layer], lam_init, near)
        xs = _ffn(xs, oas, ors, mod_s, g2, wo, wg, wu, wd, bs, l, ff_blk)

        for lst, val in zip(outs, (kf.reshape(b, s, H_A, 2, DK_A), vf.reshape(b, s, H_A, HEAD_W), st_p,
                                   kfs.reshape(bs, l, H_A, 2, DK_A), vfs.reshape(bs, l, H_A, HEAD_W), st_s)):
            lst.append(val)
    return (xp, xs) + tuple(jnp.stack(o) for o in outs)
```

```python
import functools
import math

import jax
import jax.numpy as jnp
from jax import lax
from jax.experimental import pallas as pl
from jax.experimental.pallas import tpu as pltpu

F32 = jnp.float32
BF16 = jnp.bfloat16

CHUNK = 64
H_A = 4
DK_A = 64
H_R = 4
HEAD_W = 128
SEG_W = H_A * HEAD_W
N_SEG = 7
N_BUCKETS = 32
ROPE_BASE = 10000.0
EPS = 1e-6
NEG_INF = -1e30
LOG2E = math.log2(math.e)
BUCKET_THRESHOLDS = (12, 16, 23, 32, 46, 64, 91)
FAR_BUCKET = 15
MAX_LOGIT_SPREAD = 100.0

MXU_W = 256
VMEM_LIMIT = 56 * 1024 * 1024


def _lambda_init(layer):
    return 0.8 - 0.6 * math.exp(-0.3 * layer)


def _params(sem):
    return pltpu.CompilerParams(dimension_semantics=sem, vmem_limit_bytes=VMEM_LIMIT)


def _resident(shape):
    return pl.BlockSpec(shape, lambda *_: (0,) * len(shape), pipeline_mode=pl.Buffered(1))


def _dot(a, b):
    return jnp.dot(a, b, preferred_element_type=F32)


def _dot_nt(a, b):
    return lax.dot_general(a, b, (((1,), (1,)), ((), ())), preferred_element_type=F32)


def _dot_tn(a, b):
    return lax.dot_general(a, b, (((0,), (0,)), ((), ())), preferred_element_type=F32)


def _silu(x):
    return x * jax.nn.sigmoid(x)


def _mod_kernel(c_ref, w_ref, b_ref, o_ref):
    s = _silu(c_ref[...]).astype(BF16)
    o_ref[...] = _dot(s, w_ref[...].astype(BF16)) + b_ref[...]


def _modulation(c_all, w_ada, b_ada):
    n, d = c_all.shape
    e = w_ada.shape[1]
    te = 1536
    assert e % te == 0
    return pl.pallas_call(
        _mod_kernel,
        grid=(e // te,),
        in_specs=[pl.BlockSpec((n, d), lambda j: (0, 0)),
                  pl.BlockSpec((d, te), lambda j: (0, j)),
                  pl.BlockSpec((1, te), lambda j: (0, j))],
        out_specs=pl.BlockSpec((n, te), lambda j: (0, j)),
        out_shape=jax.ShapeDtypeStruct((n, e), F32),
        compiler_params=_params(("parallel",)),
        name="modulation",
    )(c_all, w_ada, b_ada.reshape(1, e))


def _bias_table_kernel(rb_ref, o_ref, *, offset):
    h = pl.program_id(0)
    _, rows, cols = o_ref.shape
    r = lax.broadcasted_iota(jnp.int32, (rows, cols), 0)
    kc = lax.broadcasted_iota(jnp.int32, (rows, cols), 1) - offset
    rel = kc - r
    n = jnp.abs(rel)
    large = jnp.full((rows, cols), N_BUCKETS // 4, jnp.int32)
    for thr in BUCKET_THRESHOLDS:
        large = large + (n >= thr).astype(jnp.int32)
    bucket = jnp.where(rel > 0, N_BUCKETS // 2, 0) + jnp.where(n < N_BUCKETS // 4, n, large)
    val = jnp.zeros((rows, cols), F32)
    for b in range(N_BUCKETS):
        val = jnp.where(bucket == b, rb_ref[b, h], val)
    val = (val - rb_ref[FAR_BUCKET, h]) * LOG2E
    shift = int(math.log2(CHUNK))
    visible = lax.shift_right_arithmetic(kc, shift) <= lax.shift_right_arithmetic(r, shift)
    o_ref[0] = jnp.where(visible, val, NEG_INF)


def _bias_table(rel_bias, rows, cols, offset):
    assert offset >= BUCKET_THRESHOLDS[-1] and offset % CHUNK == 0
    return pl.pallas_call(
        functools.partial(_bias_table_kernel, offset=offset),
        grid=(H_A,),
        in_specs=[pl.BlockSpec(memory_space=pltpu.SMEM)],
        out_specs=pl.BlockSpec((1, rows, cols), lambda h: (h, 0, 0)),
        out_shape=jax.ShapeDtypeStruct((H_A, rows, cols), F32),
        compiler_params=_params(("parallel",)),
        name="bias_table",
    )(rel_bias)


def _inproj_kernel(x_ref, mod_ref, g1_ref, w_ref, gq_ref, gk_ref, rope_ref, cr_ref, sr_ref,
                   kf_ref, vf_ref, qa_ref, ka_ref, va_ref, qr_ref, kr_ref, vr_ref, sg_ref):
    g, r, d = x_ref.shape
    m = g * r
    x = x_ref[...]
    gain = g1_ref[...] * (1.0 + mod_ref[:, 1:2, :])
    hmod = x * lax.rsqrt(jnp.mean(x * x, axis=-1, keepdims=True) + EPS) * gain + mod_ref[:, 0:1, :]
    hb = hmod.astype(BF16).reshape(m, d)

    def seg(j):
        return _dot(hb, w_ref[:, j * SEG_W:(j + 1) * SEG_W])

    def put(ref, val):
        ref[...] = val.reshape(g, r, SEG_W).astype(ref.dtype)

    gi = lax.broadcasted_iota(jnp.int32, (MXU_W, MXU_W), 0) // DK_A
    gj = lax.broadcasted_iota(jnp.int32, (MXU_W, MXU_W), 1) // DK_A
    ones_bd = (gi == gj).astype(BF16)

    def group_rms(z):
        sq = (z * z).astype(BF16)
        ms = jnp.concatenate([_dot(sq[:, c:c + MXU_W], ones_bd) for c in range(0, SEG_W, MXU_W)], axis=1)
        return z * lax.rsqrt(ms * (1.0 / DK_A) + EPS)

    cr, sr = cr_ref[...], sr_ref[...]

    def table(c_row, s_row, sign):
        t = rope_ref[0, c_row:c_row + 1, :] * cr + sign * (rope_ref[0, s_row:s_row + 1, :] * sr)
        return jnp.broadcast_to(t[None], (g, r, HEAD_W)).reshape(m, HEAD_W)

    cos, sine, sino = table(0, 1, -1.0), table(3, 2, 1.0), table(5, 4, 1.0)

    def rotate(z):
        outs = []
        for hh in range(H_R):
            zh = z[:, hh * HEAD_W:(hh + 1) * HEAD_W]
            nxt = pltpu.roll(zh, HEAD_W - 1, axis=1)
            prv = pltpu.roll(zh, 1, axis=1)
            outs.append(zh * cos + nxt * sine + prv * sino)
        return jnp.concatenate(outs, axis=1)

    put(qa_ref, group_rms(seg(0)) * gq_ref[...] * (DK_A ** -0.5 * LOG2E))
    ka = group_rms(seg(1)) * gk_ref[...]
    put(kf_ref, ka)
    put(ka_ref, ka)
    put(qr_ref, rotate(seg(3)))
    put(kr_ref, rotate(seg(4)) * (HEAD_W ** -0.5))
    put(sg_ref, _silu(seg(6)))
    va = seg(2)
    for hh in range(H_A):
        vf_ref[:, pl.ds(hh, r, stride=H_A), :] = va[:, hh * HEAD_W:(hh + 1) * HEAD_W].reshape(g, r, HEAD_W)
    put(va_ref, va)
    put(vr_ref, seg(5))


def _inproj(x, mod, g1, w_in, gq_t, gk_t, tables, g_blk, r_blk):
    nb, l, d = x.shape
    assert nb % g_blk == 0 and l % r_blk == 0
    nl = l // r_blk
    tok = lambda w: pl.BlockSpec((g_blk, r_blk, w), lambda i: (i // nl, i % nl, 0))
    base = pl.BlockSpec((1, 8, HEAD_W), lambda i: (i % nl, 0, 0))
    tab = _resident((r_blk, HEAD_W))
    out = lambda dt: jax.ShapeDtypeStruct((nb, l, SEG_W), dt)
    return pl.pallas_call(
        _inproj_kernel,
        grid=((nb // g_blk) * nl,),
        in_specs=[tok(d),
                  pl.BlockSpec((g_blk, 6, d), lambda i: (i // nl, 0, 0)),
                  _resident((1, d)), _resident(w_in.shape), _resident((1, SEG_W)), _resident((1, SEG_W)),
                  base, tab, tab],
        out_specs=[tok(SEG_W), pl.BlockSpec((g_blk, r_blk * H_A, HEAD_W), lambda i: (i // nl, i % nl, 0))]
        + [tok(SEG_W)] * 7,
        out_shape=[out(F32), jax.ShapeDtypeStruct((nb, l * H_A, HEAD_W), F32)] + [out(BF16)] * 7,
        compiler_params=_params(("parallel",)),
        name="inproj",
    )(x, mod, g1, w_in, gq_t, gk_t, *tables)


def _stack_maps(q):
    lane = lax.broadcasted_iota(jnp.int32, q.shape, 1)
    zero = jnp.zeros_like(q)
    return jnp.concatenate([jnp.where(lane < DK_A, q, zero), jnp.where(lane >= DK_A, q, zero)], axis=0)


def _lambda(lq1, lk1, lq2, lk2, lam_init):
    s1 = jnp.sum(lq1[...] * lk1[...], axis=-1, keepdims=True)
    s2 = jnp.sum(lq2[...] * lk2[...], axis=-1, keepdims=True)
    return jnp.exp(s1) - jnp.exp(s2) + lam_init


def _diff_finish(acc, l, lam, gs, lam_init):
    t = acc.shape[0] // 2
    o = acc[:t] * (1.0 / l[:t]) - lam * (acc[t:] * (1.0 / l[t:]))
    on = o * lax.rsqrt(jnp.mean(o * o, axis=-1, keepdims=True) + EPS)
    return on * gs * (1.0 - lam_init)


def _attn_prompt_kernel(bound_ref, q_ref, k_ref, v_ref, tbl_ref, lq1, lk1, lq2, lk2, gs_ref, o_ref,
                        qq_sc, m_sc, l_sc, acc_sc, *, lam_init, tk, tkd, near):
    qi = pl.program_id(2)
    tq = q_ref.shape[1]
    q = q_ref[0]
    lane = lax.broadcasted_iota(jnp.int32, q.shape, 1)
    zero = jnp.zeros_like(q)
    qq_sc[0] = jnp.where(lane < DK_A, q, zero)
    qq_sc[1] = jnp.where(lane >= DK_A, q, zero)

    def tile(start, width, row0, corner, diag, bounded, first=False):
        nr = tq - row0
        nblk = width // HEAD_W
        rows = slice(row0, tq)
        flat = lambda x: x.reshape(2 * nr, x.shape[-1])
        kt = k_ref[0, pl.ds(start, width), :]
        vt = v_ref[0, pl.ds(start, width), :]
        s = _dot_nt(flat(qq_sc[:, rows, :]), kt)
        blocks = [s[:, c * HEAD_W:(c + 1) * HEAD_W] for c in range(nblk)]

        def add_rows(blk, r0, val):
            r1 = r0 + val.shape[0]
            cuts = [(0, r0, False), (r0, r1, True), (r1, nr + r0, False), (nr + r0, nr + r1, True),
                    (nr + r1, 2 * nr, False)]
            return jnp.concatenate([blk[a:b] + val if hit else blk[a:b] for a, b, hit in cuts if b > a], axis=0)

        if diag:
            blocks = [add_rows(blk, 0, tbl_ref[0, :, near + c * HEAD_W:near + (c + 1) * HEAD_W])
                      for c, blk in enumerate(blocks)]
        corner_row0 = width if diag else 0
        if corner_row0 < nr:
            blocks[-1] = add_rows(blocks[-1], corner_row0, corner)
        if bounded:
            ps = [jnp.exp2(blk - bound_ref[0]) for blk in blocks]
            l_new = functools.reduce(jnp.add, ps)
            p = jnp.concatenate([pc.astype(BF16) for pc in ps], axis=1)
            acc_new = _dot(p, vt)
            if not first:
                l_new = flat(l_sc[:, rows, :]) + l_new
                acc_new = flat(acc_sc[:, rows, :]) + acc_new
            l_sc[:, rows, :] = l_new.reshape(2, nr, HEAD_W)
            acc_sc[:, rows, :] = acc_new.reshape(2, nr, HEAD_W)
            return
        m_old = flat(m_sc[:, rows, :])
        mx = functools.reduce(jnp.maximum, blocks)
        m_new = jnp.maximum(m_old, jnp.max(mx, axis=-1, keepdims=True))
        alpha = jnp.exp2(m_old - m_new)
        ps = [jnp.exp2(blk - m_new) for blk in blocks]
        l_new = alpha * flat(l_sc[:, rows, :]) + functools.reduce(jnp.add, ps)
        p = jnp.concatenate([pc.astype(BF16) for pc in ps], axis=1)
        acc_new = alpha * flat(acc_sc[:, rows, :]) + _dot(p, vt)
        m_sc[:, rows, :] = m_new.reshape(2, nr, HEAD_W)
        l_sc[:, rows, :] = l_new.reshape(2, nr, HEAD_W)
        acc_sc[:, rows, :] = acc_new.reshape(2, nr, HEAD_W)

    n_far = qi * (tq // tk)
    corner_tbl = tbl_ref[0, :near, :near]

    def diag_tiles(bounded, first):
        for d in range(tq // tkd):
            tile(pl.multiple_of(qi * tq + d * tkd, tkd), tkd, d * tkd, corner_tbl, True, bounded, first and d == 0)

    def run(bounded):
        if bounded:
            diag_tiles(True, True)

        def far_tile(j):
            flag = (j == n_far - 1).astype(F32)
            tile(pl.multiple_of(j * tk, tk), tk, 0, corner_tbl * flag, False, bounded)

        odd = jnp.bitwise_and(n_far, 1)

        @pl.when(odd == 1)
        def _():
            far_tile(0)

        def far_pair(i, carry):
            far_tile(odd + 2 * i)
            far_tile(odd + 2 * i + 1)
            return carry

        lax.fori_loop(0, lax.shift_right_logical(n_far, 1), far_pair, 0)
        if not bounded:
            diag_tiles(False, False)

    @pl.when(bound_ref[1] > 0.5)
    def _():
        run(True)

    @pl.when(bound_ref[1] <= 0.5)
    def _():
        m_sc[...] = jnp.full(m_sc.shape, NEG_INF, F32)
        l_sc[...] = jnp.zeros(l_sc.shape, F32)
        acc_sc[...] = jnp.zeros(acc_sc.shape, F32)
        run(False)

    lam = _lambda(lq1, lk1, lq2, lk2, lam_init)
    l = jnp.sum(l_sc[...], axis=-1, keepdims=True).reshape(2 * tq, 1)
    acc = acc_sc[...].reshape(2 * tq, HEAD_W)
    o_ref[0] = _diff_finish(acc, l, lam, gs_ref[...], lam_init).astype(o_ref.dtype)


def _attn_prompt(bound, q, k, v, tbl, lams, gs, lam_init, tq, tk, tkd, near):
    b, s, _ = q.shape
    assert s % tq == 0 and tq % tk == 0 and tq % tkd == 0 and tbl.shape == (H_A, tkd, near + tkd)
    assert near == HEAD_W and tk % HEAD_W == 0 and tkd % HEAD_W == 0
    vec = _resident((1, DK_A))
    return pl.pallas_call(
        functools.partial(_attn_prompt_kernel, lam_init=lam_init, tk=tk, tkd=tkd, near=near),
        grid=(b, H_A, s // tq),
        in_specs=[pl.BlockSpec(memory_space=pltpu.SMEM),
                  pl.BlockSpec((1, tq, HEAD_W), lambda bi, h, qi: (bi, qi, h)),
                  pl.BlockSpec((1, s, HEAD_W), lambda bi, h, qi: (bi, 0, h)),
                  pl.BlockSpec((1, s, HEAD_W), lambda bi, h, qi: (bi, 0, h)),
                  pl.BlockSpec((1, tkd, near + tkd), lambda bi, h, qi: (h, 0, 0)),
                  vec, vec, vec, vec, _resident((1, HEAD_W))],
        out_specs=pl.BlockSpec((1, tq, HEAD_W), lambda bi, h, qi: (bi, qi, h)),
        out_shape=jax.ShapeDtypeStruct((b, s, SEG_W), BF16),
        scratch_shapes=[pltpu.VMEM((2, tq, HEAD_W), BF16)] + [pltpu.VMEM((2, tq, HEAD_W), F32)] * 3,
        compiler_params=_params(("parallel", "parallel", "arbitrary")),
        name="attn_prompt",
    )(bound, q, k, v, tbl, *lams, gs)


def _log_gamma(h):
    vals = [math.log(1.0 - 2.0 ** (-5.0 - i)) for i in range(H_R)]
    lg = jnp.float32(vals[H_R - 1])
    for i in range(H_R - 2, -1, -1):
        lg = jnp.where(h == i, jnp.float32(vals[i]), lg)
    return lg


def _retention_chunk(q, k, v, state, lg):
    c = q.shape[0]
    ti = lax.broadcasted_iota(jnp.int32, (c, c), 0)
    si = lax.broadcasted_iota(jnp.int32, (c, c), 1)
    dist = (ti - si).astype(F32)
    decay = jnp.where(dist >= 0, jnp.exp(jnp.maximum(dist, 0.0) * lg), 0.0)
    scores = _dot_nt(q, k) * decay
    row = lax.broadcasted_iota(jnp.int32, (c, 1), 0).astype(F32)
    q_dec = (q.astype(F32) * jnp.exp((row + 1.0) * lg)).astype(BF16)
    o = _dot(scores.astype(BF16), v) + _dot(q_dec, state.astype(BF16))
    k_dec = (k.astype(F32) * jnp.exp((c - 1.0 - row) * lg)).astype(BF16)
    new_state = jnp.exp(c * lg) * state + _dot_tn(k_dec, v)
    return o, new_state


def _retention_out(o, sg):
    on = o * lax.rsqrt(jnp.mean(o * o, axis=-1, keepdims=True) + EPS)
    return on * sg.astype(F32)


def _ret_prompt_kernel(q_ref, k_ref, v_ref, sg_ref, o_ref, st_ref, s_sc, d_sc, qd_sc, kd_sc, *, c):
    h = pl.program_id(1)
    ti = pl.program_id(2)
    tb = q_ref.shape[1]

    @pl.when(ti == 0)
    def _():
        lg = _log_gamma(h)
        s_sc[...] = jnp.zeros(s_sc.shape, F32)
        tt = lax.broadcasted_iota(jnp.int32, (c, c), 0)
        ss = lax.broadcasted_iota(jnp.int32, (c, c), 1)
        dist = (tt - ss).astype(F32)
        d_sc[...] = jnp.where(dist >= 0, jnp.exp(jnp.maximum(dist, 0.0) * lg), 0.0)
        row = lax.broadcasted_iota(jnp.int32, (c, HEAD_W), 0).astype(F32)
        qd_sc[...] = jnp.exp((row + 1.0) * lg)
        kd_sc[...] = jnp.exp((c - 1.0 - row) * lg)

    state = s_sc[...]
    g_c = qd_sc[c - 1:c, :]
    for i in range(tb // c):
        rows = slice(i * c, (i + 1) * c)
        q, k, v = q_ref[0, rows, :], k_ref[0, rows, :], v_ref[0, rows, :]
        scores = _dot_nt(q, k) * d_sc[...]
        q_dec = (q.astype(F32) * qd_sc[...]).astype(BF16)
        o = _dot(scores.astype(BF16), v) + _dot(q_dec, state.astype(BF16))
        k_dec = (k.astype(F32) * kd_sc[...]).astype(BF16)
        state = g_c * state + _dot_tn(k_dec, v)
        o_ref[0, rows, :] = _retention_out(o, sg_ref[0, rows, :]).astype(o_ref.dtype)
    s_sc[...] = state

    @pl.when(ti == pl.num_programs(2) - 1)
    def _():
        st_ref[0, 0] = state


def _ret_prompt(qr, kr, vr, sg, t_blk, c):
    b, s, _ = qr.shape
    assert s % t_blk == 0 and t_blk % c == 0
    blk = pl.BlockSpec((1, t_blk, HEAD_W), lambda bi, h, ti: (bi, ti, h))
    return pl.pallas_call(
        functools.partial(_ret_prompt_kernel, c=c),
        grid=(b, H_R, s // t_blk),
        in_specs=[blk, blk, blk, blk],
        out_specs=[blk, pl.BlockSpec((1, 1, HEAD_W, HEAD_W), lambda bi, h, ti: (bi, h, 0, 0))],
        out_shape=[jax.ShapeDtypeStruct((b, s, SEG_W), BF16),
                   jax.ShapeDtypeStruct((b, H_R, HEAD_W, HEAD_W), F32)],
        scratch_shapes=[pltpu.VMEM((HEAD_W, HEAD_W), F32), pltpu.VMEM((c, c), F32),
                        pltpu.VMEM((c, HEAD_W), F32), pltpu.VMEM((c, HEAD_W), F32)],
        compiler_params=_params(("parallel", "parallel", "arbitrary")),
        name="ret_prompt",
    )(qr, kr, vr, sg)


def _mix_sample_kernel(qa_ref, ka_ref, va_ref, ck_ref, cv_ref, tbl_ref, lq1, lk1, lq2, lk2, gs_ref,
                       qr_ref, kr_ref, vr_ref, sg_ref, st_ref, oa_ref, or_ref, so_ref, *, lam_init, near):
    p_len = ck_ref.shape[2]
    l = qa_ref.shape[1]
    lam = _lambda(lq1, lk1, lq2, lk2, lam_init)
    for h in range(H_A):
        cols = slice(h * HEAD_W, (h + 1) * HEAD_W)
        qq = _stack_maps(qa_ref[0, :, cols])
        kc = ck_ref[0, cols, :].astype(BF16)
        vc = cv_ref[0, pl.ds(h, p_len, stride=H_A), :].astype(BF16)
        bias = tbl_ref[h]
        bias2 = jnp.concatenate([bias, bias], axis=0)
        s_c = _dot(qq, kc)
        s_c = jnp.concatenate([s_c[:, :p_len - near], s_c[:, p_len - near:] + bias2[:, :near]], axis=1)
        s_n = _dot_nt(qq, ka_ref[0, :, cols]) + bias2[:, near:]
        m = jnp.maximum(jnp.max(s_c, axis=-1, keepdims=True), jnp.max(s_n, axis=-1, keepdims=True))
        p_c = jnp.exp2(s_c - m)
        p_n = jnp.exp2(s_n - m)
        lsum = jnp.sum(p_c, axis=-1, keepdims=True) + jnp.sum(p_n, axis=-1, keepdims=True)
        acc = _dot(p_c.astype(BF16), vc) + _dot(p_n.astype(BF16), va_ref[0, :, cols])
        oa_ref[0, :, cols] = _diff_finish(acc, lsum, lam, gs_ref[...], lam_init).astype(oa_ref.dtype)
    for h in range(H_R):
        cols = slice(h * HEAD_W, (h + 1) * HEAD_W)
        lg = jnp.float32(math.log(1.0 - 2.0 ** (-5.0 - h)))
        o, new_state = _retention_chunk(qr_ref[0, :, cols], kr_ref[0, :, cols], vr_ref[0, :, cols],
                                        st_ref[0, h], lg)
        or_ref[0, :, cols] = _retention_out(o, sg_ref[0, :, cols]).astype(or_ref.dtype)
        so_ref[0, h] = new_state


def _mix_sample(qa, ka, va, cache_k, cache_v, tbl, lams, gs, qr, kr, vr, sg, state, lam_init, near):
    b, l, _ = qa.shape
    p_len = cache_k.shape[2]
    assert p_len % CHUNK == 0 and l <= CHUNK and p_len >= near
    tok = pl.BlockSpec((1, l, SEG_W), lambda bi: (bi, 0, 0))
    cache_kt = pl.BlockSpec((1, SEG_W, p_len), lambda bi: (bi, 0, 0))
    cache_vh = pl.BlockSpec((1, p_len * H_A, HEAD_W), lambda bi: (bi, 0, 0))
    st = pl.BlockSpec((1, H_R, HEAD_W, HEAD_W), lambda bi: (bi, 0, 0, 0))
    vec = _resident((1, DK_A))
    return pl.pallas_call(
        functools.partial(_mix_sample_kernel, lam_init=lam_init, near=near),
        grid=(b,),
        in_specs=[tok, tok, tok, cache_kt, cache_vh, _resident(tbl.shape), vec, vec, vec, vec,
                  _resident((1, HEAD_W)), tok, tok, tok, tok, st],
        out_specs=[tok, tok, st],
        out_shape=[jax.ShapeDtypeStruct((b, l, SEG_W), BF16), jax.ShapeDtypeStruct((b, l, SEG_W), BF16),
                   jax.ShapeDtypeStruct((b, H_R, HEAD_W, HEAD_W), F32)],
        compiler_params=_params(("parallel",)),
        name="mix_sample",
    )(qa, ka, va, cache_k, cache_v, tbl, *lams, gs, qr, kr, vr, sg, state)


def _ffn_kernel(x_ref, oa_ref, or_ref, mod_ref, g2_ref, wo_ref, wg_ref, wu_ref, wd_ref, y_ref, *, ff_blk):
    g, r, d = x_ref.shape
    m = g * r
    f = wg_ref.shape[1]
    proj = (_dot(oa_ref[...].reshape(m, SEG_W), wo_ref[:SEG_W, :])
            + _dot(or_ref[...].reshape(m, SEG_W), wo_ref[SEG_W:, :]))
    x1 = x_ref[...] + mod_ref[:, 2:3, :] * proj.reshape(g, r, d)
    gain = g2_ref[...] * (1.0 + mod_ref[:, 4:5, :])
    y = x1 * lax.rsqrt(jnp.mean(x1 * x1, axis=-1, keepdims=True) + EPS) * gain
    h2 = (y + mod_ref[:, 3:4, :]).astype(BF16).reshape(m, d)
    acc = jnp.zeros((m, d), F32)
    for c0 in range(0, f, ff_blk):
        c1 = min(c0 + ff_blk, f)
        ff = (_silu(_dot(h2, wg_ref[:, c0:c1])) * _dot(h2, wu_ref[:, c0:c1])).astype(BF16)
        acc = acc + _dot(ff, wd_ref[c0:c1, :])
    y_ref[...] = x1 + mod_ref[:, 5:6, :] * acc.reshape(g, r, d)


def _ffn(x, oa, orr, mod, g2, wo, wg, wu, wd, g_blk, r_blk, ff_blk):
    nb, l, d = x.shape
    assert nb % g_blk == 0 and l % r_blk == 0
    nl = l // r_blk
    tok = lambda w: pl.BlockSpec((g_blk, r_blk, w), lambda i: (i // nl, i % nl, 0))
    return pl.pallas_call(
        functools.partial(_ffn_kernel, ff_blk=ff_blk),
        grid=((nb // g_blk) * nl,),
        in_specs=[tok(d), tok(SEG_W), tok(SEG_W),
                  pl.BlockSpec((g_blk, 6, d), lambda i: (i // nl, 0, 0)),
                  _resident((1, d)), _resident(wo.shape), _resident(wg.shape), _resident(wu.shape),
                  _resident(wd.shape)],
        out_specs=tok(d),
        out_shape=jax.ShapeDtypeStruct((nb, l, d), F32),
        compiler_params=_params(("parallel",)),
        name="outproj_ffn",
    )(x, oa, orr, mod, g2, wo, wg, wu, wd)


def _rope_tables(first_pos, n_blocks, blk):
    inv = 1.0 / (ROPE_BASE ** jnp.linspace(0.0, 1.0, HEAD_W // 2, dtype=F32))
    inv2 = jnp.repeat(inv, 2)[None, :]
    even = (jnp.arange(HEAD_W) % 2 == 0)[None, :]
    me, mo = jnp.where(even, -1.0, 0.0), jnp.where(even, 0.0, 1.0)
    t0 = (first_pos + blk * jnp.arange(n_blocks, dtype=jnp.int32)).astype(F32)[:, None] * inv2
    cb, sb = jnp.cos(t0), jnp.sin(t0)
    zero = jnp.zeros_like(cb)
    base = jnp.stack([cb, sb, cb * me, sb * me, cb * mo, sb * mo, zero, zero], axis=1)
    off = jnp.arange(blk, dtype=jnp.int32).astype(F32)[:, None] * inv2
    return base, jnp.cos(off), jnp.sin(off)


def _logit_bound(g_q, g_k, rel_bias):
    slack = 1.02
    dot_max = slack * DK_A * (DK_A ** -0.5 * LOG2E) * jnp.max(jnp.abs(g_q)) * jnp.max(jnp.abs(g_k))
    rb = (rel_bias - rel_bias[FAR_BUCKET][None, :]) * LOG2E
    b_hi = jnp.maximum(jnp.max(rb), 0.0)
    b_lo = jnp.minimum(jnp.min(rb), 0.0)
    ok = (2.0 * dot_max + (b_hi - b_lo)) <= MAX_LOGIT_SPREAD
    return jnp.stack([dot_max + b_hi, ok.astype(F32)])


def _tile_rows(n, target):
    t = min(n, target)
    while n % t:
        t //= 2
    return t


def kernel(x_prompt, x_sample, c_prompt, c_sample, cache_k, cache_v, state_ret, w_ada, b_ada, g_norm1, g_norm2, w_in, g_q, g_k, lam_q1, lam_k1, lam_q2, lam_k2, g_subln, w_out, w_ff_gate, w_ff_up, w_ff_down, rel_bias):
    depth = w_ada.shape[0]
    b, s, d = x_prompt.shape
    bs, l, _ = x_sample.shape
    p_len = cache_k.shape[2]
    assert w_in.shape[2] == N_SEG * SEG_W and d == 2 * SEG_W

    tm = _tile_rows(s, 512)
    tm_in = _tile_rows(s, 1024)
    tq = _tile_rows(s, 4096)
    tk = _tile_rows(tq, 256)
    tkd = _tile_rows(tq, 256)
    ret_blk = _tile_rows(s, 8192)
    ret_c = _tile_rows(ret_blk, 256)
    near = HEAD_W
    assert tq % CHUNK == 0 and tk % CHUNK == 0 and tkd % CHUNK == 0 and near >= BUCKET_THRESHOLDS[-1]

    tab_p = _rope_tables(0, s // tm_in, tm_in)
    tab_s = _rope_tables(p_len, 1, l)
    tbl_p = _bias_table(rel_bias, tkd, near + tkd, near)
    tbl_s = _bias_table(rel_bias, l, near + l, near)
    c_all = jnp.concatenate([c_prompt, c_sample], axis=0)

    xp, xs = x_prompt, x_sample
    outs = [[] for _ in range(6)]
    for layer in range(depth):
        lam_init = _lambda_init(layer)
        mod = _modulation(c_all, w_ada[layer], b_ada[layer]).reshape(b + bs, 6, d)
        mod_p, mod_s = mod[:b], mod[b:]
        g1 = g_norm1[layer].reshape(1, d)
        g2 = g_norm2[layer].reshape(1, d)
        gq_t = jnp.tile(g_q[layer], SEG_W // DK_A).reshape(1, SEG_W)
        gk_t = jnp.tile(g_k[layer], SEG_W // DK_A).reshape(1, SEG_W)
        lams = [v[layer].reshape(1, DK_A) for v in (lam_q1, lam_k1, lam_q2, lam_k2)]
        gs = g_subln[layer].reshape(1, HEAD_W)
        wi = w_in[layer].astype(BF16)
        wo = w_out[layer].astype(BF16)
        wg = w_ff_gate[layer].astype(BF16)
        wu = w_ff_up[layer].astype(BF16)
        wd = w_ff_down[layer].astype(BF16)
        ff_blk = 1024

        kf, vf, qa, ka, va, qr, kr, vr, sg = _inproj(xp, mod_p, g1, wi, gq_t, gk_t, tab_p, 1, tm_in)
        bound = _logit_bound(g_q[layer], g_k[layer], rel_bias)
        oa = _attn_prompt(bound, qa, ka, va, tbl_p, lams, gs, lam_init, tq, tk, tkd, near)
        orr, st_p = _ret_prompt(qr, kr, vr, sg, ret_blk, ret_c)
        xp = _ffn(xp, oa, orr, mod_p, g2, wo, wg, wu, wd, 1, tm, ff_blk)

        kfs, vfs, qas, kas, vas, qrs, krs, vrs, sgs = _inproj(xs, mod_s, g1, wi, gq_t, gk_t, tab_s, bs, l)
        ck = jnp.swapaxes(cache_k[layer].reshape(bs, p_len, SEG_W), 1, 2)
        cv = cache_v[layer].reshape(bs, p_len * H_A, HEAD_W)
        oas, ors, st_s = _mix_sample(qas, kas, vas, ck, cv, tbl_s, lams, gs, qrs, krs, vrs, sgs,
                                     state_ret[layer], lam_init, near)
        xs = _ffn(xs, oas, ors, mod_s, g2, wo, wg, wu, wd, bs, l, ff_blk)

        for lst, val in zip(outs, (kf.reshape(b, s, H_A, 2, DK_A), vf.reshape(b, s, H_A, HEAD_W), st_p,
                                   kfs.reshape(bs, l, H_A, 2, DK_A), vfs.reshape(bs, l, H_A, HEAD_W), st_s)):
            lst.append(val)
    return (xp, xs) + tuple(jnp.stack(o) for o in outs)
```

```python
import functools
import math

import jax
import jax.numpy as jnp
from jax import lax
from jax.experimental import pallas as pl
from jax.experimental.pallas import tpu as pltpu

F32 = jnp.float32
BF16 = jnp.bfloat16

CHUNK = 64
H_A = 4
DK_A = 64
H_R = 4
HEAD_W = 128
SEG_W = H_A * HEAD_W
N_SEG = 7
N_BUCKETS = 32
ROPE_BASE = 10000.0
EPS = 1e-6
NEG_INF = -1e30
LOG2E = math.log2(math.e)
BUCKET_THRESHOLDS = (12, 16, 23, 32, 46, 64, 91)
FAR_BUCKET = 15
MAX_LOGIT_SPREAD = 100.0

MXU_W = 256
VMEM_LIMIT = 56 * 1024 * 1024


def _lambda_init(layer):
    return 0.8 - 0.6 * math.exp(-0.3 * layer)


def _params(sem):
    return pltpu.CompilerParams(dimension_semantics=sem, vmem_limit_bytes=VMEM_LIMIT)


def _resident(shape):
    return pl.BlockSpec(shape, lambda *_: (0,) * len(shape), pipeline_mode=pl.Buffered(1))


def _dot(a, b):
    return jnp.dot(a, b, preferred_element_type=F32)


def _dot_nt(a, b):
    return lax.dot_general(a, b, (((1,), (1,)), ((), ())), preferred_element_type=F32)


def _dot_tn(a, b):
    return lax.dot_general(a, b, (((0,), (0,)), ((), ())), preferred_element_type=F32)


def _silu(x):
    return x * jax.nn.sigmoid(x)


def _mod_kernel(c_ref, w_ref, b_ref, o_ref):
    s = _silu(c_ref[...]).astype(BF16)
    o_ref[...] = _dot(s, w_ref[...].astype(BF16)) + b_ref[...]


def _modulation(c_all, w_ada, b_ada):
    n, d = c_all.shape
    e = w_ada.shape[1]
    te = 1536
    assert e % te == 0
    return pl.pallas_call(
        _mod_kernel,
        grid=(e // te,),
        in_specs=[pl.BlockSpec((n, d), lambda j: (0, 0)),
                  pl.BlockSpec((d, te), lambda j: (0, j)),
                  pl.BlockSpec((1, te), lambda j: (0, j))],
        out_specs=pl.BlockSpec((n, te), lambda j: (0, j)),
        out_shape=jax.ShapeDtypeStruct((n, e), F32),
        compiler_params=_params(("parallel",)),
        name="modulation",
    )(c_all, w_ada, b_ada.reshape(1, e))


def _bias_table_kernel(rb_ref, o_ref, *, offset):
    h = pl.program_id(0)
    _, rows, cols = o_ref.shape
    r = lax.broadcasted_iota(jnp.int32, (rows, cols), 0)
    kc = lax.broadcasted_iota(jnp.int32, (rows, cols), 1) - offset
    rel = kc - r
    n = jnp.abs(rel)
    large = jnp.full((rows, cols), N_BUCKETS // 4, jnp.int32)
    for thr in BUCKET_THRESHOLDS:
        large = large + (n >= thr).astype(jnp.int32)
    bucket = jnp.where(rel > 0, N_BUCKETS // 2, 0) + jnp.where(n < N_BUCKETS // 4, n, large)
    val = jnp.zeros((rows, cols), F32)
    for b in range(N_BUCKETS):
        val = jnp.where(bucket == b, rb_ref[b, h], val)
    val = (val - rb_ref[FAR_BUCKET, h]) * LOG2E
    shift = int(math.log2(CHUNK))
    visible = lax.shift_right_arithmetic(kc, shift) <= lax.shift_right_arithmetic(r, shift)
    o_ref[0] = jnp.where(visible, val, NEG_INF)


def _bias_table(rel_bias, rows, cols, offset):
    assert offset >= BUCKET_THRESHOLDS[-1] and offset % CHUNK == 0
    return pl.pallas_call(
        functools.partial(_bias_table_kernel, offset=offset),
        grid=(H_A,),
        in_specs=[pl.BlockSpec(memory_space=pltpu.SMEM)],
        out_specs=pl.BlockSpec((1, rows, cols), lambda h: (h, 0, 0)),
        out_shape=jax.ShapeDtypeStruct((H_A, rows, cols), F32),
        compiler_params=_params(("parallel",)),
        name="bias_table",
    )(rel_bias)


def _inproj_kernel(x_ref, mod_ref, g1_ref, w_ref, gq_ref, gk_ref, rope_ref, cr_ref, sr_ref,
                   kf_ref, vf_ref, qa_ref, ka_ref, va_ref, qr_ref, kr_ref, vr_ref, sg_ref):
    g, r, d = x_ref.shape
    m = g * r
    x = x_ref[...]
    gain = g1_ref[...] * (1.0 + mod_ref[:, 1:2, :])
    hmod = x * lax.rsqrt(jnp.mean(x * x, axis=-1, keepdims=True) + EPS) * gain + mod_ref[:, 0:1, :]
    hb = hmod.astype(BF16).reshape(m, d)

    def seg(j):
        return _dot(hb, w_ref[:, j * SEG_W:(j + 1) * SEG_W])

    def put(ref, val):
        ref[...] = val.reshape(g, r, SEG_W).astype(ref.dtype)

    gi = lax.broadcasted_iota(jnp.int32, (MXU_W, MXU_W), 0) // DK_A
    gj = lax.broadcasted_iota(jnp.int32, (MXU_W, MXU_W), 1) // DK_A
    ones_bd = (gi == gj).astype(BF16)

    def group_rms(z):
        sq = (z * z).astype(BF16)
        ms = jnp.concatenate([_dot(sq[:, c:c + MXU_W], ones_bd) for c in range(0, SEG_W, MXU_W)], axis=1)
        return z * lax.rsqrt(ms * (1.0 / DK_A) + EPS)

    cr, sr = cr_ref[...], sr_ref[...]

    def table(c_row, s_row, sign):
        t = rope_ref[0, c_row:c_row + 1, :] * cr + sign * (rope_ref[0, s_row:s_row + 1, :] * sr)
        return jnp.broadcast_to(t[None], (g, r, HEAD_W)).reshape(m, HEAD_W)

    cos, sine, sino = table(0, 1, -1.0), table(3, 2, 1.0), table(5, 4, 1.0)

    def rotate(z):
        outs = []
        for hh in range(H_R):
            zh = z[:, hh * HEAD_W:(hh + 1) * HEAD_W]
            nxt = pltpu.roll(zh, HEAD_W - 1, axis=1)
            prv = pltpu.roll(zh, 1, axis=1)
            outs.append(zh * cos + nxt * sine + prv * sino)
        return jnp.concatenate(outs, axis=1)

    put(qa_ref, group_rms(seg(0)) * gq_ref[...] * (DK_A ** -0.5 * LOG2E))
    ka = group_rms(seg(1)) * gk_ref[...]
    put(kf_ref, ka)
    put(ka_ref, ka)
    put(qr_ref, rotate(seg(3)))
    put(kr_ref, rotate(seg(4)) * (HEAD_W ** -0.5))
    put(sg_ref, _silu(seg(6)))
    va = seg(2)
    for hh in range(H_A):
        vf_ref[:, pl.ds(hh, r, stride=H_A), :] = va[:, hh * HEAD_W:(hh + 1) * HEAD_W].reshape(g, r, HEAD_W)
    put(va_ref, va)
    put(vr_ref, seg(5))


def _inproj(x, mod, g1, w_in, gq_t, gk_t, tables, g_blk, r_blk):
    nb, l, d = x.shape
    assert nb % g_blk == 0 and l % r_blk == 0
    nl = l // r_blk
    tok = lambda w: pl.BlockSpec((g_blk, r_blk, w), lambda i: (i // nl, i % nl, 0))
    base = pl.BlockSpec((1, 8, HEAD_W), lambda i: (i % nl, 0, 0))
    tab = _resident((r_blk, HEAD_W))
    out = lambda dt: jax.ShapeDtypeStruct((nb, l, SEG_W), dt)
    return pl.pallas_call(
        _inproj_kernel,
        grid=((nb // g_blk) * nl,),
        in_specs=[tok(d),
                  pl.BlockSpec((g_blk, 6, d), lambda i: (i // nl, 0, 0)),
                  _resident((1, d)), _resident(w_in.shape), _resident((1, SEG_W)), _resident((1, SEG_W)),
                  base, tab, tab],
        out_specs=[tok(SEG_W), pl.BlockSpec((g_blk, r_blk * H_A, HEAD_W), lambda i: (i // nl, i % nl, 0))]
        + [tok(SEG_W)] * 7,
        out_shape=[out(F32), jax.ShapeDtypeStruct((nb, l * H_A, HEAD_W), F32)] + [out(BF16)] * 7,
        compiler_params=_params(("parallel",)),
        name="inproj",
    )(x, mod, g1, w_in, gq_t, gk_t, *tables)


def _stack_maps(q):
    lane = lax.broadcasted_iota(jnp.int32, q.shape, 1)
    zero = jnp.zeros_like(q)
    return jnp.concatenate([jnp.where(lane < DK_A, q, zero), jnp.where(lane >= DK_A, q, zero)], axis=0)


def _lambda(lq1, lk1, lq2, lk2, lam_init):
    s1 = jnp.sum(lq1[...] * lk1[...], axis=-1, keepdims=True)
    s2 = jnp.sum(lq2[...] * lk2[...], axis=-1, keepdims=True)
    return jnp.exp(s1) - jnp.exp(s2) + lam_init


def _diff_finish(acc, l, lam, gs, lam_init):
    t = acc.shape[0] // 2
    o = acc[:t] * (1.0 / l[:t]) - lam * (acc[t:] * (1.0 / l[t:]))
    on = o * lax.rsqrt(jnp.mean(o * o, axis=-1, keepdims=True) + EPS)
    return on * gs * (1.0 - lam_init)


def _attn_prompt_kernel(bound_ref, q_ref, k_ref, v_ref, tbl_ref, lq1, lk1, lq2, lk2, gs_ref, o_ref,
                        qq_sc, m_sc, l_sc, acc_sc, *, lam_init, tk, tkd, near):
    qi = pl.program_id(2)
    tq = q_ref.shape[1]
    q = q_ref[0]
    lane = lax.broadcasted_iota(jnp.int32, q.shape, 1)
    zero = jnp.zeros_like(q)
    qq_sc[0] = jnp.where(lane < DK_A, q, zero)
    qq_sc[1] = jnp.where(lane >= DK_A, q, zero)

    def tile(start, width, row0, corner, diag, bounded, first=False):
        nr = tq - row0
        nblk = width // HEAD_W
        rows = slice(row0, tq)
        flat = lambda x: x.reshape(2 * nr, x.shape[-1])
        kt = k_ref[0, pl.ds(start, width), :]
        vt = v_ref[0, pl.ds(start, width), :]
        s = _dot_nt(flat(qq_sc[:, rows, :]), kt)
        blocks = [s[:, c * HEAD_W:(c + 1) * HEAD_W] for c in range(nblk)]

        def add_rows(blk, r0, val):
            r1 = r0 + val.shape[0]
            cuts = [(0, r0, False), (r0, r1, True), (r1, nr + r0, False), (nr + r0, nr + r1, True),
                    (nr + r1, 2 * nr, False)]
            return jnp.concatenate([blk[a:b] + val if hit else blk[a:b] for a, b, hit in cuts if b > a], axis=0)

        if diag:
            blocks = [add_rows(blk, 0, tbl_ref[0, :, near + c * HEAD_W:near + (c + 1) * HEAD_W])
                      for c, blk in enumerate(blocks)]
        corner_row0 = width if diag else 0
        if corner_row0 < nr:
            blocks[-1] = add_rows(blocks[-1], corner_row0, corner)
        if bounded:
            ps = [jnp.exp2(blk - bound_ref[0]) for blk in blocks]
            l_new = functools.reduce(jnp.add, ps)
            p = jnp.concatenate([pc.astype(BF16) for pc in ps], axis=1)
            acc_new = _dot(p, vt)
            if not first:
                l_new = flat(l_sc[:, rows, :]) + l_new
                acc_new = flat(acc_sc[:, rows, :]) + acc_new
            l_sc[:, rows, :] = l_new.reshape(2, nr, HEAD_W)
            acc_sc[:, rows, :] = acc_new.reshape(2, nr, HEAD_W)
            return
        m_old = flat(m_sc[:, rows, :])
        mx = functools.reduce(jnp.maximum, blocks)
        m_new = jnp.maximum(m_old, jnp.max(mx, axis=-1, keepdims=True))
        alpha = jnp.exp2(m_old - m_new)
        ps = [jnp.exp2(blk - m_new) for blk in blocks]
        l_new = alpha * flat(l_sc[:, rows, :]) + functools.reduce(jnp.add, ps)
        p = jnp.concatenate([pc.astype(BF16) for pc in ps], axis=1)
        acc_new = alpha * flat(acc_sc[:, rows, :]) + _dot(p, vt)
        m_sc[:, rows, :] = m_new.reshape(2, nr, HEAD_W)
        l_sc[:, rows, :] = l_new.reshape(2, nr, HEAD_W)
        acc_sc[:, rows, :] = acc_new.reshape(2, nr, HEAD_W)

    n_far = qi * (tq // tk)
    corner_tbl = tbl_ref[0, :near, :near]

    def diag_tiles(bounded, first):
        for d in range(tq // tkd):
            tile(pl.multiple_of(qi * tq + d * tkd, tkd), tkd, d * tkd, corner_tbl, True, bounded, first and d == 0)

    def run(bounded):
        if bounded:
            diag_tiles(True, True)

        def far_tile(j):
            flag = (j == n_far - 1).astype(F32)
            tile(pl.multiple_of(j * tk, tk), tk, 0, corner_tbl * flag, False, bounded)

        odd = jnp.bitwise_and(n_far, 1)

        @pl.when(odd == 1)
        def _():
            far_tile(0)

        def far_pair(i, carry):
            far_tile(odd + 2 * i)
            far_tile(odd + 2 * i + 1)
            return carry

        lax.fori_loop(0, lax.shift_right_logical(n_far, 1), far_pair, 0)
        if not bounded:
            diag_tiles(False, False)

    @pl.when(bound_ref[1] > 0.5)
    def _():
        run(True)

    @pl.when(bound_ref[1] <= 0.5)
    def _():
        m_sc[...] = jnp.full(m_sc.shape, NEG_INF, F32)
        l_sc[...] = jnp.zeros(l_sc.shape, F32)
        acc_sc[...] = jnp.zeros(acc_sc.shape, F32)
        run(False)

    lam = _lambda(lq1, lk1, lq2, lk2, lam_init)
    l = jnp.sum(l_sc[...], axis=-1, keepdims=True).reshape(2 * tq, 1)
    acc = acc_sc[...].reshape(2 * tq, HEAD_W)
    o_ref[0] = _diff_finish(acc, l, lam, gs_ref[...], lam_init).astype(o_ref.dtype)


def _attn_prompt(bound, q, k, v, tbl, lams, gs, lam_init, tq, tk, tkd, near):
    b, s, _ = q.shape
    assert s % tq == 0 and tq % tk == 0 and tq % tkd == 0 and tbl.shape == (H_A, tkd, near + tkd)
    assert near == HEAD_W and tk % HEAD_W == 0 and tkd % HEAD_W == 0
    vec = _resident((1, DK_A))
    return pl.pallas_call(
        functools.partial(_attn_prompt_kernel, lam_init=lam_init, tk=tk, tkd=tkd, near=near),
        grid=(b, H_A, s // tq),
        in_specs=[pl.BlockSpec(memory_space=pltpu.SMEM),
                  pl.BlockSpec((1, tq, HEAD_W), lambda bi, h, qi: (bi, qi, h)),
                  pl.BlockSpec((1, s, HEAD_W), lambda bi, h, qi: (bi, 0, h)),
                  pl.BlockSpec((1, s, HEAD_W), lambda bi, h, qi: (bi, 0, h)),
                  pl.BlockSpec((1, tkd, near + tkd), lambda bi, h, qi: (h, 0, 0)),
                  vec, vec, vec, vec, _resident((1, HEAD_W))],
        out_specs=pl.BlockSpec((1, tq, HEAD_W), lambda bi, h, qi: (bi, qi, h)),
        out_shape=jax.ShapeDtypeStruct((b, s, SEG_W), BF16),
        scratch_shapes=[pltpu.VMEM((2, tq, HEAD_W), BF16)] + [pltpu.VMEM((2, tq, HEAD_W), F32)] * 3,
        compiler_params=_params(("parallel", "parallel", "arbitrary")),
        name="attn_prompt",
    )(bound, q, k, v, tbl, *lams, gs)


def _log_gamma(h):
    vals = [math.log(1.0 - 2.0 ** (-5.0 - i)) for i in range(H_R)]
    lg = jnp.float32(vals[H_R - 1])
    for i in range(H_R - 2, -1, -1):
        lg = jnp.where(h == i, jnp.float32(vals[i]), lg)
    return lg


def _retention_chunk(q, k, v, state, lg):
    c = q.shape[0]
    ti = lax.broadcasted_iota(jnp.int32, (c, c), 0)
    si = lax.broadcasted_iota(jnp.int32, (c, c), 1)
    dist = (ti - si).astype(F32)
    decay = jnp.where(dist >= 0, jnp.exp(jnp.maximum(dist, 0.0) * lg), 0.0)
    scores = _dot_nt(q, k) * decay
    row = lax.broadcasted_iota(jnp.int32, (c, 1), 0).astype(F32)
    q_dec = (q.astype(F32) * jnp.exp((row + 1.0) * lg)).astype(BF16)
    o = _dot(scores.astype(BF16), v) + _dot(q_dec, state.astype(BF16))
    k_dec = (k.astype(F32) * jnp.exp((c - 1.0 - row) * lg)).astype(BF16)
    new_state = jnp.exp(c * lg) * state + _dot_tn(k_dec, v)
    return o, new_state


def _retention_out(o, sg):
    on = o * lax.rsqrt(jnp.mean(o * o, axis=-1, keepdims=True) + EPS)
    return on * sg.astype(F32)


def _ret_prompt_kernel(q_ref, k_ref, v_ref, sg_ref, o_ref, st_ref, s_sc, d_sc, qd_sc, kd_sc, *, c):
    h = pl.program_id(1)
    ti = pl.program_id(2)
    tb = q_ref.shape[1]

    @pl.when(ti == 0)
    def _():
        lg = _log_gamma(h)
        s_sc[...] = jnp.zeros(s_sc.shape, F32)
        tt = lax.broadcasted_iota(jnp.int32, (c, c), 0)
        ss = lax.broadcasted_iota(jnp.int32, (c, c), 1)
        dist = (tt - ss).astype(F32)
        d_sc[...] = jnp.where(dist >= 0, jnp.exp(jnp.maximum(dist, 0.0) * lg), 0.0)
        row = lax.broadcasted_iota(jnp.int32, (c, HEAD_W), 0).astype(F32)
        qd_sc[...] = jnp.exp((row + 1.0) * lg)
        kd_sc[...] = jnp.exp((c - 1.0 - row) * lg)

    state = s_sc[...]
    g_c = qd_sc[c - 1:c, :]
    for i in range(tb // c):
        rows = slice(i * c, (i + 1) * c)
        q, k, v = q_ref[0, rows, :], k_ref[0, rows, :], v_ref[0, rows, :]
        scores = _dot_nt(q, k) * d_sc[...]
        q_dec = (q.astype(F32) * qd_sc[...]).astype(BF16)
        o = _dot(scores.astype(BF16), v) + _dot(q_dec, state.astype(BF16))
        k_dec = (k.astype(F32) * kd_sc[...]).astype(BF16)
        state = g_c * state + _dot_tn(k_dec, v)
        o_ref[0, rows, :] = _retention_out(o, sg_ref[0, rows, :]).astype(o_ref.dtype)
    s_sc[...] = state

    @pl.when(ti == pl.num_programs(2) - 1)
    def _():
        st_ref[0, 0] = state


def _ret_prompt(qr, kr, vr, sg, t_blk, c):
    b, s, _ = qr.shape
    assert s % t_blk == 0 and t_blk % c == 0
    blk = pl.BlockSpec((1, t_blk, HEAD_W), lambda bi, h, ti: (bi, ti, h))
    return pl.pallas_call(
        functools.partial(_ret_prompt_kernel, c=c),
        grid=(b, H_R, s // t_blk),
        in_specs=[blk, blk, blk, blk],
        out_specs=[blk, pl.BlockSpec((1, 1, HEAD_W, HEAD_W), lambda bi, h, ti: (bi, h, 0, 0))],
        out_shape=[jax.ShapeDtypeStruct((b, s, SEG_W), BF16),
                   jax.ShapeDtypeStruct((b, H_R, HEAD_W, HEAD_W), F32)],
        scratch_shapes=[pltpu.VMEM((HEAD_W, HEAD_W), F32), pltpu.VMEM((c, c), F32),
                        pltpu.VMEM((c, HEAD_W), F32), pltpu.VMEM((c, HEAD_W), F32)],
        compiler_params=_params(("parallel", "parallel", "arbitrary")),
        name="ret_prompt",
    )(qr, kr, vr, sg)


def _mix_sample_kernel(qa_ref, ka_ref, va_ref, ck_ref, cv_ref, tbl_ref, lq1, lk1, lq2, lk2, gs_ref,
                       qr_ref, kr_ref, vr_ref, sg_ref, st_ref, oa_ref, or_ref, so_ref, *, lam_init, near):
    p_len = ck_ref.shape[2]
    l = qa_ref.shape[1]
    lam = _lambda(lq1, lk1, lq2, lk2, lam_init)
    for h in range(H_A):
        cols = slice(h * HEAD_W, (h + 1) * HEAD_W)
        qq = _stack_maps(qa_ref[0, :, cols])
        kc = ck_ref[0, cols, :].astype(BF16)
        vc = cv_ref[0, pl.ds(h, p_len, stride=H_A), :].astype(BF16)
        bias = tbl_ref[h]
        bias2 = jnp.concatenate([bias, bias], axis=0)
        s_c = _dot(qq, kc)
        s_c = jnp.concatenate([s_c[:, :p_len - near], s_c[:, p_len - near:] + bias2[:, :near]], axis=1)
        s_n = _dot_nt(qq, ka_ref[0, :, cols]) + bias2[:, near:]
        m = jnp.maximum(jnp.max(s_c, axis=-1, keepdims=True), jnp.max(s_n, axis=-1, keepdims=True))
        p_c = jnp.exp2(s_c - m)
        p_n = jnp.exp2(s_n - m)
        lsum = jnp.sum(p_c, axis=-1, keepdims=True) + jnp.sum(p_n, axis=-1, keepdims=True)
        acc = _dot(p_c.astype(BF16), vc) + _dot(p_n.astype(BF16), va_ref[0, :, cols])
        oa_ref[0, :, cols] = _diff_finish(acc, lsum, lam, gs_ref[...], lam_init).astype(oa_ref.dtype)
    for h in range(H_R):
        cols = slice(h * HEAD_W, (h + 1) * HEAD_W)
        lg = jnp.float32(math.log(1.0 - 2.0 ** (-5.0 - h)))
        o, new_state = _retention_chunk(qr_ref[0, :, cols], kr_ref[0, :, cols], vr_ref[0, :, cols],
                                        st_ref[0, h], lg)
        or_ref[0, :, cols] = _retention_out(o, sg_ref[0, :, cols]).astype(or_ref.dtype)
        so_ref[0, h] = new_state


def _mix_sample(qa, ka, va, cache_k, cache_v, tbl, lams, gs, qr, kr, vr, sg, state, lam_init, near):
    b, l, _ = qa.shape
    p_len = cache_k.shape[2]
    assert p_len % CHUNK == 0 and l <= CHUNK and p_len >= near
    tok = pl.BlockSpec((1, l, SEG_W), lambda bi: (bi, 0, 0))
    cache_kt = pl.BlockSpec((1, SEG_W, p_len), lambda bi: (bi, 0, 0))
    cache_vh = pl.BlockSpec((1, p_len * H_A, HEAD_W), lambda bi: (bi, 0, 0))
    st = pl.BlockSpec((1, H_R, HEAD_W, HEAD_W), lambda bi: (bi, 0, 0, 0))
    vec = _resident((1, DK_A))
    return pl.pallas_call(
        functools.partial(_mix_sample_kernel, lam_init=lam_init, near=near),
        grid=(b,),
        in_specs=[tok, tok, tok, cache_kt, cache_vh, _resident(tbl.shape), vec, vec, vec, vec,
                  _resident((1, HEAD_W)), tok, tok, tok, tok, st],
        out_specs=[tok, tok, st],
        out_shape=[jax.ShapeDtypeStruct((b, l, SEG_W), BF16), jax.ShapeDtypeStruct((b, l, SEG_W), BF16),
                   jax.ShapeDtypeStruct((b, H_R, HEAD_W, HEAD_W), F32)],
        compiler_params=_params(("parallel",)),
        name="mix_sample",
    )(qa, ka, va, cache_k, cache_v, tbl, *lams, gs, qr, kr, vr, sg, state)


def _ffn_kernel(x_ref, oa_ref, or_ref, mod_ref, g2_ref, wo_ref, wg_ref, wu_ref, wd_ref, y_ref, *, ff_blk):
    g, r, d = x_ref.shape
    m = g * r
    f = wg_ref.shape[1]
    proj = (_dot(oa_ref[...].reshape(m, SEG_W), wo_ref[:SEG_W, :])
            + _dot(or_ref[...].reshape(m, SEG_W), wo_ref[SEG_W:, :]))
    x1 = x_ref[...] + mod_ref[:, 2:3, :] * proj.reshape(g, r, d)
    gain = g2_ref[...] * (1.0 + mod_ref[:, 4:5, :])
    y = x1 * lax.rsqrt(jnp.mean(x1 * x1, axis=-1, keepdims=True) + EPS) * gain
    h2 = (y + mod_ref[:, 3:4, :]).astype(BF16).reshape(m, d)
    acc = jnp.zeros((m, d), F32)
    for c0 in range(0, f, ff_blk):
        c1 = min(c0 + ff_blk, f)
        ff = (_silu(_dot(h2, wg_ref[:, c0:c1])) * _dot(h2, wu_ref[:, c0:c1])).astype(BF16)
        acc = acc + _dot(ff, wd_ref[c0:c1, :])
    y_ref[...] = x1 + mod_ref[:, 5:6, :] * acc.reshape(g, r, d)


def _ffn(x, oa, orr, mod, g2, wo, wg, wu, wd, g_blk, r_blk, ff_blk):
    nb, l, d = x.shape
    assert nb % g_blk == 0 and l % r_blk == 0
    nl = l // r_blk
    tok = lambda w: pl.BlockSpec((g_blk, r_blk, w), lambda i: (i // nl, i % nl, 0))
    return pl.pallas_call(
        functools.partial(_ffn_kernel, ff_blk=ff_blk),
        grid=((nb // g_blk) * nl,),
        in_specs=[tok(d), tok(SEG_W), tok(SEG_W),
                  pl.BlockSpec((g_blk, 6, d), lambda i: (i // nl, 0, 0)),
                  _resident((1, d)), _resident(wo.shape), _resident(wg.shape), _resident(wu.shape),
                  _resident(wd.shape)],
        out_specs=tok(d),
        out_shape=jax.ShapeDtypeStruct((nb, l, d), F32),
        compiler_params=_params(("parallel",)),
        name="outproj_ffn",
    )(x, oa, orr, mod, g2, wo, wg, wu, wd)


def _rope_tables(first_pos, n_blocks, blk):
    inv = 1.0 / (ROPE_BASE ** jnp.linspace(0.0, 1.0, HEAD_W // 2, dtype=F32))
    inv2 = jnp.repeat(inv, 2)[None, :]
    even = (jnp.arange(HEAD_W) % 2 == 0)[None, :]
    me, mo = jnp.where(even, -1.0, 0.0), jnp.where(even, 0.0, 1.0)
    t0 = (first_pos + blk * jnp.arange(n_blocks, dtype=jnp.int32)).astype(F32)[:, None] * inv2
    cb, sb = jnp.cos(t0), jnp.sin(t0)
    zero = jnp.zeros_like(cb)
    base = jnp.stack([cb, sb, cb * me, sb * me, cb * mo, sb * mo, zero, zero], axis=1)
    off = jnp.arange(blk, dtype=jnp.int32).astype(F32)[:, None] * inv2
    return base, jnp.cos(off), jnp.sin(off)


def _logit_bound(g_q, g_k, rel_bias):
    slack = 1.02
    dot_max = slack * DK_A * (DK_A ** -0.5 * LOG2E) * jnp.max(jnp.abs(g_q)) * jnp.max(jnp.abs(g_k))
    rb = (rel_bias - rel_bias[FAR_BUCKET][None, :]) * LOG2E
    b_hi = jnp.maximum(jnp.max(rb), 0.0)
    b_lo = jnp.minimum(jnp.min(rb), 0.0)
    ok = (2.0 * dot_max + (b_hi - b_lo)) <= MAX_LOGIT_SPREAD
    return jnp.stack([dot_max + b_hi, ok.astype(F32)])


def _tile_rows(n, target):
    t = min(n, target)
    while n % t:
        t //= 2
    return t


def kernel(x_prompt, x_sample, c_prompt, c_sample, cache_k, cache_v, state_ret, w_ada, b_ada, g_norm1, g_norm2, w_in, g_q, g_k, lam_q1, lam_k1, lam_q2, lam_k2, g_subln, w_out, w_ff_gate, w_ff_up, w_ff_down, rel_bias):
    depth = w_ada.shape[0]
    b, s, d = x_prompt.shape
    bs, l, _ = x_sample.shape
    p_len = cache_k.shape[2]
    assert w_in.shape[2] == N_SEG * SEG_W and d == 2 * SEG_W

    tm = _tile_rows(s, 512)
    tm_in = _tile_rows(s, 1024)
    tq = _tile_rows(s, 2048)
    tk = _tile_rows(tq, 512)
    tkd = _tile_rows(tq, 256)
    ret_blk = _tile_rows(s, 8192)
    ret_c = _tile_rows(ret_blk, 256)
    near = HEAD_W
    assert tq % CHUNK == 0 and tk % CHUNK == 0 and tkd % CHUNK == 0 and near >= BUCKET_THRESHOLDS[-1]

    tab_p = _rope_tables(0, s // tm_in, tm_in)
    tab_s = _rope_tables(p_len, 1, l)
    tbl_p = _bias_table(rel_bias, tkd, near + tkd, near)
    tbl_s = _bias_table(rel_bias, l, near + l, near)
    c_all = jnp.concatenate([c_prompt, c_sample], axis=0)

    xp, xs = x_prompt, x_sample
    outs = [[] for _ in range(6)]
    for layer in range(depth):
        lam_init = _lambda_init(layer)
        mod = _modulation(c_all, w_ada[layer], b_ada[layer]).reshape(b + bs, 6, d)
        mod_p, mod_s = mod[:b], mod[b:]
        g1 = g_norm1[layer].reshape(1, d)
        g2 = g_norm2[layer].reshape(1, d)
        gq_t = jnp.tile(g_q[layer], SEG_W // DK_A).reshape(1, SEG_W)
        gk_t = jnp.tile(g_k[layer], SEG_W // DK_A).reshape(1, SEG_W)
        lams = [v[layer].reshape(1, DK_A) for v in (lam_q1, lam_k1, lam_q2, lam_k2)]
        gs = g_subln[layer].reshape(1, HEAD_W)
        wi = w_in[layer].astype(BF16)
        wo = w_out[layer].astype(BF16)
        wg = w_ff_gate[layer].astype(BF16)
        wu = w_ff_up[layer].astype(BF16)
        wd = w_ff_down[layer].astype(BF16)
        ff_blk = 1024

        kf, vf, qa, ka, va, qr, kr, vr, sg = _inproj(xp, mod_p, g1, wi, gq_t, gk_t, tab_p, 1, tm_in)
        bound = _logit_bound(g_q[layer], g_k[layer], rel_bias)
        oa = _attn_prompt(bound, qa, ka, va, tbl_p, lams, gs, lam_init, tq, tk, tkd, near)
        orr, st_p = _ret_prompt(qr, kr, vr, sg, ret_blk, ret_c)
        xp = _ffn(xp, oa, orr, mod_p, g2, wo, wg, wu, wd, 1, tm, ff_blk)

        kfs, vfs, qas, kas, vas, qrs, krs, vrs, sgs = _inproj(xs, mod_s, g1, wi, gq_t, gk_t, tab_s, bs, l)
        ck = jnp.swapaxes(cache_k[layer].reshape(bs, p_len, SEG_W), 1, 2)
        cv = cache_v[layer].reshape(bs, p_len * H_A, HEAD_W)
        oas, ors, st_s = _mix_sample(qas, kas, vas, ck, cv, tbl_s, lams, gs, qrs, krs, vrs, sgs,
                                     state_ret[layer], lam_init, near)
        xs = _ffn(xs, oas, ors, mod_s, g2, wo, wg, wu, wd, bs, l, ff_blk)

        for lst, val in zip(outs, (kf.reshape(b, s, H_A, 2, DK_A), vf.reshape(b, s, H_A, HEAD_W), st_p,
                                   kfs.reshape(bs, l, H_A, 2, DK_A), vfs.reshape(bs, l, H_A, HEAD_W), st_s)):
            lst.append(val)
    return (xp, xs) + tuple(jnp.stack(o) for o in outs)
```

```python
import functools
import math

import jax
import jax.numpy as jnp
from jax import lax
from jax.experimental import pallas as pl
from jax.experimental.pallas import tpu as pltpu

F32 = jnp.float32
BF16 = jnp.bfloat16

CHUNK = 64
H_A = 4
DK_A = 64
H_R = 4
HEAD_W = 128
SEG_W = H_A * HEAD_W
N_SEG = 7
N_BUCKETS = 32
ROPE_BASE = 10000.0
EPS = 1e-6
NEG_INF = -1e30
LOG2E = math.log2(math.e)
BUCKET_THRESHOLDS = (12, 16, 23, 32, 46, 64, 91)
FAR_BUCKET = 15
MAX_LOGIT_SPREAD = 100.0

MXU_W = 256
VMEM_LIMIT = 56 * 1024 * 1024


def _lambda_init(layer):
    return 0.8 - 0.6 * math.exp(-0.3 * layer)


def _params(sem):
    return pltpu.CompilerParams(dimension_semantics=sem, vmem_limit_bytes=VMEM_LIMIT)


def _resident(shape):
    return pl.BlockSpec(shape, lambda *_: (0,) * len(shape), pipeline_mode=pl.Buffered(1))


def _dot(a, b):
    return jnp.dot(a, b, preferred_element_type=F32)


def _dot_nt(a, b):
    return lax.dot_general(a, b, (((1,), (1,)), ((), ())), preferred_element_type=F32)


def _dot_tn(a, b):
    return lax.dot_general(a, b, (((0,), (0,)), ((), ())), preferred_element_type=F32)


def _silu(x):
    return x * jax.nn.sigmoid(x)


def _mod_kernel(c_ref, w_ref, b_ref, o_ref):
    s = _silu(c_ref[...]).astype(BF16)
    o_ref[...] = _dot(s, w_ref[...].astype(BF16)) + b_ref[...]


def _modulation(c_all, w_ada, b_ada):
    n, d = c_all.shape
    e = w_ada.shape[1]
    te = 1536
    assert e % te == 0
    return pl.pallas_call(
        _mod_kernel,
        grid=(e // te,),
        in_specs=[pl.BlockSpec((n, d), lambda j: (0, 0)),
                  pl.BlockSpec((d, te), lambda j: (0, j)),
                  pl.BlockSpec((1, te), lambda j: (0, j))],
        out_specs=pl.BlockSpec((n, te), lambda j: (0, j)),
        out_shape=jax.ShapeDtypeStruct((n, e), F32),
        compiler_params=_params(("parallel",)),
        name="modulation",
    )(c_all, w_ada, b_ada.reshape(1, e))


def _bias_table_kernel(rb_ref, o_ref, *, offset):
    h = pl.program_id(0)
    _, rows, cols = o_ref.shape
    r = lax.broadcasted_iota(jnp.int32, (rows, cols), 0)
    kc = lax.broadcasted_iota(jnp.int32, (rows, cols), 1) - offset
    rel = kc - r
    n = jnp.abs(rel)
    large = jnp.full((rows, cols), N_BUCKETS // 4, jnp.int32)
    for thr in BUCKET_THRESHOLDS:
        large = large + (n >= thr).astype(jnp.int32)
    bucket = jnp.where(rel > 0, N_BUCKETS // 2, 0) + jnp.where(n < N_BUCKETS // 4, n, large)
    val = jnp.zeros((rows, cols), F32)
    for b in range(N_BUCKETS):
        val = jnp.where(bucket == b, rb_ref[b, h], val)
    val = (val - rb_ref[FAR_BUCKET, h]) * LOG2E
    shift = int(math.log2(CHUNK))
    visible = lax.shift_right_arithmetic(kc, shift) <= lax.shift_right_arithmetic(r, shift)
    o_ref[0] = jnp.where(visible, val, NEG_INF)


def _bias_table(rel_bias, rows, cols, offset):
    assert offset >= BUCKET_THRESHOLDS[-1] and offset % CHUNK == 0
    return pl.pallas_call(
        functools.partial(_bias_table_kernel, offset=offset),
        grid=(H_A,),
        in_specs=[pl.BlockSpec(memory_space=pltpu.SMEM)],
        out_specs=pl.BlockSpec((1, rows, cols), lambda h: (h, 0, 0)),
        out_shape=jax.ShapeDtypeStruct((H_A, rows, cols), F32),
        compiler_params=_params(("parallel",)),
        name="bias_table",
    )(rel_bias)


def _inproj_kernel(x_ref, mod_ref, g1_ref, w_ref, gq_ref, gk_ref, rope_ref, cr_ref, sr_ref,
                   kf_ref, vf_ref, qa_ref, ka_ref, va_ref, qr_ref, kr_ref, vr_ref, sg_ref):
    g, r, d = x_ref.shape
    m = g * r
    x = x_ref[...]
    gain = g1_ref[...] * (1.0 + mod_ref[:, 1:2, :])
    hmod = x * lax.rsqrt(jnp.mean(x * x, axis=-1, keepdims=True) + EPS) * gain + mod_ref[:, 0:1, :]
    hb = hmod.astype(BF16).reshape(m, d)

    def seg(j):
        return _dot(hb, w_ref[:, j * SEG_W:(j + 1) * SEG_W])

    def put(ref, val):
        ref[...] = val.reshape(g, r, SEG_W).astype(ref.dtype)

    gi = lax.broadcasted_iota(jnp.int32, (MXU_W, MXU_W), 0) // DK_A
    gj = lax.broadcasted_iota(jnp.int32, (MXU_W, MXU_W), 1) // DK_A
    ones_bd = (gi == gj).astype(BF16)

    def group_rms(z):
        sq = (z * z).astype(BF16)
        ms = jnp.concatenate([_dot(sq[:, c:c + MXU_W], ones_bd) for c in range(0, SEG_W, MXU_W)], axis=1)
        return z * lax.rsqrt(ms * (1.0 / DK_A) + EPS)

    cr, sr = cr_ref[...], sr_ref[...]

    def table(c_row, s_row, sign):
        t = rope_ref[0, c_row:c_row + 1, :] * cr + sign * (rope_ref[0, s_row:s_row + 1, :] * sr)
        return jnp.broadcast_to(t[None], (g, r, HEAD_W)).reshape(m, HEAD_W)

    cos, sine, sino = table(0, 1, -1.0), table(3, 2, 1.0), table(5, 4, 1.0)

    def rotate(z):
        outs = []
        for hh in range(H_R):
            zh = z[:, hh * HEAD_W:(hh + 1) * HEAD_W]
            nxt = pltpu.roll(zh, HEAD_W - 1, axis=1)
            prv = pltpu.roll(zh, 1, axis=1)
            outs.append(zh * cos + nxt * sine + prv * sino)
        return jnp.concatenate(outs, axis=1)

    put(qa_ref, group_rms(seg(0)) * gq_ref[...] * (DK_A ** -0.5 * LOG2E))
    ka = group_rms(seg(1)) * gk_ref[...]
    put(kf_ref, ka)
    put(ka_ref, ka)
    put(qr_ref, rotate(seg(3)))
    put(kr_ref, rotate(seg(4)) * (HEAD_W ** -0.5))
    put(sg_ref, _silu(seg(6)))
    va = seg(2)
    for hh in range(H_A):
        vf_ref[:, pl.ds(hh, r, stride=H_A), :] = va[:, hh * HEAD_W:(hh + 1) * HEAD_W].reshape(g, r, HEAD_W)
    put(va_ref, va)
    put(vr_ref, seg(5))


def _inproj(x, mod, g1, w_in, gq_t, gk_t, tables, g_blk, r_blk):
    nb, l, d = x.shape
    assert nb % g_blk == 0 and l % r_blk == 0
    nl = l // r_blk
    tok = lambda w: pl.BlockSpec((g_blk, r_blk, w), lambda i: (i // nl, i % nl, 0))
    base = pl.BlockSpec((1, 8, HEAD_W), lambda i: (i % nl, 0, 0))
    tab = _resident((r_blk, HEAD_W))
    out = lambda dt: jax.ShapeDtypeStruct((nb, l, SEG_W), dt)
    return pl.pallas_call(
        _inproj_kernel,
        grid=((nb // g_blk) * nl,),
        in_specs=[tok(d),
                  pl.BlockSpec((g_blk, 6, d), lambda i: (i // nl, 0, 0)),
                  _resident((1, d)), _resident(w_in.shape), _resident((1, SEG_W)), _resident((1, SEG_W)),
                  base, tab, tab],
        out_specs=[tok(SEG_W), pl.BlockSpec((g_blk, r_blk * H_A, HEAD_W), lambda i: (i // nl, i % nl, 0))]
        + [tok(SEG_W)] * 7,
        out_shape=[out(F32), jax.ShapeDtypeStruct((nb, l * H_A, HEAD_W), F32)] + [out(BF16)] * 7,
        compiler_params=_params(("parallel",)),
        name="inproj",
    )(x, mod, g1, w_in, gq_t, gk_t, *tables)


def _stack_maps(q):
    lane = lax.broadcasted_iota(jnp.int32, q.shape, 1)
    zero = jnp.zeros_like(q)
    return jnp.concatenate([jnp.where(lane < DK_A, q, zero), jnp.where(lane >= DK_A, q, zero)], axis=0)


def _lambda(lq1, lk1, lq2, lk2, lam_init):
    s1 = jnp.sum(lq1[...] * lk1[...], axis=-1, keepdims=True)
    s2 = jnp.sum(lq2[...] * lk2[...], axis=-1, keepdims=True)
    return jnp.exp(s1) - jnp.exp(s2) + lam_init


def _diff_finish(acc, l, lam, gs, lam_init):
    t = acc.shape[0] // 2
    o = acc[:t] * (1.0 / l[:t]) - lam * (acc[t:] * (1.0 / l[t:]))
    on = o * lax.rsqrt(jnp.mean(o * o, axis=-1, keepdims=True) + EPS)
    return on * gs * (1.0 - lam_init)


def _attn_prompt_kernel(bound_ref, q_ref, k_ref, v_ref, tbl_ref, lq1, lk1, lq2, lk2, gs_ref, o_ref,
                        qq_sc, m_sc, l_sc, acc_sc, *, lam_init, tk, tkd, near, bounded):
    qi = pl.program_id(2)
    tq = q_ref.shape[1]
    q = q_ref[0]
    lane = lax.broadcasted_iota(jnp.int32, q.shape, 1)
    zero = jnp.zeros_like(q)
    qq_sc[0] = jnp.where(lane < DK_A, q, zero)
    qq_sc[1] = jnp.where(lane >= DK_A, q, zero)

    def tile(start, width, row0, corner, diag, bounded, first=False):
        nr = tq - row0
        nblk = width // HEAD_W
        rows = slice(row0, tq)
        flat = lambda x: x.reshape(2 * nr, x.shape[-1])
        kt = k_ref[0, pl.ds(start, width), :]
        vt = v_ref[0, pl.ds(start, width), :]
        s = _dot_nt(flat(qq_sc[:, rows, :]), kt)
        blocks = [s[:, c * HEAD_W:(c + 1) * HEAD_W] for c in range(nblk)]

        def add_rows(blk, r0, val):
            r1 = r0 + val.shape[0]
            cuts = [(0, r0, False), (r0, r1, True), (r1, nr + r0, False), (nr + r0, nr + r1, True),
                    (nr + r1, 2 * nr, False)]
            return jnp.concatenate([blk[a:b] + val if hit else blk[a:b] for a, b, hit in cuts if b > a], axis=0)

        if diag:
            blocks = [add_rows(blk, 0, tbl_ref[0, :, near + c * HEAD_W:near + (c + 1) * HEAD_W])
                      for c, blk in enumerate(blocks)]
        corner_row0 = width if diag else 0
        if corner_row0 < nr:
            blocks[-1] = add_rows(blocks[-1], corner_row0, corner)
        if bounded:
            ps = [jnp.exp2(blk - bound_ref[0]) for blk in blocks]
            l_new = functools.reduce(jnp.add, ps)
            p = jnp.concatenate([pc.astype(BF16) for pc in ps], axis=1)
            acc_new = _dot(p, vt)
            if not first:
                l_new = flat(l_sc[:, rows, :]) + l_new
                acc_new = flat(acc_sc[:, rows, :]) + acc_new
            l_sc[:, rows, :] = l_new.reshape(2, nr, HEAD_W)
            acc_sc[:, rows, :] = acc_new.reshape(2, nr, HEAD_W)
            return
        m_old = flat(m_sc[:, rows, :])
        mx = functools.reduce(jnp.maximum, blocks)
        m_new = jnp.maximum(m_old, jnp.max(mx, axis=-1, keepdims=True))
        alpha = jnp.exp2(m_old - m_new)
        ps = [jnp.exp2(blk - m_new) for blk in blocks]
        l_new = alpha * flat(l_sc[:, rows, :]) + functools.reduce(jnp.add, ps)
        p = jnp.concatenate([pc.astype(BF16) for pc in ps], axis=1)
        acc_new = alpha * flat(acc_sc[:, rows, :]) + _dot(p, vt)
        m_sc[:, rows, :] = m_new.reshape(2, nr, HEAD_W)
        l_sc[:, rows, :] = l_new.reshape(2, nr, HEAD_W)
        acc_sc[:, rows, :] = acc_new.reshape(2, nr, HEAD_W)

    n_far = qi * (tq // tk)
    corner_tbl = tbl_ref[0, :near, :near]

    def diag_tiles(bounded, first):
        for d in range(tq // tkd):
            tile(pl.multiple_of(qi * tq + d * tkd, tkd), tkd, d * tkd, corner_tbl, True, bounded, first and d == 0)

    def run(bounded):
        if bounded:
            diag_tiles(True, True)

        def far_tile(j):
            flag = (j == n_far - 1).astype(F32)
            tile(pl.multiple_of(j * tk, tk), tk, 0, corner_tbl * flag, False, bounded)

        odd = jnp.bitwise_and(n_far, 1)

        @pl.when(odd == 1)
        def _():
            far_tile(0)

        def far_pair(i, carry):
            far_tile(odd + 2 * i)
            far_tile(odd + 2 * i + 1)
            return carry

        lax.fori_loop(0, lax.shift_right_logical(n_far, 1), far_pair, 0)
        if not bounded:
            diag_tiles(False, False)

    if not bounded:
        m_sc[...] = jnp.full(m_sc.shape, NEG_INF, F32)
        l_sc[...] = jnp.zeros(l_sc.shape, F32)
        acc_sc[...] = jnp.zeros(acc_sc.shape, F32)
    run(bounded)

    lam = _lambda(lq1, lk1, lq2, lk2, lam_init)
    l = jnp.sum(l_sc[...], axis=-1, keepdims=True).reshape(2 * tq, 1)
    acc = acc_sc[...].reshape(2 * tq, HEAD_W)
    o_ref[0] = _diff_finish(acc, l, lam, gs_ref[...], lam_init).astype(o_ref.dtype)


def _attn_prompt(bound, q, k, v, tbl, lams, gs, lam_init, tq, tk, tkd, near, bounded):
    b, s, _ = q.shape
    assert s % tq == 0 and tq % tk == 0 and tq % tkd == 0 and tbl.shape == (H_A, tkd, near + tkd)
    assert near == HEAD_W and tk % HEAD_W == 0 and tkd % HEAD_W == 0
    vec = _resident((1, DK_A))
    return pl.pallas_call(
        functools.partial(_attn_prompt_kernel, lam_init=lam_init, tk=tk, tkd=tkd, near=near, bounded=bounded),
        grid=(b, H_A, s // tq),
        in_specs=[pl.BlockSpec(memory_space=pltpu.SMEM),
                  pl.BlockSpec((1, tq, HEAD_W), lambda bi, h, qi: (bi, qi, h)),
                  pl.BlockSpec((1, s, HEAD_W), lambda bi, h, qi: (bi, 0, h)),
                  pl.BlockSpec((1, s, HEAD_W), lambda bi, h, qi: (bi, 0, h)),
                  pl.BlockSpec((1, tkd, near + tkd), lambda bi, h, qi: (h, 0, 0)),
                  vec, vec, vec, vec, _resident((1, HEAD_W))],
        out_specs=pl.BlockSpec((1, tq, HEAD_W), lambda bi, h, qi: (bi, qi, h)),
        out_shape=jax.ShapeDtypeStruct((b, s, SEG_W), BF16),
        scratch_shapes=[pltpu.VMEM((2, tq, HEAD_W), BF16)] + [pltpu.VMEM((2, tq, HEAD_W), F32)] * 3,
        compiler_params=_params(("parallel", "parallel", "arbitrary")),
        name="attn_prompt_bounded" if bounded else "attn_prompt_exact",
    )(bound, q, k, v, tbl, *lams, gs)


def _log_gamma(h):
    vals = [math.log(1.0 - 2.0 ** (-5.0 - i)) for i in range(H_R)]
    lg = jnp.float32(vals[H_R - 1])
    for i in range(H_R - 2, -1, -1):
        lg = jnp.where(h == i, jnp.float32(vals[i]), lg)
    return lg


def _retention_chunk(q, k, v, state, lg):
    c = q.shape[0]
    ti = lax.broadcasted_iota(jnp.int32, (c, c), 0)
    si = lax.broadcasted_iota(jnp.int32, (c, c), 1)
    dist = (ti - si).astype(F32)
    decay = jnp.where(dist >= 0, jnp.exp(jnp.maximum(dist, 0.0) * lg), 0.0)
    scores = _dot_nt(q, k) * decay
    row = lax.broadcasted_iota(jnp.int32, (c, 1), 0).astype(F32)
    q_dec = (q.astype(F32) * jnp.exp((row + 1.0) * lg)).astype(BF16)
    o = _dot(scores.astype(BF16), v) + _dot(q_dec, state.astype(BF16))
    k_dec = (k.astype(F32) * jnp.exp((c - 1.0 - row) * lg)).astype(BF16)
    new_state = jnp.exp(c * lg) * state + _dot_tn(k_dec, v)
    return o, new_state


def _retention_out(o, sg):
    on = o * lax.rsqrt(jnp.mean(o * o, axis=-1, keepdims=True) + EPS)
    return on * sg.astype(F32)


def _ret_prompt_kernel(q_ref, k_ref, v_ref, sg_ref, o_ref, st_ref, s_sc, d_sc, qd_sc, kd_sc, *, c):
    h = pl.program_id(1)
    ti = pl.program_id(2)
    tb = q_ref.shape[1]

    @pl.when(ti == 0)
    def _():
        lg = _log_gamma(h)
        s_sc[...] = jnp.zeros(s_sc.shape, F32)
        tt = lax.broadcasted_iota(jnp.int32, (c, c), 0)
        ss = lax.broadcasted_iota(jnp.int32, (c, c), 1)
        dist = (tt - ss).astype(F32)
        d_sc[...] = jnp.where(dist >= 0, jnp.exp(jnp.maximum(dist, 0.0) * lg), 0.0)
        row = lax.broadcasted_iota(jnp.int32, (c, HEAD_W), 0).astype(F32)
        qd_sc[...] = jnp.exp((row + 1.0) * lg)
        kd_sc[...] = jnp.exp((c - 1.0 - row) * lg)

    state = s_sc[...]
    g_c = qd_sc[c - 1:c, :]
    for i in range(tb // c):
        rows = slice(i * c, (i + 1) * c)
        q, k, v = q_ref[0, rows, :], k_ref[0, rows, :], v_ref[0, rows, :]
        scores = _dot_nt(q, k) * d_sc[...]
        q_dec = (q.astype(F32) * qd_sc[...]).astype(BF16)
        o = _dot(scores.astype(BF16), v) + _dot(q_dec, state.astype(BF16))
        k_dec = (k.astype(F32) * kd_sc[...]).astype(BF16)
        state = g_c * state + _dot_tn(k_dec, v)
        o_ref[0, rows, :] = _retention_out(o, sg_ref[0, rows, :]).astype(o_ref.dtype)
    s_sc[...] = state

    @pl.when(ti == pl.num_programs(2) - 1)
    def _():
        st_ref[0, 0] = state


def _ret_prompt(qr, kr, vr, sg, t_blk, c):
    b, s, _ = qr.shape
    assert s % t_blk == 0 and t_blk % c == 0
    blk = pl.BlockSpec((1, t_blk, HEAD_W), lambda bi, h, ti: (bi, ti, h))
    return pl.pallas_call(
        functools.partial(_ret_prompt_kernel, c=c),
        grid=(b, H_R, s // t_blk),
        in_specs=[blk, blk, blk, blk],
        out_specs=[blk, pl.BlockSpec((1, 1, HEAD_W, HEAD_W), lambda bi, h, ti: (bi, h, 0, 0))],
        out_shape=[jax.ShapeDtypeStruct((b, s, SEG_W), BF16),
                   jax.ShapeDtypeStruct((b, H_R, HEAD_W, HEAD_W), F32)],
        scratch_shapes=[pltpu.VMEM((HEAD_W, HEAD_W), F32), pltpu.VMEM((c, c), F32),
                        pltpu.VMEM((c, HEAD_W), F32), pltpu.VMEM((c, HEAD_W), F32)],
        compiler_params=_params(("parallel", "parallel", "arbitrary")),
        name="ret_prompt",
    )(qr, kr, vr, sg)


def _mix_sample_kernel(qa_ref, ka_ref, va_ref, ck_ref, cv_ref, tbl_ref, lq1, lk1, lq2, lk2, gs_ref,
                       qr_ref, kr_ref, vr_ref, sg_ref, st_ref, oa_ref, or_ref, so_ref, *, lam_init, near):
    p_len = ck_ref.shape[2]
    l = qa_ref.shape[1]
    lam = _lambda(lq1, lk1, lq2, lk2, lam_init)
    for h in range(H_A):
        cols = slice(h * HEAD_W, (h + 1) * HEAD_W)
        qq = _stack_maps(qa_ref[0, :, cols])
        kc = ck_ref[0, cols, :].astype(BF16)
        vc = cv_ref[0, pl.ds(h, p_len, stride=H_A), :].astype(BF16)
        bias = tbl_ref[h]
        bias2 = jnp.concatenate([bias, bias], axis=0)
        s_c = _dot(qq, kc)
        s_c = jnp.concatenate([s_c[:, :p_len - near], s_c[:, p_len - near:] + bias2[:, :near]], axis=1)
        s_n = _dot_nt(qq, ka_ref[0, :, cols]) + bias2[:, near:]
        m = jnp.maximum(jnp.max(s_c, axis=-1, keepdims=True), jnp.max(s_n, axis=-1, keepdims=True))
        p_c = jnp.exp2(s_c - m)
        p_n = jnp.exp2(s_n - m)
        lsum = jnp.sum(p_c, axis=-1, keepdims=True) + jnp.sum(p_n, axis=-1, keepdims=True)
        acc = _dot(p_c.astype(BF16), vc) + _dot(p_n.astype(BF16), va_ref[0, :, cols])
        oa_ref[0, :, cols] = _diff_finish(acc, lsum, lam, gs_ref[...], lam_init).astype(oa_ref.dtype)
    for h in range(H_R):
        cols = slice(h * HEAD_W, (h + 1) * HEAD_W)
        lg = jnp.float32(math.log(1.0 - 2.0 ** (-5.0 - h)))
        o, new_state = _retention_chunk(qr_ref[0, :, cols], kr_ref[0, :, cols], vr_ref[0, :, cols],
                                        st_ref[0, h], lg)
        or_ref[0, :, cols] = _retention_out(o, sg_ref[0, :, cols]).astype(or_ref.dtype)
        so_ref[0, h] = new_state


def _mix_sample(qa, ka, va, cache_k, cache_v, tbl, lams, gs, qr, kr, vr, sg, state, lam_init, near):
    b, l, _ = qa.shape
    p_len = cache_k.shape[2]
    assert p_len % CHUNK == 0 and l <= CHUNK and p_len >= near
    tok = pl.BlockSpec((1, l, SEG_W), lambda bi: (bi, 0, 0))
    cache_kt = pl.BlockSpec((1, SEG_W, p_len), lambda bi: (bi, 0, 0))
    cache_vh = pl.BlockSpec((1, p_len * H_A, HEAD_W), lambda bi: (bi, 0, 0))
    st = pl.BlockSpec((1, H_R, HEAD_W, HEAD_W), lambda bi: (bi, 0, 0, 0))
    vec = _resident((1, DK_A))
    return pl.pallas_call(
        functools.partial(_mix_sample_kernel, lam_init=lam_init, near=near),
        grid=(b,),
        in_specs=[tok, tok, tok, cache_kt, cache_vh, _resident(tbl.shape), vec, vec, vec, vec,
                  _resident((1, HEAD_W)), tok, tok, tok, tok, st],
        out_specs=[tok, tok, st],
        out_shape=[jax.ShapeDtypeStruct((b, l, SEG_W), BF16), jax.ShapeDtypeStruct((b, l, SEG_W), BF16),
                   jax.ShapeDtypeStruct((b, H_R, HEAD_W, HEAD_W), F32)],
        compiler_params=_params(("parallel",)),
        name="mix_sample",
    )(qa, ka, va, cache_k, cache_v, tbl, *lams, gs, qr, kr, vr, sg, state)


def _ffn_kernel(x_ref, oa_ref, or_ref, mod_ref, g2_ref, wo_ref, wg_ref, wu_ref, wd_ref, y_ref, *, ff_blk):
    g, r, d = x_ref.shape
    m = g * r
    f = wg_ref.shape[1]
    proj = (_dot(oa_ref[...].reshape(m, SEG_W), wo_ref[:SEG_W, :])
            + _dot(or_ref[...].reshape(m, SEG_W), wo_ref[SEG_W:, :]))
    x1 = x_ref[...] + mod_ref[:, 2:3, :] * proj.reshape(g, r, d)
    gain = g2_ref[...] * (1.0 + mod_ref[:, 4:5, :])
    y = x1 * lax.rsqrt(jnp.mean(x1 * x1, axis=-1, keepdims=True) + EPS) * gain
    h2 = (y + mod_ref[:, 3:4, :]).astype(BF16).reshape(m, d)
    acc = jnp.zeros((m, d), F32)
    for c0 in range(0, f, ff_blk):
        c1 = min(c0 + ff_blk, f)
        ff = (_silu(_dot(h2, wg_ref[:, c0:c1])) * _dot(h2, wu_ref[:, c0:c1])).astype(BF16)
        acc = acc + _dot(ff, wd_ref[c0:c1, :])
    y_ref[...] = x1 + mod_ref[:, 5:6, :] * acc.reshape(g, r, d)


def _ffn(x, oa, orr, mod, g2, wo, wg, wu, wd, g_blk, r_blk, ff_blk):
    nb, l, d = x.shape
    assert nb % g_blk == 0 and l % r_blk == 0
    nl = l // r_blk
    tok = lambda w: pl.BlockSpec((g_blk, r_blk, w), lambda i: (i // nl, i % nl, 0))
    return pl.pallas_call(
        functools.partial(_ffn_kernel, ff_blk=ff_blk),
        grid=((nb // g_blk) * nl,),
        in_specs=[tok(d), tok(SEG_W), tok(SEG_W),
                  pl.BlockSpec((g_blk, 6, d), lambda i: (i // nl, 0, 0)),
                  _resident((1, d)), _resident(wo.shape), _resident(wg.shape), _resident(wu.shape),
                  _resident(wd.shape)],
        out_specs=tok(d),
        out_shape=jax.ShapeDtypeStruct((nb, l, d), F32),
        compiler_params=_params(("parallel",)),
        name="outproj_ffn",
    )(x, oa, orr, mod, g2, wo, wg, wu, wd)


def _rope_tables(first_pos, n_blocks, blk):
    inv = 1.0 / (ROPE_BASE ** jnp.linspace(0.0, 1.0, HEAD_W // 2, dtype=F32))
    inv2 = jnp.repeat(inv, 2)[None, :]
    even = (jnp.arange(HEAD_W) % 2 == 0)[None, :]
    me, mo = jnp.where(even, -1.0, 0.0), jnp.where(even, 0.0, 1.0)
    t0 = (first_pos + blk * jnp.arange(n_blocks, dtype=jnp.int32)).astype(F32)[:, None] * inv2
    cb, sb = jnp.cos(t0), jnp.sin(t0)
    zero = jnp.zeros_like(cb)
    base = jnp.stack([cb, sb, cb * me, sb * me, cb * mo, sb * mo, zero, zero], axis=1)
    off = jnp.arange(blk, dtype=jnp.int32).astype(F32)[:, None] * inv2
    return base, jnp.cos(off), jnp.sin(off)


def _logit_bound(g_q, g_k, rel_bias):
    slack = 1.02
    dot_max = slack * DK_A * (DK_A ** -0.5 * LOG2E) * jnp.max(jnp.abs(g_q)) * jnp.max(jnp.abs(g_k))
    rb = (rel_bias - rel_bias[FAR_BUCKET][None, :]) * LOG2E
    b_hi = jnp.maximum(jnp.max(rb), 0.0)
    b_lo = jnp.minimum(jnp.min(rb), 0.0)
    ok = (2.0 * dot_max + (b_hi - b_lo)) <= MAX_LOGIT_SPREAD
    return jnp.stack([dot_max + b_hi, ok.astype(F32)])


def _tile_rows(n, target):
    t = min(n, target)
    while n % t:
        t //= 2
    return t


def kernel(x_prompt, x_sample, c_prompt, c_sample, cache_k, cache_v, state_ret, w_ada, b_ada, g_norm1, g_norm2, w_in, g_q, g_k, lam_q1, lam_k1, lam_q2, lam_k2, g_subln, w_out, w_ff_gate, w_ff_up, w_ff_down, rel_bias):
    depth = w_ada.shape[0]
    b, s, d = x_prompt.shape
    bs, l, _ = x_sample.shape
    p_len = cache_k.shape[2]
    assert w_in.shape[2] == N_SEG * SEG_W and d == 2 * SEG_W

    tm = _tile_rows(s, 512)
    tm_in = _tile_rows(s, 1024)
    tq = _tile_rows(s, 4096)
    tk = _tile_rows(tq, 256)
    tkd = _tile_rows(tq, 256)
    ret_blk = _tile_rows(s, 8192)
    ret_c = _tile_rows(ret_blk, 256)
    near = HEAD_W
    assert tq % CHUNK == 0 and tk % CHUNK == 0 and tkd % CHUNK == 0 and near >= BUCKET_THRESHOLDS[-1]

    tab_p = _rope_tables(0, s // tm_in, tm_in)
    tab_s = _rope_tables(p_len, 1, l)
    tbl_p = _bias_table(rel_bias, tkd, near + tkd, near)
    tbl_s = _bias_table(rel_bias, l, near + l, near)
    c_all = jnp.concatenate([c_prompt, c_sample], axis=0)

    xp, xs = x_prompt, x_sample
    outs = [[] for _ in range(6)]
    for layer in range(depth):
        lam_init = _lambda_init(layer)
        mod = _modulation(c_all, w_ada[layer], b_ada[layer]).reshape(b + bs, 6, d)
        mod_p, mod_s = mod[:b], mod[b:]
        g1 = g_norm1[layer].reshape(1, d)
        g2 = g_norm2[layer].reshape(1, d)
        gq_t = jnp.tile(g_q[layer], SEG_W // DK_A).reshape(1, SEG_W)
        gk_t = jnp.tile(g_k[layer], SEG_W // DK_A).reshape(1, SEG_W)
        lams = [v[layer].reshape(1, DK_A) for v in (lam_q1, lam_k1, lam_q2, lam_k2)]
        gs = g_subln[layer].reshape(1, HEAD_W)
        wi = w_in[layer].astype(BF16)
        wo = w_out[layer].astype(BF16)
        wg = w_ff_gate[layer].astype(BF16)
        wu = w_ff_up[layer].astype(BF16)
        wd = w_ff_down[layer].astype(BF16)
        ff_blk = 1024

        kf, vf, qa, ka, va, qr, kr, vr, sg = _inproj(xp, mod_p, g1, wi, gq_t, gk_t, tab_p, 1, tm_in)
        bound = _logit_bound(g_q[layer], g_k[layer], rel_bias)
        attn = lambda bounded: functools.partial(_attn_prompt, lam_init=lam_init, tq=tq, tk=tk, tkd=tkd,
                                                 near=near, bounded=bounded)
        oa = lax.cond(bound[1] > 0.5, attn(True), attn(False), bound, qa, ka, va, tbl_p, lams, gs)
        orr, st_p = _ret_prompt(qr, kr, vr, sg, ret_blk, ret_c)
        xp = _ffn(xp, oa, orr, mod_p, g2, wo, wg, wu, wd, 1, tm, ff_blk)

        kfs, vfs, qas, kas, vas, qrs, krs, vrs, sgs = _inproj(xs, mod_s, g1, wi, gq_t, gk_t, tab_s, bs, l)
        ck = jnp.swapaxes(cache_k[layer].reshape(bs, p_len, SEG_W), 1, 2)
        cv = cache_v[layer].reshape(bs, p_len * H_A, HEAD_W)
        oas, ors, st_s = _mix_sample(qas, kas, vas, ck, cv, tbl_s, lams, gs, qrs, krs, vrs, sgs,
                                     state_ret[layer], lam_init, near)
        xs = _ffn(xs, oas, ors, mod_s, g2, wo, wg, wu, wd, bs, l, ff_blk)

        for lst, val in zip(outs, (kf.reshape(b, s, H_A, 2, DK_A), vf.reshape(b, s, H_A, HEAD_W), st_p,
                                   kfs.reshape(bs, l, H_A, 2, DK_A), vfs.reshape(bs, l, H_A, HEAD_W), st_s)):
            lst.append(val)
    return (xp, xs) + tuple(jnp.stack(o) for o in outs)
```

```python
import functools
import math

import jax
import jax.numpy as jnp
from jax import lax
from jax.experimental import pallas as pl
from jax.experimental.pallas import tpu as pltpu

F32 = jnp.float32
BF16 = jnp.bfloat16

CHUNK = 64
H_A = 4
DK_A = 64
H_R = 4
HEAD_W = 128
SEG_W = H_A * HEAD_W
N_SEG = 7
N_BUCKETS = 32
ROPE_BASE = 10000.0
EPS = 1e-6
NEG_INF = -1e30
LOG2E = math.log2(math.e)
BUCKET_THRESHOLDS = (12, 16, 23, 32, 46, 64, 91)
FAR_BUCKET = 15
MAX_LOGIT_SPREAD = 100.0

MXU_W = 256
VMEM_LIMIT = 56 * 1024 * 1024


def _lambda_init(layer):
    return 0.8 - 0.6 * math.exp(-0.3 * layer)


def _params(sem):
    return pltpu.CompilerParams(dimension_semantics=sem, vmem_limit_bytes=VMEM_LIMIT)


def _resident(shape):
    return pl.BlockSpec(shape, lambda *_: (0,) * len(shape), pipeline_mode=pl.Buffered(1))


def _dot(a, b):
    return jnp.dot(a, b, preferred_element_type=F32)


def _dot_nt(a, b):
    return lax.dot_general(a, b, (((1,), (1,)), ((), ())), preferred_element_type=F32)


def _dot_tn(a, b):
    return lax.dot_general(a, b, (((0,), (0,)), ((), ())), preferred_element_type=F32)


def _silu(x):
    return x * jax.nn.sigmoid(x)


def _mod_kernel(c_ref, w_ref, b_ref, o_ref):
    s = _silu(c_ref[...]).astype(BF16)
    o_ref[...] = _dot(s, w_ref[...].astype(BF16)) + b_ref[...]


def _modulation(c_all, w_ada, b_ada):
    n, d = c_all.shape
    e = w_ada.shape[1]
    te = 1536
    assert e % te == 0
    return pl.pallas_call(
        _mod_kernel,
        grid=(e // te,),
        in_specs=[pl.BlockSpec((n, d), lambda j: (0, 0)),
                  pl.BlockSpec((d, te), lambda j: (0, j)),
                  pl.BlockSpec((1, te), lambda j: (0, j))],
        out_specs=pl.BlockSpec((n, te), lambda j: (0, j)),
        out_shape=jax.ShapeDtypeStruct((n, e), F32),
        compiler_params=_params(("parallel",)),
        name="modulation",
    )(c_all, w_ada, b_ada.reshape(1, e))


def _bias_table_kernel(rb_ref, o_ref, *, offset):
    h = pl.program_id(0)
    _, rows, cols = o_ref.shape
    r = lax.broadcasted_iota(jnp.int32, (rows, cols), 0)
    kc = lax.broadcasted_iota(jnp.int32, (rows, cols), 1) - offset
    rel = kc - r
    n = jnp.abs(rel)
    large = jnp.full((rows, cols), N_BUCKETS // 4, jnp.int32)
    for thr in BUCKET_THRESHOLDS:
        large = large + (n >= thr).astype(jnp.int32)
    bucket = jnp.where(rel > 0, N_BUCKETS // 2, 0) + jnp.where(n < N_BUCKETS // 4, n, large)
    val = jnp.zeros((rows, cols), F32)
    for b in range(N_BUCKETS):
        val = jnp.where(bucket == b, rb_ref[b, h], val)
    val = (val - rb_ref[FAR_BUCKET, h]) * LOG2E
    shift = int(math.log2(CHUNK))
    visible = lax.shift_right_arithmetic(kc, shift) <= lax.shift_right_arithmetic(r, shift)
    o_ref[0] = jnp.where(visible, val, NEG_INF)


def _bias_table(rel_bias, rows, cols, offset):
    assert offset >= BUCKET_THRESHOLDS[-1] and offset % CHUNK == 0
    return pl.pallas_call(
        functools.partial(_bias_table_kernel, offset=offset),
        grid=(H_A,),
        in_specs=[pl.BlockSpec(memory_space=pltpu.SMEM)],
        out_specs=pl.BlockSpec((1, rows, cols), lambda h: (h, 0, 0)),
        out_shape=jax.ShapeDtypeStruct((H_A, rows, cols), F32),
        compiler_params=_params(("parallel",)),
        name="bias_table",
    )(rel_bias)


def _inproj_kernel(x_ref, mod_ref, g1_ref, w_ref, gq_ref, gk_ref, rope_ref, cr_ref, sr_ref,
                   kf_ref, vf_ref, qa_ref, ka_ref, va_ref, qr_ref, kr_ref, vr_ref, sg_ref):
    g, r, d = x_ref.shape
    m = g * r
    x = x_ref[...]
    gain = g1_ref[...] * (1.0 + mod_ref[:, 1:2, :])
    hmod = x * lax.rsqrt(jnp.mean(x * x, axis=-1, keepdims=True) + EPS) * gain + mod_ref[:, 0:1, :]
    hb = hmod.astype(BF16).reshape(m, d)

    def seg(j):
        return _dot(hb, w_ref[:, j * SEG_W:(j + 1) * SEG_W])

    def put(ref, val):
        ref[...] = val.reshape(g, r, SEG_W).astype(ref.dtype)

    gi = lax.broadcasted_iota(jnp.int32, (MXU_W, MXU_W), 0) // DK_A
    gj = lax.broadcasted_iota(jnp.int32, (MXU_W, MXU_W), 1) // DK_A
    ones_bd = (gi == gj).astype(BF16)

    def group_rms(z):
        sq = (z * z).astype(BF16)
        ms = jnp.concatenate([_dot(sq[:, c:c + MXU_W], ones_bd) for c in range(0, SEG_W, MXU_W)], axis=1)
        return z * lax.rsqrt(ms * (1.0 / DK_A) + EPS)

    cr, sr = cr_ref[...], sr_ref[...]

    def table(c_row, s_row, sign):
        t = rope_ref[0, c_row:c_row + 1, :] * cr + sign * (rope_ref[0, s_row:s_row + 1, :] * sr)
        return jnp.broadcast_to(t[None], (g, r, HEAD_W)).reshape(m, HEAD_W)

    cos, sine, sino = table(0, 1, -1.0), table(3, 2, 1.0), table(5, 4, 1.0)

    def rotate(z):
        outs = []
        for hh in range(H_R):
            zh = z[:, hh * HEAD_W:(hh + 1) * HEAD_W]
            nxt = pltpu.roll(zh, HEAD_W - 1, axis=1)
            prv = pltpu.roll(zh, 1, axis=1)
            outs.append(zh * cos + nxt * sine + prv * sino)
        return jnp.concatenate(outs, axis=1)

    put(qa_ref, group_rms(seg(0)) * gq_ref[...] * (DK_A ** -0.5 * LOG2E))
    ka = group_rms(seg(1)) * gk_ref[...]
    put(kf_ref, ka)
    put(ka_ref, ka)
    put(qr_ref, rotate(seg(3)))
    put(kr_ref, rotate(seg(4)) * (HEAD_W ** -0.5))
    put(sg_ref, _silu(seg(6)))
    va = seg(2)
    for hh in range(H_A):
        vf_ref[:, pl.ds(hh, r, stride=H_A), :] = va[:, hh * HEAD_W:(hh + 1) * HEAD_W].reshape(g, r, HEAD_W)
    put(va_ref, va)
    put(vr_ref, seg(5))


N_INPROJ_OUT = 9


def _tail_step_kernel(body, n_a, n_shared, n_out, *refs):
    a_in, b_in = refs[:n_a], refs[n_a:2 * n_a]
    shared = refs[2 * n_a:2 * n_a + n_shared]
    outs = refs[2 * n_a + n_shared:]
    a_out, b_out = outs[:n_out], outs[n_out:]
    last = pl.num_programs(0) - 1

    @pl.when(pl.program_id(0) < last)
    def _():
        body(a_in, shared, a_out)

    @pl.when(pl.program_id(0) == last)
    def _():
        body(b_in, shared, b_out)


def _inproj_body(stream, shared, outs):
    x_ref, mod_ref, rope_ref, cr_ref, sr_ref = stream
    g1_ref, w_ref, gq_ref, gk_ref = shared
    _inproj_kernel(x_ref, mod_ref, g1_ref, w_ref, gq_ref, gk_ref, rope_ref, cr_ref, sr_ref, *outs)


def _inproj(xp, mod_p, tab_p, r_blk, xs, mod_s, tab_s, g1, w_in, gq_t, gk_t):
    b, s, d = xp.shape
    bs, l, _ = xs.shape
    assert s % r_blk == 0
    nl = s // r_blk
    n = b * nl
    pj = lambda i: jnp.minimum(i, n - 1)
    tok_p = lambda w, rows=r_blk: pl.BlockSpec((1, rows, w), lambda i: (pj(i) // nl, pj(i) % nl, 0))
    whole = lambda shape: pl.BlockSpec(shape, lambda i: (0,) * len(shape))
    stream_p = [tok_p(d), pl.BlockSpec((1, 6, d), lambda i: (pj(i) // nl, 0, 0)),
                pl.BlockSpec((1, 8, HEAD_W), lambda i: (pj(i) % nl, 0, 0)),
                _resident((r_blk, HEAD_W)), _resident((r_blk, HEAD_W))]
    stream_s = [whole((bs, l, d)), whole((bs, 6, d)), whole((1, 8, HEAD_W)), whole((l, HEAD_W)), whole((l, HEAD_W))]
    shared = [_resident((1, d)), _resident(w_in.shape), _resident((1, SEG_W)), _resident((1, SEG_W))]
    out_p = [tok_p(SEG_W), tok_p(HEAD_W, r_blk * H_A)] + [tok_p(SEG_W)] * 7
    out_s = [whole((bs, l, SEG_W)), whole((bs, l * H_A, HEAD_W))] + [whole((bs, l, SEG_W))] * 7
    shapes = lambda nb, t: ([jax.ShapeDtypeStruct((nb, t, SEG_W), F32), jax.ShapeDtypeStruct((nb, t * H_A, HEAD_W), F32)]
                            + [jax.ShapeDtypeStruct((nb, t, SEG_W), BF16)] * 7)
    res = pl.pallas_call(
        functools.partial(_tail_step_kernel, _inproj_body, len(stream_p), len(shared), N_INPROJ_OUT),
        grid=(n + 1,),
        in_specs=stream_p + stream_s + shared,
        out_specs=out_p + out_s,
        out_shape=shapes(b, s) + shapes(bs, l),
        compiler_params=_params(("arbitrary",)),
        name="inproj",
    )(xp, mod_p, *tab_p, xs, mod_s, *tab_s, g1, w_in, gq_t, gk_t)
    return res[:N_INPROJ_OUT], res[N_INPROJ_OUT:]


def _stack_maps(q):
    lane = lax.broadcasted_iota(jnp.int32, q.shape, 1)
    zero = jnp.zeros_like(q)
    return jnp.concatenate([jnp.where(lane < DK_A, q, zero), jnp.where(lane >= DK_A, q, zero)], axis=0)


def _lambda(lq1, lk1, lq2, lk2, lam_init):
    s1 = jnp.sum(lq1[...] * lk1[...], axis=-1, keepdims=True)
    s2 = jnp.sum(lq2[...] * lk2[...], axis=-1, keepdims=True)
    return jnp.exp(s1) - jnp.exp(s2) + lam_init


def _diff_finish(acc, l, lam, gs, lam_init):
    t = acc.shape[0] // 2
    o = acc[:t] * (1.0 / l[:t]) - lam * (acc[t:] * (1.0 / l[t:]))
    on = o * lax.rsqrt(jnp.mean(o * o, axis=-1, keepdims=True) + EPS)
    return on * gs * (1.0 - lam_init)


def _attn_prompt_kernel(bound_ref, q_ref, k_ref, v_ref, tbl_ref, lq1, lk1, lq2, lk2, gs_ref, o_ref,
                        qq_sc, m_sc, l_sc, acc_sc, *, lam_init, tk, tkd, near, bounded):
    qi = pl.program_id(2)
    tq = q_ref.shape[1]
    q = q_ref[0]
    lane = lax.broadcasted_iota(jnp.int32, q.shape, 1)
    zero = jnp.zeros_like(q)
    qq_sc[0] = jnp.where(lane < DK_A, q, zero)
    qq_sc[1] = jnp.where(lane >= DK_A, q, zero)

    def tile(start, width, row0, corner, diag, bounded, first=False):
        nr = tq - row0
        nblk = width // HEAD_W
        rows = slice(row0, tq)
        flat = lambda x: x.reshape(2 * nr, x.shape[-1])
        kt = k_ref[0, pl.ds(start, width), :]
        vt = v_ref[0, pl.ds(start, width), :]
        s = _dot_nt(flat(qq_sc[:, rows, :]), kt)
        blocks = [s[:, c * HEAD_W:(c + 1) * HEAD_W] for c in range(nblk)]

        def add_rows(blk, r0, val):
            r1 = r0 + val.shape[0]
            cuts = [(0, r0, False), (r0, r1, True), (r1, nr + r0, False), (nr + r0, nr + r1, True),
                    (nr + r1, 2 * nr, False)]
            return jnp.concatenate([blk[a:b] + val if hit else blk[a:b] for a, b, hit in cuts if b > a], axis=0)

        if diag:
            blocks = [add_rows(blk, 0, tbl_ref[0, :, near + c * HEAD_W:near + (c + 1) * HEAD_W])
                      for c, blk in enumerate(blocks)]
        corner_row0 = width if diag else 0
        if corner_row0 < nr:
            blocks[-1] = add_rows(blocks[-1], corner_row0, corner)
        if bounded:
            ps = [jnp.exp2(blk - bound_ref[0]) for blk in blocks]
            l_new = functools.reduce(jnp.add, ps)
            p = jnp.concatenate([pc.astype(BF16) for pc in ps], axis=1)
            acc_new = _dot(p, vt)
            if not first:
                l_new = flat(l_sc[:, rows, :]) + l_new
                acc_new = flat(acc_sc[:, rows, :]) + acc_new
            l_sc[:, rows, :] = l_new.reshape(2, nr, HEAD_W)
            acc_sc[:, rows, :] = acc_new.reshape(2, nr, HEAD_W)
            return
        m_old = flat(m_sc[:, rows, :])
        mx = functools.reduce(jnp.maximum, blocks)
        m_new = jnp.maximum(m_old, jnp.max(mx, axis=-1, keepdims=True))
        alpha = jnp.exp2(m_old - m_new)
        ps = [jnp.exp2(blk - m_new) for blk in blocks]
        l_new = alpha * flat(l_sc[:, rows, :]) + functools.reduce(jnp.add, ps)
        p = jnp.concatenate([pc.astype(BF16) for pc in ps], axis=1)
        acc_new = alpha * flat(acc_sc[:, rows, :]) + _dot(p, vt)
        m_sc[:, rows, :] = m_new.reshape(2, nr, HEAD_W)
        l_sc[:, rows, :] = l_new.reshape(2, nr, HEAD_W)
        acc_sc[:, rows, :] = acc_new.reshape(2, nr, HEAD_W)

    n_far = qi * (tq // tk)
    corner_tbl = tbl_ref[0, :near, :near]

    def diag_tiles(bounded, first):
        for d in range(tq // tkd):
            tile(pl.multiple_of(qi * tq + d * tkd, tkd), tkd, d * tkd, corner_tbl, True, bounded, first and d == 0)

    def run(bounded):
        if bounded:
            diag_tiles(True, True)

        def far_tile(j):
            flag = (j == n_far - 1).astype(F32)
            tile(pl.multiple_of(j * tk, tk), tk, 0, corner_tbl * flag, False, bounded)

        odd = jnp.bitwise_and(n_far, 1)

        @pl.when(odd == 1)
        def _():
            far_tile(0)

        def far_pair(i, carry):
            far_tile(odd + 2 * i)
            far_tile(odd + 2 * i + 1)
            return carry

        lax.fori_loop(0, lax.shift_right_logical(n_far, 1), far_pair, 0)
        if not bounded:
            diag_tiles(False, False)

    if not bounded:
        m_sc[...] = jnp.full(m_sc.shape, NEG_INF, F32)
        l_sc[...] = jnp.zeros(l_sc.shape, F32)
        acc_sc[...] = jnp.zeros(acc_sc.shape, F32)
    run(bounded)

    lam = _lambda(lq1, lk1, lq2, lk2, lam_init)
    l = jnp.sum(l_sc[...], axis=-1, keepdims=True).reshape(2 * tq, 1)
    acc = acc_sc[...].reshape(2 * tq, HEAD_W)
    o_ref[0] = _diff_finish(acc, l, lam, gs_ref[...], lam_init).astype(o_ref.dtype)


def _attn_prompt(bound, q, k, v, tbl, lams, gs, lam_init, tq, tk, tkd, near, bounded):
    b, s, _ = q.shape
    assert s % tq == 0 and tq % tk == 0 and tq % tkd == 0 and tbl.shape == (H_A, tkd, near + tkd)
    assert near == HEAD_W and tk % HEAD_W == 0 and tkd % HEAD_W == 0
    vec = _resident((1, DK_A))
    return pl.pallas_call(
        functools.partial(_attn_prompt_kernel, lam_init=lam_init, tk=tk, tkd=tkd, near=near, bounded=bounded),
        grid=(b, H_A, s // tq),
        in_specs=[pl.BlockSpec(memory_space=pltpu.SMEM),
                  pl.BlockSpec((1, tq, HEAD_W), lambda bi, h, qi: (bi, qi, h)),
                  pl.BlockSpec((1, s, HEAD_W), lambda bi, h, qi: (bi, 0, h)),
                  pl.BlockSpec((1, s, HEAD_W), lambda bi, h, qi: (bi, 0, h)),
                  pl.BlockSpec((1, tkd, near + tkd), lambda bi, h, qi: (h, 0, 0)),
                  vec, vec, vec, vec, _resident((1, HEAD_W))],
        out_specs=pl.BlockSpec((1, tq, HEAD_W), lambda bi, h, qi: (bi, qi, h)),
        out_shape=jax.ShapeDtypeStruct((b, s, SEG_W), BF16),
        scratch_shapes=[pltpu.VMEM((2, tq, HEAD_W), BF16)] + [pltpu.VMEM((2, tq, HEAD_W), F32)] * 3,
        compiler_params=_params(("parallel", "parallel", "arbitrary")),
        name="attn_prompt_bounded" if bounded else "attn_prompt_exact",
    )(bound, q, k, v, tbl, *lams, gs)


def _log_gamma(h):
    vals = [math.log(1.0 - 2.0 ** (-5.0 - i)) for i in range(H_R)]
    lg = jnp.float32(vals[H_R - 1])
    for i in range(H_R - 2, -1, -1):
        lg = jnp.where(h == i, jnp.float32(vals[i]), lg)
    return lg


def _retention_chunk(q, k, v, state, lg):
    c = q.shape[0]
    ti = lax.broadcasted_iota(jnp.int32, (c, c), 0)
    si = lax.broadcasted_iota(jnp.int32, (c, c), 1)
    dist = (ti - si).astype(F32)
    decay = jnp.where(dist >= 0, jnp.exp(jnp.maximum(dist, 0.0) * lg), 0.0)
    scores = _dot_nt(q, k) * decay
    row = lax.broadcasted_iota(jnp.int32, (c, 1), 0).astype(F32)
    q_dec = (q.astype(F32) * jnp.exp((row + 1.0) * lg)).astype(BF16)
    o = _dot(scores.astype(BF16), v) + _dot(q_dec, state.astype(BF16))
    k_dec = (k.astype(F32) * jnp.exp((c - 1.0 - row) * lg)).astype(BF16)
    new_state = jnp.exp(c * lg) * state + _dot_tn(k_dec, v)
    return o, new_state


def _retention_out(o, sg):
    on = o * lax.rsqrt(jnp.mean(o * o, axis=-1, keepdims=True) + EPS)
    return on * sg.astype(F32)


def _ret_prompt_kernel(q_ref, k_ref, v_ref, sg_ref, o_ref, st_ref, s_sc, d_sc, qd_sc, kd_sc, *, c):
    h = pl.program_id(1)
    ti = pl.program_id(2)
    tb = q_ref.shape[1]

    @pl.when(ti == 0)
    def _():
        lg = _log_gamma(h)
        s_sc[...] = jnp.zeros(s_sc.shape, F32)
        tt = lax.broadcasted_iota(jnp.int32, (c, c), 0)
        ss = lax.broadcasted_iota(jnp.int32, (c, c), 1)
        dist = (tt - ss).astype(F32)
        d_sc[...] = jnp.where(dist >= 0, jnp.exp(jnp.maximum(dist, 0.0) * lg), 0.0)
        row = lax.broadcasted_iota(jnp.int32, (c, HEAD_W), 0).astype(F32)
        qd_sc[...] = jnp.exp((row + 1.0) * lg)
        kd_sc[...] = jnp.exp((c - 1.0 - row) * lg)

    state = s_sc[...]
    g_c = qd_sc[c - 1:c, :]
    for i in range(tb // c):
        rows = slice(i * c, (i + 1) * c)
        q, k, v = q_ref[0, rows, :], k_ref[0, rows, :], v_ref[0, rows, :]
        scores = _dot_nt(q, k) * d_sc[...]
        q_dec = (q.astype(F32) * qd_sc[...]).astype(BF16)
        o = _dot(scores.astype(BF16), v) + _dot(q_dec, state.astype(BF16))
        k_dec = (k.astype(F32) * kd_sc[...]).astype(BF16)
        state = g_c * state + _dot_tn(k_dec, v)
        o_ref[0, rows, :] = _retention_out(o, sg_ref[0, rows, :]).astype(o_ref.dtype)
    s_sc[...] = state

    @pl.when(ti == pl.num_programs(2) - 1)
    def _():
        st_ref[0, 0] = state


def _ret_prompt(qr, kr, vr, sg, t_blk, c):
    b, s, _ = qr.shape
    assert s % t_blk == 0 and t_blk % c == 0
    blk = pl.BlockSpec((1, t_blk, HEAD_W), lambda bi, h, ti: (bi, ti, h))
    return pl.pallas_call(
        functools.partial(_ret_prompt_kernel, c=c),
        grid=(b, H_R, s // t_blk),
        in_specs=[blk, blk, blk, blk],
        out_specs=[blk, pl.BlockSpec((1, 1, HEAD_W, HEAD_W), lambda bi, h, ti: (bi, h, 0, 0))],
        out_shape=[jax.ShapeDtypeStruct((b, s, SEG_W), BF16),
                   jax.ShapeDtypeStruct((b, H_R, HEAD_W, HEAD_W), F32)],
        scratch_shapes=[pltpu.VMEM((HEAD_W, HEAD_W), F32), pltpu.VMEM((c, c), F32),
                        pltpu.VMEM((c, HEAD_W), F32), pltpu.VMEM((c, HEAD_W), F32)],
        compiler_params=_params(("parallel", "parallel", "arbitrary")),
        name="ret_prompt",
    )(qr, kr, vr, sg)


def _mix_sample_kernel(qa_ref, ka_ref, va_ref, ck_ref, cv_ref, tbl_ref, lq1, lk1, lq2, lk2, gs_ref,
                       qr_ref, kr_ref, vr_ref, sg_ref, st_ref, oa_ref, or_ref, so_ref, *, lam_init, near):
    p_len = ck_ref.shape[2]
    l = qa_ref.shape[1]
    lam = _lambda(lq1, lk1, lq2, lk2, lam_init)
    for h in range(H_A):
        cols = slice(h * HEAD_W, (h + 1) * HEAD_W)
        qq = _stack_maps(qa_ref[0, :, cols])
        kc = ck_ref[0, cols, :].astype(BF16)
        vc = cv_ref[0, pl.ds(h, p_len, stride=H_A), :].astype(BF16)
        bias = tbl_ref[h]
        bias2 = jnp.concatenate([bias, bias], axis=0)
        s_c = _dot(qq, kc)
        s_c = jnp.concatenate([s_c[:, :p_len - near], s_c[:, p_len - near:] + bias2[:, :near]], axis=1)
        s_n = _dot_nt(qq, ka_ref[0, :, cols]) + bias2[:, near:]
        m = jnp.maximum(jnp.max(s_c, axis=-1, keepdims=True), jnp.max(s_n, axis=-1, keepdims=True))
        p_c = jnp.exp2(s_c - m)
        p_n = jnp.exp2(s_n - m)
        lsum = jnp.sum(p_c, axis=-1, keepdims=True) + jnp.sum(p_n, axis=-1, keepdims=True)
        acc = _dot(p_c.astype(BF16), vc) + _dot(p_n.astype(BF16), va_ref[0, :, cols])
        oa_ref[0, :, cols] = _diff_finish(acc, lsum, lam, gs_ref[...], lam_init).astype(oa_ref.dtype)
    for h in range(H_R):
        cols = slice(h * HEAD_W, (h + 1) * HEAD_W)
        lg = jnp.float32(math.log(1.0 - 2.0 ** (-5.0 - h)))
        o, new_state = _retention_chunk(qr_ref[0, :, cols], kr_ref[0, :, cols], vr_ref[0, :, cols],
                                        st_ref[0, h], lg)
        or_ref[0, :, cols] = _retention_out(o, sg_ref[0, :, cols]).astype(or_ref.dtype)
        so_ref[0, h] = new_state


def _mix_sample(qa, ka, va, cache_k, cache_v, tbl, lams, gs, qr, kr, vr, sg, state, lam_init, near):
    b, l, _ = qa.shape
    p_len = cache_k.shape[2]
    assert p_len % CHUNK == 0 and l <= CHUNK and p_len >= near
    tok = pl.BlockSpec((1, l, SEG_W), lambda bi: (bi, 0, 0))
    cache_kt = pl.BlockSpec((1, SEG_W, p_len), lambda bi: (bi, 0, 0))
    cache_vh = pl.BlockSpec((1, p_len * H_A, HEAD_W), lambda bi: (bi, 0, 0))
    st = pl.BlockSpec((1, H_R, HEAD_W, HEAD_W), lambda bi: (bi, 0, 0, 0))
    vec = _resident((1, DK_A))
    return pl.pallas_call(
        functools.partial(_mix_sample_kernel, lam_init=lam_init, near=near),
        grid=(b,),
        in_specs=[tok, tok, tok, cache_kt, cache_vh, _resident(tbl.shape), vec, vec, vec, vec,
                  _resident((1, HEAD_W)), tok, tok, tok, tok, st],
        out_specs=[tok, tok, st],
        out_shape=[jax.ShapeDtypeStruct((b, l, SEG_W), BF16), jax.ShapeDtypeStruct((b, l, SEG_W), BF16),
                   jax.ShapeDtypeStruct((b, H_R, HEAD_W, HEAD_W), F32)],
        compiler_params=_params(("parallel",)),
        name="mix_sample",
    )(qa, ka, va, cache_k, cache_v, tbl, *lams, gs, qr, kr, vr, sg, state)


def _ffn_kernel(x_ref, oa_ref, or_ref, mod_ref, g2_ref, wo_ref, wg_ref, wu_ref, wd_ref, y_ref, *, ff_blk):
    g, r, d = x_ref.shape
    m = g * r
    f = wg_ref.shape[1]
    proj = (_dot(oa_ref[...].reshape(m, SEG_W), wo_ref[:SEG_W, :])
            + _dot(or_ref[...].reshape(m, SEG_W), wo_ref[SEG_W:, :]))
    x1 = x_ref[...] + mod_ref[:, 2:3, :] * proj.reshape(g, r, d)
    gain = g2_ref[...] * (1.0 + mod_ref[:, 4:5, :])
    y = x1 * lax.rsqrt(jnp.mean(x1 * x1, axis=-1, keepdims=True) + EPS) * gain
    h2 = (y + mod_ref[:, 3:4, :]).astype(BF16).reshape(m, d)
    acc = jnp.zeros((m, d), F32)
    for c0 in range(0, f, ff_blk):
        c1 = min(c0 + ff_blk, f)
        ff = (_silu(_dot(h2, wg_ref[:, c0:c1])) * _dot(h2, wu_ref[:, c0:c1])).astype(BF16)
        acc = acc + _dot(ff, wd_ref[c0:c1, :])
    y_ref[...] = x1 + mod_ref[:, 5:6, :] * acc.reshape(g, r, d)


def _ffn_body(ff_blk, stream, shared, outs):
    _ffn_kernel(*stream, *shared, *outs, ff_blk=ff_blk)


def _ffn(xp, oa_p, or_p, mod_p, r_blk, xs, oa_s, or_s, mod_s, g2, wo, wg, wu, wd, ff_blk):
    b, s, d = xp.shape
    bs, l, _ = xs.shape
    assert s % r_blk == 0
    nl = s // r_blk
    n = b * nl
    pj = lambda i: jnp.minimum(i, n - 1)
    tok_p = lambda w: pl.BlockSpec((1, r_blk, w), lambda i: (pj(i) // nl, pj(i) % nl, 0))
    whole = lambda shape: pl.BlockSpec(shape, lambda i: (0,) * len(shape))
    stream_p = [tok_p(d), tok_p(SEG_W), tok_p(SEG_W), pl.BlockSpec((1, 6, d), lambda i: (pj(i) // nl, 0, 0))]
    stream_s = [whole((bs, l, d)), whole((bs, l, SEG_W)), whole((bs, l, SEG_W)), whole((bs, 6, d))]
    shared = [_resident((1, d)), _resident(wo.shape), _resident(wg.shape), _resident(wu.shape), _resident(wd.shape)]
    return pl.pallas_call(
        functools.partial(_tail_step_kernel, functools.partial(_ffn_body, ff_blk), len(stream_p), len(shared), 1),
        grid=(n + 1,),
        in_specs=stream_p + stream_s + shared,
        out_specs=[tok_p(d), whole((bs, l, d))],
        out_shape=[jax.ShapeDtypeStruct((b, s, d), F32), jax.ShapeDtypeStruct((bs, l, d), F32)],
        compiler_params=_params(("arbitrary",)),
        name="outproj_ffn",
    )(xp, oa_p, or_p, mod_p, xs, oa_s, or_s, mod_s, g2, wo, wg, wu, wd)


def _rope_tables(first_pos, n_blocks, blk):
    inv = 1.0 / (ROPE_BASE ** jnp.linspace(0.0, 1.0, HEAD_W // 2, dtype=F32))
    inv2 = jnp.repeat(inv, 2)[None, :]
    even = (jnp.arange(HEAD_W) % 2 == 0)[None, :]
    me, mo = jnp.where(even, -1.0, 0.0), jnp.where(even, 0.0, 1.0)
    t0 = (first_pos + blk * jnp.arange(n_blocks, dtype=jnp.int32)).astype(F32)[:, None] * inv2
    cb, sb = jnp.cos(t0), jnp.sin(t0)
    zero = jnp.zeros_like(cb)
    base = jnp.stack([cb, sb, cb * me, sb * me, cb * mo, sb * mo, zero, zero], axis=1)
    off = jnp.arange(blk, dtype=jnp.int32).astype(F32)[:, None] * inv2
    return base, jnp.cos(off), jnp.sin(off)


def _logit_bound(g_q, g_k, rel_bias):
    slack = 1.02
    dot_max = slack * DK_A * (DK_A ** -0.5 * LOG2E) * jnp.max(jnp.abs(g_q)) * jnp.max(jnp.abs(g_k))
    rb = (rel_bias - rel_bias[FAR_BUCKET][None, :]) * LOG2E
    b_hi = jnp.maximum(jnp.max(rb), 0.0)
    b_lo = jnp.minimum(jnp.min(rb), 0.0)
    ok = (2.0 * dot_max + (b_hi - b_lo)) <= MAX_LOGIT_SPREAD
    return jnp.stack([dot_max + b_hi, ok.astype(F32)])


def _tile_rows(n, target):
    t = min(n, target)
    while n % t:
        t //= 2
    return t


def kernel(x_prompt, x_sample, c_prompt, c_sample, cache_k, cache_v, state_ret, w_ada, b_ada, g_norm1, g_norm2, w_in, g_q, g_k, lam_q1, lam_k1, lam_q2, lam_k2, g_subln, w_out, w_ff_gate, w_ff_up, w_ff_down, rel_bias):
    depth = w_ada.shape[0]
    b, s, d = x_prompt.shape
    bs, l, _ = x_sample.shape
    p_len = cache_k.shape[2]
    assert w_in.shape[2] == N_SEG * SEG_W and d == 2 * SEG_W

    tm = _tile_rows(s, 512)
    tm_in = _tile_rows(s, 1024)
    tq = _tile_rows(s, 4096)
    tk = _tile_rows(tq, 256)
    tkd = _tile_rows(tq, 256)
    ret_blk = _tile_rows(s, 8192)
    ret_c = _tile_rows(ret_blk, 256)
    near = HEAD_W
    assert tq % CHUNK == 0 and tk % CHUNK == 0 and tkd % CHUNK == 0 and near >= BUCKET_THRESHOLDS[-1]

    tab_p = _rope_tables(0, s // tm_in, tm_in)
    tab_s = _rope_tables(p_len, 1, l)
    tbl_p = _bias_table(rel_bias, tkd, near + tkd, near)
    tbl_s = _bias_table(rel_bias, l, near + l, near)
    c_all = jnp.concatenate([c_prompt, c_sample], axis=0)

    xp, xs = x_prompt, x_sample
    outs = [[] for _ in range(6)]
    for layer in range(depth):
        lam_init = _lambda_init(layer)
        mod = _modulation(c_all, w_ada[layer], b_ada[layer]).reshape(b + bs, 6, d)
        mod_p, mod_s = mod[:b], mod[b:]
        g1 = g_norm1[layer].reshape(1, d)
        g2 = g_norm2[layer].reshape(1, d)
        gq_t = jnp.tile(g_q[layer], SEG_W // DK_A).reshape(1, SEG_W)
        gk_t = jnp.tile(g_k[layer], SEG_W // DK_A).reshape(1, SEG_W)
        lams = [v[layer].reshape(1, DK_A) for v in (lam_q1, lam_k1, lam_q2, lam_k2)]
        gs = g_subln[layer].reshape(1, HEAD_W)
        wi = w_in[layer].astype(BF16)
        wo = w_out[layer].astype(BF16)
        wg = w_ff_gate[layer].astype(BF16)
        wu = w_ff_up[layer].astype(BF16)
        wd = w_ff_down[layer].astype(BF16)
        ff_blk = 1024

        (kf, vf, qa, ka, va, qr, kr, vr, sg), (kfs, vfs, qas, kas, vas, qrs, krs, vrs, sgs) = _inproj(
            xp, mod_p, tab_p, tm_in, xs, mod_s, tab_s, g1, wi, gq_t, gk_t)
        bound = _logit_bound(g_q[layer], g_k[layer], rel_bias)
        attn = lambda bounded: functools.partial(_attn_prompt, lam_init=lam_init, tq=tq, tk=tk, tkd=tkd,
                                                 near=near, bounded=bounded)
        oa = lax.cond(bound[1] > 0.5, attn(True), attn(False), bound, qa, ka, va, tbl_p, lams, gs)
        orr, st_p = _ret_prompt(qr, kr, vr, sg, ret_blk, ret_c)
        ck = jnp.swapaxes(cache_k[layer].reshape(bs, p_len, SEG_W), 1, 2)
        cv = cache_v[layer].reshape(bs, p_len * H_A, HEAD_W)
        oas, ors, st_s = _mix_sample(qas, kas, vas, ck, cv, tbl_s, lams, gs, qrs, krs, vrs, sgs,
                                     state_ret[layer], lam_init, near)
        xp, xs = _ffn(xp, oa, orr, mod_p, tm, xs, oas, ors, mod_s, g2, wo, wg, wu, wd, ff_blk)

        for lst, val in zip(outs, (kf.reshape(b, s, H_A, 2, DK_A), vf.reshape(b, s, H_A, HEAD_W), st_p,
                                   kfs.reshape(bs, l, H_A, 2, DK_A), vfs.reshape(bs, l, H_A, HEAD_W), st_s)):
            lst.append(val)
    return (xp, xs) + tuple(jnp.stack(o) for o in outs)
```

```python
import functools
import math

import jax
import jax.numpy as jnp
from jax import lax
from jax.experimental import pallas as pl
from jax.experimental.pallas import tpu as pltpu

F32 = jnp.float32
BF16 = jnp.bfloat16

CHUNK = 64
H_A = 4
DK_A = 64
H_R = 4
HEAD_W = 128
SEG_W = H_A * HEAD_W
N_SEG = 7
N_BUCKETS = 32
ROPE_BASE = 10000.0
EPS = 1e-6
NEG_INF = -1e30
LOG2E = math.log2(math.e)
BUCKET_THRESHOLDS = (12, 16, 23, 32, 46, 64, 91)
FAR_BUCKET = 15
MAX_LOGIT_SPREAD = 100.0

MXU_W = 256
VMEM_LIMIT = 56 * 1024 * 1024


def _lambda_init(layer):
    return 0.8 - 0.6 * math.exp(-0.3 * layer)


def _params(sem):
    return pltpu.CompilerParams(dimension_semantics=sem, vmem_limit_bytes=VMEM_LIMIT)


def _resident(shape):
    return pl.BlockSpec(shape, lambda *_: (0,) * len(shape), pipeline_mode=pl.Buffered(1))


def _dot(a, b):
    return jnp.dot(a, b, preferred_element_type=F32)


def _dot_nt(a, b):
    return lax.dot_general(a, b, (((1,), (1,)), ((), ())), preferred_element_type=F32)


def _dot_tn(a, b):
    return lax.dot_general(a, b, (((0,), (0,)), ((), ())), preferred_element_type=F32)


def _silu(x):
    return x * jax.nn.sigmoid(x)


def _mod_kernel(c_ref, w_ref, b_ref, o_ref):
    s = _silu(c_ref[...]).astype(BF16)
    o_ref[...] = _dot(s, w_ref[...].astype(BF16)) + b_ref[...]


def _modulation(c_all, w_ada, b_ada):
    n, d = c_all.shape
    e = w_ada.shape[1]
    te = 1536
    assert e % te == 0
    return pl.pallas_call(
        _mod_kernel,
        grid=(e // te,),
        in_specs=[pl.BlockSpec((n, d), lambda j: (0, 0)),
                  pl.BlockSpec((d, te), lambda j: (0, j)),
                  pl.BlockSpec((1, te), lambda j: (0, j))],
        out_specs=pl.BlockSpec((n, te), lambda j: (0, j)),
        out_shape=jax.ShapeDtypeStruct((n, e), F32),
        compiler_params=_params(("parallel",)),
        name="modulation",
    )(c_all, w_ada, b_ada.reshape(1, e))


def _bias_table_kernel(rb_ref, o_ref, *, offset):
    h = pl.program_id(0)
    _, rows, cols = o_ref.shape
    r = lax.broadcasted_iota(jnp.int32, (rows, cols), 0)
    kc = lax.broadcasted_iota(jnp.int32, (rows, cols), 1) - offset
    rel = kc - r
    n = jnp.abs(rel)
    large = jnp.full((rows, cols), N_BUCKETS // 4, jnp.int32)
    for thr in BUCKET_THRESHOLDS:
        large = large + (n >= thr).astype(jnp.int32)
    bucket = jnp.where(rel > 0, N_BUCKETS // 2, 0) + jnp.where(n < N_BUCKETS // 4, n, large)
    val = jnp.zeros((rows, cols), F32)
    for b in range(N_BUCKETS):
        val = jnp.where(bucket == b, rb_ref[b, h], val)
    val = (val - rb_ref[FAR_BUCKET, h]) * LOG2E
    shift = int(math.log2(CHUNK))
    visible = lax.shift_right_arithmetic(kc, shift) <= lax.shift_right_arithmetic(r, shift)
    o_ref[0] = jnp.where(visible, val, NEG_INF)


def _bias_table(rel_bias, rows, cols, offset):
    assert offset >= BUCKET_THRESHOLDS[-1] and offset % CHUNK == 0
    return pl.pallas_call(
        functools.partial(_bias_table_kernel, offset=offset),
        grid=(H_A,),
        in_specs=[pl.BlockSpec(memory_space=pltpu.SMEM)],
        out_specs=pl.BlockSpec((1, rows, cols), lambda h: (h, 0, 0)),
        out_shape=jax.ShapeDtypeStruct((H_A, rows, cols), F32),
        compiler_params=_params(("parallel",)),
        name="bias_table",
    )(rel_bias)


def _inproj_kernel(x_ref, mod_ref, g1_ref, w_ref, gq_ref, gk_ref, rope_ref, cr_ref, sr_ref,
                   kf_ref, vf_ref, qa_ref, ka_ref, va_ref, qr_ref, kr_ref, vr_ref, sg_ref):
    g, r, d = x_ref.shape
    m = g * r
    x = x_ref[...]
    gain = g1_ref[...] * (1.0 + mod_ref[:, 1:2, :])
    hmod = x * lax.rsqrt(jnp.mean(x * x, axis=-1, keepdims=True) + EPS) * gain + mod_ref[:, 0:1, :]
    hb = hmod.astype(BF16).reshape(m, d)

    def seg(j):
        return _dot(hb, w_ref[:, j * SEG_W:(j + 1) * SEG_W])

    def put(ref, val):
        ref[...] = val.reshape(g, r, SEG_W).astype(ref.dtype)

    gi = lax.broadcasted_iota(jnp.int32, (MXU_W, MXU_W), 0) // DK_A
    gj = lax.broadcasted_iota(jnp.int32, (MXU_W, MXU_W), 1) // DK_A
    ones_bd = (gi == gj).astype(BF16)

    def group_rms(z):
        sq = (z * z).astype(BF16)
        ms = jnp.concatenate([_dot(sq[:, c:c + MXU_W], ones_bd) for c in range(0, SEG_W, MXU_W)], axis=1)
        return z * lax.rsqrt(ms * (1.0 / DK_A) + EPS)

    cr, sr = cr_ref[...], sr_ref[...]

    def table(c_row, s_row, sign):
        t = rope_ref[0, c_row:c_row + 1, :] * cr + sign * (rope_ref[0, s_row:s_row + 1, :] * sr)
        return jnp.broadcast_to(t[None], (g, r, HEAD_W)).reshape(m, HEAD_W)

    cos, sine, sino = table(0, 1, -1.0), table(3, 2, 1.0), table(5, 4, 1.0)

    def rotate(z):
        outs = []
        for hh in range(H_R):
            zh = z[:, hh * HEAD_W:(hh + 1) * HEAD_W]
            nxt = pltpu.roll(zh, HEAD_W - 1, axis=1)
            prv = pltpu.roll(zh, 1, axis=1)
            outs.append(zh * cos + nxt * sine + prv * sino)
        return jnp.concatenate(outs, axis=1)

    put(qa_ref, group_rms(seg(0)) * gq_ref[...] * (DK_A ** -0.5 * LOG2E))
    ka = group_rms(seg(1)) * gk_ref[...]
    put(kf_ref, ka)
    put(ka_ref, ka)
    put(qr_ref, rotate(seg(3)))
    put(kr_ref, rotate(seg(4)) * (HEAD_W ** -0.5))
    put(sg_ref, _silu(seg(6)))
    va = seg(2)
    for hh in range(H_A):
        vf_ref[:, pl.ds(hh, r, stride=H_A), :] = va[:, hh * HEAD_W:(hh + 1) * HEAD_W].reshape(g, r, HEAD_W)
    put(va_ref, va)
    put(vr_ref, seg(5))


N_INPROJ_OUT = 9


def _tail_step_kernel(body, n_a, n_shared, n_out, *refs):
    a_in, b_in = refs[:n_a], refs[n_a:2 * n_a]
    shared = refs[2 * n_a:2 * n_a + n_shared]
    outs = refs[2 * n_a + n_shared:]
    a_out, b_out = outs[:n_out], outs[n_out:]
    last = pl.num_programs(0) - 1

    @pl.when(pl.program_id(0) < last)
    def _():
        body(a_in, shared, a_out)

    @pl.when(pl.program_id(0) == last)
    def _():
        body(b_in, shared, b_out)


def _inproj_body(stream, shared, outs):
    x_ref, mod_ref, rope_ref, cr_ref, sr_ref = stream
    g1_ref, w_ref, gq_ref, gk_ref = shared
    _inproj_kernel(x_ref, mod_ref, g1_ref, w_ref, gq_ref, gk_ref, rope_ref, cr_ref, sr_ref, *outs)


def _inproj(xp, mod_p, tab_p, r_blk, xs, mod_s, tab_s, g1, w_in, gq_t, gk_t):
    b, s, d = xp.shape
    bs, l, _ = xs.shape
    assert s % r_blk == 0
    nl = s // r_blk
    n = b * nl
    pj = lambda i: jnp.minimum(i, n - 1)
    tok_p = lambda w, rows=r_blk: pl.BlockSpec((1, rows, w), lambda i: (pj(i) // nl, pj(i) % nl, 0))
    whole = lambda shape: pl.BlockSpec(shape, lambda i: (0,) * len(shape))
    stream_p = [tok_p(d), pl.BlockSpec((1, 6, d), lambda i: (pj(i) // nl, 0, 0)),
                pl.BlockSpec((1, 8, HEAD_W), lambda i: (pj(i) % nl, 0, 0)),
                _resident((r_blk, HEAD_W)), _resident((r_blk, HEAD_W))]
    stream_s = [whole((bs, l, d)), whole((bs, 6, d)), whole((1, 8, HEAD_W)), whole((l, HEAD_W)), whole((l, HEAD_W))]
    shared = [_resident((1, d)), _resident(w_in.shape), _resident((1, SEG_W)), _resident((1, SEG_W))]
    out_p = [tok_p(SEG_W), tok_p(HEAD_W, r_blk * H_A)] + [tok_p(SEG_W)] * 7
    out_s = [whole((bs, l, SEG_W)), whole((bs, l * H_A, HEAD_W))] + [whole((bs, l, SEG_W))] * 7
    shapes = lambda nb, t: ([jax.ShapeDtypeStruct((nb, t, SEG_W), F32), jax.ShapeDtypeStruct((nb, t * H_A, HEAD_W), F32)]
                            + [jax.ShapeDtypeStruct((nb, t, SEG_W), BF16)] * 7)
    res = pl.pallas_call(
        functools.partial(_tail_step_kernel, _inproj_body, len(stream_p), len(shared), N_INPROJ_OUT),
        grid=(n + 1,),
        in_specs=stream_p + stream_s + shared,
        out_specs=out_p + out_s,
        out_shape=shapes(b, s) + shapes(bs, l),
        compiler_params=_params(("arbitrary",)),
        name="inproj",
    )(xp, mod_p, *tab_p, xs, mod_s, *tab_s, g1, w_in, gq_t, gk_t)
    return res[:N_INPROJ_OUT], res[N_INPROJ_OUT:]


def _stack_maps(q):
    lane = lax.broadcasted_iota(jnp.int32, q.shape, 1)
    zero = jnp.zeros_like(q)
    return jnp.concatenate([jnp.where(lane < DK_A, q, zero), jnp.where(lane >= DK_A, q, zero)], axis=0)


def _lambda(lq1, lk1, lq2, lk2, lam_init):
    s1 = jnp.sum(lq1[...] * lk1[...], axis=-1, keepdims=True)
    s2 = jnp.sum(lq2[...] * lk2[...], axis=-1, keepdims=True)
    return jnp.exp(s1) - jnp.exp(s2) + lam_init


def _diff_finish(acc, l, lam, gs, lam_init):
    t = acc.shape[0] // 2
    o = acc[:t] * (1.0 / l[:t]) - lam * (acc[t:] * (1.0 / l[t:]))
    on = o * lax.rsqrt(jnp.mean(o * o, axis=-1, keepdims=True) + EPS)
    return on * gs * (1.0 - lam_init)


def _attn_prompt_kernel(bound_ref, q_ref, k_ref, v_ref, tbl_ref, lq1, lk1, lq2, lk2, gs_ref, o_ref,
                        qq_sc, m_sc, l_sc, acc_sc, *, lam_init, tk, tkd, near, bounded):
    qi = pl.program_id(2)
    tq = q_ref.shape[1]
    q = q_ref[0]
    lane = lax.broadcasted_iota(jnp.int32, q.shape, 1)
    zero = jnp.zeros_like(q)
    qq_sc[0] = jnp.where(lane < DK_A, q, zero)
    qq_sc[1] = jnp.where(lane >= DK_A, q, zero)

    def tile(start, width, row0, corner, diag, bounded, first=False):
        nr = tq - row0
        nblk = width // HEAD_W
        rows = slice(row0, tq)
        flat = lambda x: x.reshape(2 * nr, x.shape[-1])
        kt = k_ref[0, pl.ds(start, width), :]
        vt = v_ref[0, pl.ds(start, width), :]
        s = _dot_nt(flat(qq_sc[:, rows, :]), kt)
        blocks = [s[:, c * HEAD_W:(c + 1) * HEAD_W] for c in range(nblk)]

        def add_rows(blk, r0, val):
            r1 = r0 + val.shape[0]
            cuts = [(0, r0, False), (r0, r1, True), (r1, nr + r0, False), (nr + r0, nr + r1, True),
                    (nr + r1, 2 * nr, False)]
            return jnp.concatenate([blk[a:b] + val if hit else blk[a:b] for a, b, hit in cuts if b > a], axis=0)

        if diag:
            blocks = [add_rows(blk, 0, tbl_ref[0, :, near + c * HEAD_W:near + (c + 1) * HEAD_W])
                      for c, blk in enumerate(blocks)]
        corner_row0 = width if diag else 0
        if corner_row0 < nr:
            blocks[-1] = add_rows(blocks[-1], corner_row0, corner)
        if bounded:
            ps = [jnp.exp2(blk - bound_ref[0]) for blk in blocks]
            l_new = functools.reduce(jnp.add, ps)
            p = jnp.concatenate([pc.astype(BF16) for pc in ps], axis=1)
            acc_new = _dot(p, vt)
            if not first:
                l_new = flat(l_sc[:, rows, :]) + l_new
                acc_new = flat(acc_sc[:, rows, :]) + acc_new
            l_sc[:, rows, :] = l_new.reshape(2, nr, HEAD_W)
            acc_sc[:, rows, :] = acc_new.reshape(2, nr, HEAD_W)
            return
        m_old = flat(m_sc[:, rows, :])
        mx = functools.reduce(jnp.maximum, blocks)
        m_new = jnp.maximum(m_old, jnp.max(mx, axis=-1, keepdims=True))
        alpha = jnp.exp2(m_old - m_new)
        ps = [jnp.exp2(blk - m_new) for blk in blocks]
        l_new = alpha * flat(l_sc[:, rows, :]) + functools.reduce(jnp.add, ps)
        p = jnp.concatenate([pc.astype(BF16) for pc in ps], axis=1)
        acc_new = alpha * flat(acc_sc[:, rows, :]) + _dot(p, vt)
        m_sc[:, rows, :] = m_new.reshape(2, nr, HEAD_W)
        l_sc[:, rows, :] = l_new.reshape(2, nr, HEAD_W)
        acc_sc[:, rows, :] = acc_new.reshape(2, nr, HEAD_W)

    n_far = qi * (tq // tk)
    corner_tbl = tbl_ref[0, :near, :near]

    def diag_tiles(bounded, first):
        for d in range(tq // tkd):
            tile(pl.multiple_of(qi * tq + d * tkd, tkd), tkd, d * tkd, corner_tbl, True, bounded, first and d == 0)

    def run(bounded):
        if bounded:
            diag_tiles(True, True)

        def far_tile(j):
            flag = (j == n_far - 1).astype(F32)
            tile(pl.multiple_of(j * tk, tk), tk, 0, corner_tbl * flag, False, bounded)

        rem = jnp.bitwise_and(n_far, 3)

        def peel(j, carry):
            far_tile(j)
            return carry

        lax.fori_loop(0, rem, peel, 0)

        def far_quad(i, carry):
            for u in range(4):
                far_tile(rem + 4 * i + u)
            return carry

        lax.fori_loop(0, lax.shift_right_logical(n_far, 2), far_quad, 0)
        if not bounded:
            diag_tiles(False, False)

    if not bounded:
        m_sc[...] = jnp.full(m_sc.shape, NEG_INF, F32)
        l_sc[...] = jnp.zeros(l_sc.shape, F32)
        acc_sc[...] = jnp.zeros(acc_sc.shape, F32)
    run(bounded)

    lam = _lambda(lq1, lk1, lq2, lk2, lam_init)
    l = jnp.sum(l_sc[...], axis=-1, keepdims=True).reshape(2 * tq, 1)
    acc = acc_sc[...].reshape(2 * tq, HEAD_W)
    o_ref[0] = _diff_finish(acc, l, lam, gs_ref[...], lam_init).astype(o_ref.dtype)


def _attn_prompt(bound, q, k, v, tbl, lams, gs, lam_init, tq, tk, tkd, near, bounded):
    b, s, _ = q.shape
    assert s % tq == 0 and tq % tk == 0 and tq % tkd == 0 and tbl.shape == (H_A, tkd, near + tkd)
    assert near == HEAD_W and tk % HEAD_W == 0 and tkd % HEAD_W == 0
    vec = _resident((1, DK_A))
    return pl.pallas_call(
        functools.partial(_attn_prompt_kernel, lam_init=lam_init, tk=tk, tkd=tkd, near=near, bounded=bounded),
        grid=(b, H_A, s // tq),
        in_specs=[pl.BlockSpec(memory_space=pltpu.SMEM),
                  pl.BlockSpec((1, tq, HEAD_W), lambda bi, h, qi: (bi, qi, h)),
                  pl.BlockSpec((1, s, HEAD_W), lambda bi, h, qi: (bi, 0, h)),
                  pl.BlockSpec((1, s, HEAD_W), lambda bi, h, qi: (bi, 0, h)),
                  pl.BlockSpec((1, tkd, near + tkd), lambda bi, h, qi: (h, 0, 0)),
                  vec, vec, vec, vec, _resident((1, HEAD_W))],
        out_specs=pl.BlockSpec((1, tq, HEAD_W), lambda bi, h, qi: (bi, qi, h)),
        out_shape=jax.ShapeDtypeStruct((b, s, SEG_W), BF16),
        scratch_shapes=[pltpu.VMEM((2, tq, HEAD_W), BF16)] + [pltpu.VMEM((2, tq, HEAD_W), F32)] * 3,
        compiler_params=_params(("parallel", "parallel", "arbitrary")),
        name="attn_prompt_bounded" if bounded else "attn_prompt_exact",
    )(bound, q, k, v, tbl, *lams, gs)


def _log_gamma(h):
    vals = [math.log(1.0 - 2.0 ** (-5.0 - i)) for i in range(H_R)]
    lg = jnp.float32(vals[H_R - 1])
    for i in range(H_R - 2, -1, -1):
        lg = jnp.where(h == i, jnp.float32(vals[i]), lg)
    return lg


def _retention_chunk(q, k, v, state, lg):
    c = q.shape[0]
    ti = lax.broadcasted_iota(jnp.int32, (c, c), 0)
    si = lax.broadcasted_iota(jnp.int32, (c, c), 1)
    dist = (ti - si).astype(F32)
    decay = jnp.where(dist >= 0, jnp.exp(jnp.maximum(dist, 0.0) * lg), 0.0)
    scores = _dot_nt(q, k) * decay
    row = lax.broadcasted_iota(jnp.int32, (c, 1), 0).astype(F32)
    q_dec = (q.astype(F32) * jnp.exp((row + 1.0) * lg)).astype(BF16)
    o = _dot(scores.astype(BF16), v) + _dot(q_dec, state.astype(BF16))
    k_dec = (k.astype(F32) * jnp.exp((c - 1.0 - row) * lg)).astype(BF16)
    new_state = jnp.exp(c * lg) * state + _dot_tn(k_dec, v)
    return o, new_state


def _retention_out(o, sg):
    on = o * lax.rsqrt(jnp.mean(o * o, axis=-1, keepdims=True) + EPS)
    return on * sg.astype(F32)


def _ret_prompt_kernel(q_ref, k_ref, v_ref, sg_ref, o_ref, st_ref, s_sc, d_sc, qd_sc, kd_sc, *, c):
    h = pl.program_id(1)
    ti = pl.program_id(2)
    tb = q_ref.shape[1]

    @pl.when(ti == 0)
    def _():
        lg = _log_gamma(h)
        s_sc[...] = jnp.zeros(s_sc.shape, F32)
        tt = lax.broadcasted_iota(jnp.int32, (c, c), 0)
        ss = lax.broadcasted_iota(jnp.int32, (c, c), 1)
        dist = (tt - ss).astype(F32)
        d_sc[...] = jnp.where(dist >= 0, jnp.exp(jnp.maximum(dist, 0.0) * lg), 0.0)
        row = lax.broadcasted_iota(jnp.int32, (c, HEAD_W), 0).astype(F32)
        qd_sc[...] = jnp.exp((row + 1.0) * lg)
        kd_sc[...] = jnp.exp((c - 1.0 - row) * lg)

    state = s_sc[...]
    g_c = qd_sc[c - 1:c, :]
    for i in range(tb // c):
        rows = slice(i * c, (i + 1) * c)
        q, k, v = q_ref[0, rows, :], k_ref[0, rows, :], v_ref[0, rows, :]
        scores = _dot_nt(q, k) * d_sc[...]
        q_dec = (q.astype(F32) * qd_sc[...]).astype(BF16)
        o = _dot(scores.astype(BF16), v) + _dot(q_dec, state.astype(BF16))
        k_dec = (k.astype(F32) * kd_sc[...]).astype(BF16)
        state = g_c * state + _dot_tn(k_dec, v)
        o_ref[0, rows, :] = _retention_out(o, sg_ref[0, rows, :]).astype(o_ref.dtype)
    s_sc[...] = state

    @pl.when(ti == pl.num_programs(2) - 1)
    def _():
        st_ref[0, 0] = state


def _ret_prompt(qr, kr, vr, sg, t_blk, c):
    b, s, _ = qr.shape
    assert s % t_blk == 0 and t_blk % c == 0
    blk = pl.BlockSpec((1, t_blk, HEAD_W), lambda bi, h, ti: (bi, ti, h))
    return pl.pallas_call(
        functools.partial(_ret_prompt_kernel, c=c),
        grid=(b, H_R, s // t_blk),
        in_specs=[blk, blk, blk, blk],
        out_specs=[blk, pl.BlockSpec((1, 1, HEAD_W, HEAD_W), lambda bi, h, ti: (bi, h, 0, 0))],
        out_shape=[jax.ShapeDtypeStruct((b, s, SEG_W), BF16),
                   jax.ShapeDtypeStruct((b, H_R, HEAD_W, HEAD_W), F32)],
        scratch_shapes=[pltpu.VMEM((HEAD_W, HEAD_W), F32), pltpu.VMEM((c, c), F32),
                        pltpu.VMEM((c, HEAD_W), F32), pltpu.VMEM((c, HEAD_W), F32)],
        compiler_params=_params(("parallel", "parallel", "arbitrary")),
        name="ret_prompt",
    )(qr, kr, vr, sg)


def _mix_sample_kernel(qa_ref, ka_ref, va_ref, ck_ref, cv_ref, tbl_ref, lq1, lk1, lq2, lk2, gs_ref,
                       qr_ref, kr_ref, vr_ref, sg_ref, st_ref, oa_ref, or_ref, so_ref, *, lam_init, near):
    p_len = ck_ref.shape[2]
    l = qa_ref.shape[1]
    lam = _lambda(lq1, lk1, lq2, lk2, lam_init)
    for h in range(H_A):
        cols = slice(h * HEAD_W, (h + 1) * HEAD_W)
        qq = _stack_maps(qa_ref[0, :, cols])
        kc = ck_ref[0, cols, :].astype(BF16)
        vc = cv_ref[0, pl.ds(h, p_len, stride=H_A), :].astype(BF16)
        bias = tbl_ref[h]
        bias2 = jnp.concatenate([bias, bias], axis=0)
        s_c = _dot(qq, kc)
        s_c = jnp.concatenate([s_c[:, :p_len - near], s_c[:, p_len - near:] + bias2[:, :near]], axis=1)
        s_n = _dot_nt(qq, ka_ref[0, :, cols]) + bias2[:, near:]
        m = jnp.maximum(jnp.max(s_c, axis=-1, keepdims=True), jnp.max(s_n, axis=-1, keepdims=True))
        p_c = jnp.exp2(s_c - m)
        p_n = jnp.exp2(s_n - m)
        lsum = jnp.sum(p_c, axis=-1, keepdims=True) + jnp.sum(p_n, axis=-1, keepdims=True)
        acc = _dot(p_c.astype(BF16), vc) + _dot(p_n.astype(BF16), va_ref[0, :, cols])
        oa_ref[0, :, cols] = _diff_finish(acc, lsum, lam, gs_ref[...], lam_init).astype(oa_ref.dtype)
    for h in range(H_R):
        cols = slice(h * HEAD_W, (h + 1) * HEAD_W)
        lg = jnp.float32(math.log(1.0 - 2.0 ** (-5.0 - h)))
        o, new_state = _retention_chunk(qr_ref[0, :, cols], kr_ref[0, :, cols], vr_ref[0, :, cols],
                                        st_ref[0, h], lg)
        or_ref[0, :, cols] = _retention_out(o, sg_ref[0, :, cols]).astype(or_ref.dtype)
        so_ref[0, h] = new_state


def _mix_sample(qa, ka, va, cache_k, cache_v, tbl, lams, gs, qr, kr, vr, sg, state, lam_init, near):
    b, l, _ = qa.shape
    p_len = cache_k.shape[2]
    assert p_len % CHUNK == 0 and l <= CHUNK and p_len >= near
    tok = pl.BlockSpec((1, l, SEG_W), lambda bi: (bi, 0, 0))
    cache_kt = pl.BlockSpec((1, SEG_W, p_len), lambda bi: (bi, 0, 0))
    cache_vh = pl.BlockSpec((1, p_len * H_A, HEAD_W), lambda bi: (bi, 0, 0))
    st = pl.BlockSpec((1, H_R, HEAD_W, HEAD_W), lambda bi: (bi, 0, 0, 0))
    vec = _resident((1, DK_A))
    return pl.pallas_call(
        functools.partial(_mix_sample_kernel, lam_init=lam_init, near=near),
        grid=(b,),
        in_specs=[tok, tok, tok, cache_kt, cache_vh, _resident(tbl.shape), vec, vec, vec, vec,
                  _resident((1, HEAD_W)), tok, tok, tok, tok, st],
        out_specs=[tok, tok, st],
        out_shape=[jax.ShapeDtypeStruct((b, l, SEG_W), BF16), jax.ShapeDtypeStruct((b, l, SEG_W), BF16),
                   jax.ShapeDtypeStruct((b, H_R, HEAD_W, HEAD_W), F32)],
        compiler_params=_params(("parallel",)),
        name="mix_sample",
    )(qa, ka, va, cache_k, cache_v, tbl, *lams, gs, qr, kr, vr, sg, state)


def _ffn_kernel(x_ref, oa_ref, or_ref, mod_ref, g2_ref, wo_ref, wg_ref, wu_ref, wd_ref, y_ref, *, ff_blk):
    g, r, d = x_ref.shape
    m = g * r
    f = wg_ref.shape[1]
    proj = (_dot(oa_ref[...].reshape(m, SEG_W), wo_ref[:SEG_W, :])
            + _dot(or_ref[...].reshape(m, SEG_W), wo_ref[SEG_W:, :]))
    x1 = x_ref[...] + mod_ref[:, 2:3, :] * proj.reshape(g, r, d)
    gain = g2_ref[...] * (1.0 + mod_ref[:, 4:5, :])
    y = x1 * lax.rsqrt(jnp.mean(x1 * x1, axis=-1, keepdims=True) + EPS) * gain
    h2 = (y + mod_ref[:, 3:4, :]).astype(BF16).reshape(m, d)
    acc = jnp.zeros((m, d), F32)
    for c0 in range(0, f, ff_blk):
        c1 = min(c0 + ff_blk, f)
        ff = (_silu(_dot(h2, wg_ref[:, c0:c1])) * _dot(h2, wu_ref[:, c0:c1])).astype(BF16)
        acc = acc + _dot(ff, wd_ref[c0:c1, :])
    y_ref[...] = x1 + mod_ref[:, 5:6, :] * acc.reshape(g, r, d)


def _ffn_body(ff_blk, stream, shared, outs):
    _ffn_kernel(*stream, *shared, *outs, ff_blk=ff_blk)


def _ffn(xp, oa_p, or_p, mod_p, r_blk, xs, oa_s, or_s, mod_s, g2, wo, wg, wu, wd, ff_blk):
    b, s, d = xp.shape
    bs, l, _ = xs.shape
    assert s % r_blk == 0
    nl = s // r_blk
    n = b * nl
    pj = lambda i: jnp.minimum(i, n - 1)
    tok_p = lambda w: pl.BlockSpec((1, r_blk, w), lambda i: (pj(i) // nl, pj(i) % nl, 0))
    whole = lambda shape: pl.BlockSpec(shape, lambda i: (0,) * len(shape))
    stream_p = [tok_p(d), tok_p(SEG_W), tok_p(SEG_W), pl.BlockSpec((1, 6, d), lambda i: (pj(i) // nl, 0, 0))]
    stream_s = [whole((bs, l, d)), whole((bs, l, SEG_W)), whole((bs, l, SEG_W)), whole((bs, 6, d))]
    shared = [_resident((1, d)), _resident(wo.shape), _resident(wg.shape), _resident(wu.shape), _resident(wd.shape)]
    return pl.pallas_call(
        functools.partial(_tail_step_kernel, functools.partial(_ffn_body, ff_blk), len(stream_p), len(shared), 1),
        grid=(n + 1,),
        in_specs=stream_p + stream_s + shared,
        out_specs=[tok_p(d), whole((bs, l, d))],
        out_shape=[jax.ShapeDtypeStruct((b, s, d), F32), jax.ShapeDtypeStruct((bs, l, d), F32)],
        compiler_params=_params(("arbitrary",)),
        name="outproj_ffn",
    )(xp, oa_p, or_p, mod_p, xs, oa_s, or_s, mod_s, g2, wo, wg, wu, wd)


def _rope_tables(first_pos, n_blocks, blk):
    inv = 1.0 / (ROPE_BASE ** jnp.linspace(0.0, 1.0, HEAD_W // 2, dtype=F32))
    inv2 = jnp.repeat(inv, 2)[None, :]
    even = (jnp.arange(HEAD_W) % 2 == 0)[None, :]
    me, mo = jnp.where(even, -1.0, 0.0), jnp.where(even, 0.0, 1.0)
    t0 = (first_pos + blk * jnp.arange(n_blocks, dtype=jnp.int32)).astype(F32)[:, None] * inv2
    cb, sb = jnp.cos(t0), jnp.sin(t0)
    zero = jnp.zeros_like(cb)
    base = jnp.stack([cb, sb, cb * me, sb * me, cb * mo, sb * mo, zero, zero], axis=1)
    off = jnp.arange(blk, dtype=jnp.int32).astype(F32)[:, None] * inv2
    return base, jnp.cos(off), jnp.sin(off)


def _logit_bound(g_q, g_k, rel_bias):
    slack = 1.02
    dot_max = slack * DK_A * (DK_A ** -0.5 * LOG2E) * jnp.max(jnp.abs(g_q)) * jnp.max(jnp.abs(g_k))
    rb = (rel_bias - rel_bias[FAR_BUCKET][None, :]) * LOG2E
    b_hi = jnp.maximum(jnp.max(rb), 0.0)
    b_lo = jnp.minimum(jnp.min(rb), 0.0)
    ok = (2.0 * dot_max + (b_hi - b_lo)) <= MAX_LOGIT_SPREAD
    return jnp.stack([dot_max + b_hi, ok.astype(F32)])


def _tile_rows(n, target):
    t = min(n, target)
    while n % t:
        t //= 2
    return t


def kernel(x_prompt, x_sample, c_prompt, c_sample, cache_k, cache_v, state_ret, w_ada, b_ada, g_norm1, g_norm2, w_in, g_q, g_k, lam_q1, lam_k1, lam_q2, lam_k2, g_subln, w_out, w_ff_gate, w_ff_up, w_ff_down, rel_bias):
    depth = w_ada.shape[0]
    b, s, d = x_prompt.shape
    bs, l, _ = x_sample.shape
    p_len = cache_k.shape[2]
    assert w_in.shape[2] == N_SEG * SEG_W and d == 2 * SEG_W

    tm = _tile_rows(s, 512)
    tm_in = _tile_rows(s, 1024)
    tq = _tile_rows(s, 4096)
    tk = _tile_rows(tq, 256)
    tkd = _tile_rows(tq, 256)
    ret_blk = _tile_rows(s, 8192)
    ret_c = _tile_rows(ret_blk, 256)
    near = HEAD_W
    assert tq % CHUNK == 0 and tk % CHUNK == 0 and tkd % CHUNK == 0 and near >= BUCKET_THRESHOLDS[-1]

    tab_p = _rope_tables(0, s // tm_in, tm_in)
    tab_s = _rope_tables(p_len, 1, l)
    tbl_p = _bias_table(rel_bias, tkd, near + tkd, near)
    tbl_s = _bias_table(rel_bias, l, near + l, near)
    c_all = jnp.concatenate([c_prompt, c_sample], axis=0)

    xp, xs = x_prompt, x_sample
    outs = [[] for _ in range(6)]
    for layer in range(depth):
        lam_init = _lambda_init(layer)
        mod = _modulation(c_all, w_ada[layer], b_ada[layer]).reshape(b + bs, 6, d)
        mod_p, mod_s = mod[:b], mod[b:]
        g1 = g_norm1[layer].reshape(1, d)
        g2 = g_norm2[layer].reshape(1, d)
        gq_t = jnp.tile(g_q[layer], SEG_W // DK_A).reshape(1, SEG_W)
        gk_t = jnp.tile(g_k[layer], SEG_W // DK_A).reshape(1, SEG_W)
        lams = [v[layer].reshape(1, DK_A) for v in (lam_q1, lam_k1, lam_q2, lam_k2)]
        gs = g_subln[layer].reshape(1, HEAD_W)
        wi = w_in[layer].astype(BF16)
        wo = w_out[layer].astype(BF16)
        wg = w_ff_gate[layer].astype(BF16)
        wu = w_ff_up[layer].astype(BF16)
        wd = w_ff_down[layer].astype(BF16)
        ff_blk = 1024

        (kf, vf, qa, ka, va, qr, kr, vr, sg), (kfs, vfs, qas, kas, vas, qrs, krs, vrs, sgs) = _inproj(
            xp, mod_p, tab_p, tm_in, xs, mod_s, tab_s, g1, wi, gq_t, gk_t)
        bound = _logit_bound(g_q[layer], g_k[layer], rel_bias)
        attn = lambda bounded: functools.partial(_attn_prompt, lam_init=lam_init, tq=tq, tk=tk, tkd=tkd,
                                                 near=near, bounded=bounded)
        oa = lax.cond(bound[1] > 0.5, attn(True), attn(False), bound, qa, ka, va, tbl_p, lams, gs)
        orr, st_p = _ret_prompt(qr, kr, vr, sg, ret_blk, ret_c)
        ck = jnp.swapaxes(cache_k[layer].reshape(bs, p_len, SEG_W), 1, 2)
        cv = cache_v[layer].reshape(bs, p_len * H_A, HEAD_W)
        oas, ors, st_s = _mix_sample(qas, kas, vas, ck, cv, tbl_s, lams, gs, qrs, krs, vrs, sgs,
                                     state_ret[layer], lam_init, near)
        xp, xs = _ffn(xp, oa, orr, mod_p, tm, xs, oas, ors, mod_s, g2, wo, wg, wu, wd, ff_blk)

        for lst, val in zip(outs, (kf.reshape(b, s, H_A, 2, DK_A), vf.reshape(b, s, H_A, HEAD_W), st_p,
                                   kfs.reshape(bs, l, H_A, 2, DK_A), vfs.reshape(bs, l, H_A, HEAD_W), st_s)):
            lst.append(val)
    return (xp, xs) + tuple(jnp.stack(o) for o in outs)
```

```python
import functools
import math

import jax
import jax.numpy as jnp
from jax import lax
from jax.experimental import pallas as pl
from jax.experimental.pallas import tpu as pltpu

F32 = jnp.float32
BF16 = jnp.bfloat16

CHUNK = 64
H_A = 4
DK_A = 64
H_R = 4
HEAD_W = 128
SEG_W = H_A * HEAD_W
N_SEG = 7
N_BUCKETS = 32
ROPE_BASE = 10000.0
EPS = 1e-6
NEG_INF = -1e30
LOG2E = math.log2(math.e)
BUCKET_THRESHOLDS = (12, 16, 23, 32, 46, 64, 91)
FAR_BUCKET = 15
MAX_LOGIT_SPREAD = 100.0

MXU_W = 256
VMEM_LIMIT = 56 * 1024 * 1024


def _lambda_init(layer):
    return 0.8 - 0.6 * math.exp(-0.3 * layer)


def _params(sem):
    return pltpu.CompilerParams(dimension_semantics=sem, vmem_limit_bytes=VMEM_LIMIT)


def _resident(shape):
    return pl.BlockSpec(shape, lambda *_: (0,) * len(shape), pipeline_mode=pl.Buffered(1))


def _dot(a, b):
    return jnp.dot(a, b, preferred_element_type=F32)


def _dot_nt(a, b):
    return lax.dot_general(a, b, (((1,), (1,)), ((), ())), preferred_element_type=F32)


def _dot_tn(a, b):
    return lax.dot_general(a, b, (((0,), (0,)), ((), ())), preferred_element_type=F32)


def _silu(x):
    return x * jax.nn.sigmoid(x)


def _mod_kernel(c_ref, w_ref, b_ref, o_ref):
    s = _silu(c_ref[...]).astype(BF16)
    o_ref[...] = _dot(s, w_ref[...].astype(BF16)) + b_ref[...]


def _modulation(c_all, w_ada, b_ada):
    n, d = c_all.shape
    e = w_ada.shape[1]
    te = 1536
    assert e % te == 0
    return pl.pallas_call(
        _mod_kernel,
        grid=(e // te,),
        in_specs=[pl.BlockSpec((n, d), lambda j: (0, 0)),
                  pl.BlockSpec((d, te), lambda j: (0, j)),
                  pl.BlockSpec((1, te), lambda j: (0, j))],
        out_specs=pl.BlockSpec((n, te), lambda j: (0, j)),
        out_shape=jax.ShapeDtypeStruct((n, e), F32),
        compiler_params=_params(("parallel",)),
        name="modulation",
    )(c_all, w_ada, b_ada.reshape(1, e))


def _bias_table_kernel(rb_ref, o_ref, *, offset):
    h = pl.program_id(0)
    _, rows, cols = o_ref.shape
    r = lax.broadcasted_iota(jnp.int32, (rows, cols), 0)
    kc = lax.broadcasted_iota(jnp.int32, (rows, cols), 1) - offset
    rel = kc - r
    n = jnp.abs(rel)
    large = jnp.full((rows, cols), N_BUCKETS // 4, jnp.int32)
    for thr in BUCKET_THRESHOLDS:
        large = large + (n >= thr).astype(jnp.int32)
    bucket = jnp.where(rel > 0, N_BUCKETS // 2, 0) + jnp.where(n < N_BUCKETS // 4, n, large)
    val = jnp.zeros((rows, cols), F32)
    for b in range(N_BUCKETS):
        val = jnp.where(bucket == b, rb_ref[b, h], val)
    val = (val - rb_ref[FAR_BUCKET, h]) * LOG2E
    shift = int(math.log2(CHUNK))
    visible = lax.shift_right_arithmetic(kc, shift) <= lax.shift_right_arithmetic(r, shift)
    o_ref[0] = jnp.where(visible, val, NEG_INF)


def _bias_table(rel_bias, rows, cols, offset):
    assert offset >= BUCKET_THRESHOLDS[-1] and offset % CHUNK == 0
    return pl.pallas_call(
        functools.partial(_bias_table_kernel, offset=offset),
        grid=(H_A,),
        in_specs=[pl.BlockSpec(memory_space=pltpu.SMEM)],
        out_specs=pl.BlockSpec((1, rows, cols), lambda h: (h, 0, 0)),
        out_shape=jax.ShapeDtypeStruct((H_A, rows, cols), F32),
        compiler_params=_params(("parallel",)),
        name="bias_table",
    )(rel_bias)


def _inproj_kernel(x_ref, mod_ref, g1_ref, w_ref, gq_ref, gk_ref, rope_ref, cr_ref, sr_ref,
                   kf_ref, vf_ref, qa_ref, ka_ref, va_ref, qr_ref, kr_ref, vr_ref, sg_ref):
    g, r, d = x_ref.shape
    m = g * r
    x = x_ref[...]
    gain = g1_ref[...] * (1.0 + mod_ref[:, 1:2, :])
    hmod = x * lax.rsqrt(jnp.mean(x * x, axis=-1, keepdims=True) + EPS) * gain + mod_ref[:, 0:1, :]
    hb = hmod.astype(BF16).reshape(m, d)

    def seg(j):
        return _dot(hb, w_ref[:, j * SEG_W:(j + 1) * SEG_W])

    def put(ref, val):
        ref[...] = val.reshape(g, r, SEG_W).astype(ref.dtype)

    gi = lax.broadcasted_iota(jnp.int32, (MXU_W, MXU_W), 0) // DK_A
    gj = lax.broadcasted_iota(jnp.int32, (MXU_W, MXU_W), 1) // DK_A
    ones_bd = (gi == gj).astype(BF16)

    def group_rms(z):
        sq = (z * z).astype(BF16)
        ms = jnp.concatenate([_dot(sq[:, c:c + MXU_W], ones_bd) for c in range(0, SEG_W, MXU_W)], axis=1)
        return z * lax.rsqrt(ms * (1.0 / DK_A) + EPS)

    cr, sr = cr_ref[...], sr_ref[...]

    def table(c_row, s_row, sign):
        t = rope_ref[0, c_row:c_row + 1, :] * cr + sign * (rope_ref[0, s_row:s_row + 1, :] * sr)
        return jnp.broadcast_to(t[None], (g, r, HEAD_W)).reshape(m, HEAD_W)

    cos, sine, sino = table(0, 1, -1.0), table(3, 2, 1.0), table(5, 4, 1.0)

    def rotate(z):
        outs = []
        for hh in range(H_R):
            zh = z[:, hh * HEAD_W:(hh + 1) * HEAD_W]
            nxt = pltpu.roll(zh, HEAD_W - 1, axis=1)
            prv = pltpu.roll(zh, 1, axis=1)
            outs.append(zh * cos + nxt * sine + prv * sino)
        return jnp.concatenate(outs, axis=1)

    put(qa_ref, group_rms(seg(0)) * gq_ref[...] * (DK_A ** -0.5 * LOG2E))
    ka = group_rms(seg(1)) * gk_ref[...]
    put(kf_ref, ka)
    put(ka_ref, ka)
    put(qr_ref, rotate(seg(3)))
    put(kr_ref, rotate(seg(4)) * (HEAD_W ** -0.5))
    put(sg_ref, _silu(seg(6)))
    va = seg(2)
    for hh in range(H_A):
        vf_ref[:, pl.ds(hh, r, stride=H_A), :] = va[:, hh * HEAD_W:(hh + 1) * HEAD_W].reshape(g, r, HEAD_W)
    put(va_ref, va)
    put(vr_ref, seg(5))


N_INPROJ_OUT = 9


def _tail_step_kernel(body, n_a, n_shared, n_out, *refs):
    a_in, b_in = refs[:n_a], refs[n_a:2 * n_a]
    shared = refs[2 * n_a:2 * n_a + n_shared]
    outs = refs[2 * n_a + n_shared:]
    a_out, b_out = outs[:n_out], outs[n_out:]
    last = pl.num_programs(0) - 1

    @pl.when(pl.program_id(0) < last)
    def _():
        body(a_in, shared, a_out)

    @pl.when(pl.program_id(0) == last)
    def _():
        body(b_in, shared, b_out)


def _inproj_body(stream, shared, outs):
    x_ref, mod_ref, rope_ref, cr_ref, sr_ref = stream
    g1_ref, w_ref, gq_ref, gk_ref = shared
    _inproj_kernel(x_ref, mod_ref, g1_ref, w_ref, gq_ref, gk_ref, rope_ref, cr_ref, sr_ref, *outs)


def _inproj(xp, mod_p, tab_p, r_blk, xs, mod_s, tab_s, g1, w_in, gq_t, gk_t):
    b, s, d = xp.shape
    bs, l, _ = xs.shape
    assert s % r_blk == 0
    nl = s // r_blk
    n = b * nl
    pj = lambda i: jnp.minimum(i, n - 1)
    tok_p = lambda w, rows=r_blk: pl.BlockSpec((1, rows, w), lambda i: (pj(i) // nl, pj(i) % nl, 0))
    whole = lambda shape: pl.BlockSpec(shape, lambda i: (0,) * len(shape))
    stream_p = [tok_p(d), pl.BlockSpec((1, 6, d), lambda i: (pj(i) // nl, 0, 0)),
                pl.BlockSpec((1, 8, HEAD_W), lambda i: (pj(i) % nl, 0, 0)),
                _resident((r_blk, HEAD_W)), _resident((r_blk, HEAD_W))]
    stream_s = [whole((bs, l, d)), whole((bs, 6, d)), whole((1, 8, HEAD_W)), whole((l, HEAD_W)), whole((l, HEAD_W))]
    shared = [_resident((1, d)), _resident(w_in.shape), _resident((1, SEG_W)), _resident((1, SEG_W))]
    out_p = [tok_p(SEG_W), tok_p(HEAD_W, r_blk * H_A)] + [tok_p(SEG_W)] * 7
    out_s = [whole((bs, l, SEG_W)), whole((bs, l * H_A, HEAD_W))] + [whole((bs, l, SEG_W))] * 7
    shapes = lambda nb, t: ([jax.ShapeDtypeStruct((nb, t, SEG_W), F32), jax.ShapeDtypeStruct((nb, t * H_A, HEAD_W), F32)]
                            + [jax.ShapeDtypeStruct((nb, t, SEG_W), BF16)] * 7)
    res = pl.pallas_call(
        functools.partial(_tail_step_kernel, _inproj_body, len(stream_p), len(shared), N_INPROJ_OUT),
        grid=(n + 1,),
        in_specs=stream_p + stream_s + shared,
        out_specs=out_p + out_s,
        out_shape=shapes(b, s) + shapes(bs, l),
        compiler_params=_params(("arbitrary",)),
        name="inproj",
    )(xp, mod_p, *tab_p, xs, mod_s, *tab_s, g1, w_in, gq_t, gk_t)
    return res[:N_INPROJ_OUT], res[N_INPROJ_OUT:]


def _stack_maps(q):
    lane = lax.broadcasted_iota(jnp.int32, q.shape, 1)
    zero = jnp.zeros_like(q)
    return jnp.concatenate([jnp.where(lane < DK_A, q, zero), jnp.where(lane >= DK_A, q, zero)], axis=0)


def _lambda(lq1, lk1, lq2, lk2, lam_init):
    s1 = jnp.sum(lq1[...] * lk1[...], axis=-1, keepdims=True)
    s2 = jnp.sum(lq2[...] * lk2[...], axis=-1, keepdims=True)
    return jnp.exp(s1) - jnp.exp(s2) + lam_init


def _diff_finish(acc, l, lam, gs, lam_init):
    t = acc.shape[0] // 2
    o = acc[:t] * (1.0 / l[:t]) - lam * (acc[t:] * (1.0 / l[t:]))
    on = o * lax.rsqrt(jnp.mean(o * o, axis=-1, keepdims=True) + EPS)
    return on * gs * (1.0 - lam_init)


def _attn_prompt_kernel(bound_ref, q_ref, k_ref, v_ref, tbl_ref, lq1, lk1, lq2, lk2, gs_ref, o_ref,
                        qq_sc, m_sc, l_sc, acc_sc, *, lam_init, tk, tkd, near, bounded):
    qi = pl.program_id(2)
    tq = q_ref.shape[1]
    q = q_ref[0]
    lane = lax.broadcasted_iota(jnp.int32, q.shape, 1)
    zero = jnp.zeros_like(q)
    qq_sc[0] = jnp.where(lane < DK_A, q, zero)
    qq_sc[1] = jnp.where(lane >= DK_A, q, zero)

    def tile(start, width, row0, corner, diag, bounded, first=False):
        nr = tq - row0
        nblk = width // HEAD_W
        rows = slice(row0, tq)
        flat = lambda x: x.reshape(2 * nr, x.shape[-1])
        kt = k_ref[0, pl.ds(start, width), :]
        vt = v_ref[0, pl.ds(start, width), :]
        s = _dot_nt(flat(qq_sc[:, rows, :]), kt)
        blocks = [s[:, c * HEAD_W:(c + 1) * HEAD_W] for c in range(nblk)]

        def add_rows(blk, r0, val):
            r1 = r0 + val.shape[0]
            cuts = [(0, r0, False), (r0, r1, True), (r1, nr + r0, False), (nr + r0, nr + r1, True),
                    (nr + r1, 2 * nr, False)]
            return jnp.concatenate([blk[a:b] + val if hit else blk[a:b] for a, b, hit in cuts if b > a], axis=0)

        if diag:
            blocks = [add_rows(blk, 0, tbl_ref[0, :, near + c * HEAD_W:near + (c + 1) * HEAD_W])
                      for c, blk in enumerate(blocks)]
        corner_row0 = width if diag else 0
        if corner_row0 < nr:
            blocks[-1] = add_rows(blocks[-1], corner_row0, corner)
        if bounded:
            ps = [jnp.exp2(blk - bound_ref[0]) for blk in blocks]
            l_new = functools.reduce(jnp.add, ps)
            p = jnp.concatenate([pc.astype(BF16) for pc in ps], axis=1)
            acc_new = _dot(p, vt)
            if not first:
                l_new = flat(l_sc[:, rows, :]) + l_new
                acc_new = flat(acc_sc[:, rows, :]) + acc_new
            l_sc[:, rows, :] = l_new.reshape(2, nr, HEAD_W)
            acc_sc[:, rows, :] = acc_new.reshape(2, nr, HEAD_W)
            return
        m_old = flat(m_sc[:, rows, :])
        mx = functools.reduce(jnp.maximum, blocks)
        m_new = jnp.maximum(m_old, jnp.max(mx, axis=-1, keepdims=True))
        alpha = jnp.exp2(m_old - m_new)
        ps = [jnp.exp2(blk - m_new) for blk in blocks]
        l_new = alpha * flat(l_sc[:, rows, :]) + functools.reduce(jnp.add, ps)
        p = jnp.concatenate([pc.astype(BF16) for pc in ps], axis=1)
        acc_new = alpha * flat(acc_sc[:, rows, :]) + _dot(p, vt)
        m_sc[:, rows, :] = m_new.reshape(2, nr, HEAD_W)
        l_sc[:, rows, :] = l_new.reshape(2, nr, HEAD_W)
        acc_sc[:, rows, :] = acc_new.reshape(2, nr, HEAD_W)

    n_far = qi * (tq // tk)
    corner_tbl = tbl_ref[0, :near, :near]

    def diag_tiles(bounded, first):
        for d in range(tq // tkd):
            tile(pl.multiple_of(qi * tq + d * tkd, tkd), tkd, d * tkd, corner_tbl, True, bounded, first and d == 0)

    def run(bounded):
        if bounded:
            diag_tiles(True, True)

        def far_tile(j):
            flag = (j == n_far - 1).astype(F32)
            tile(pl.multiple_of(j * tk, tk), tk, 0, corner_tbl * flag, False, bounded)

        rem = jnp.bitwise_and(n_far, 3)

        def peel(j, carry):
            far_tile(j)
            return carry

        lax.fori_loop(0, rem, peel, 0)

        def far_quad(i, carry):
            for u in range(4):
                far_tile(rem + 4 * i + u)
            return carry

        lax.fori_loop(0, lax.shift_right_logical(n_far, 2), far_quad, 0)
        if not bounded:
            diag_tiles(False, False)

    if not bounded:
        m_sc[...] = jnp.full(m_sc.shape, NEG_INF, F32)
        l_sc[...] = jnp.zeros(l_sc.shape, F32)
        acc_sc[...] = jnp.zeros(acc_sc.shape, F32)
    run(bounded)

    lam = _lambda(lq1, lk1, lq2, lk2, lam_init)
    l = jnp.sum(l_sc[...], axis=-1, keepdims=True).reshape(2 * tq, 1)
    acc = acc_sc[...].reshape(2 * tq, HEAD_W)
    o_ref[0] = _diff_finish(acc, l, lam, gs_ref[...], lam_init).astype(o_ref.dtype)


def _attn_prompt(bound, q, k, v, tbl, lams, gs, lam_init, tq, tk, tkd, near, bounded):
    b, s, _ = q.shape
    assert s % tq == 0 and tq % tk == 0 and tq % tkd == 0 and tbl.shape == (H_A, tkd, near + tkd)
    assert near == HEAD_W and tk % HEAD_W == 0 and tkd % HEAD_W == 0
    vec = _resident((1, DK_A))
    return pl.pallas_call(
        functools.partial(_attn_prompt_kernel, lam_init=lam_init, tk=tk, tkd=tkd, near=near, bounded=bounded),
        grid=(b, H_A, s // tq),
        in_specs=[pl.BlockSpec(memory_space=pltpu.SMEM),
                  pl.BlockSpec((1, tq, HEAD_W), lambda bi, h, qi: (bi, qi, h)),
                  pl.BlockSpec((1, s, HEAD_W), lambda bi, h, qi: (bi, 0, h)),
                  pl.BlockSpec((1, s, HEAD_W), lambda bi, h, qi: (bi, 0, h)),
                  pl.BlockSpec((1, tkd, near + tkd), lambda bi, h, qi: (h, 0, 0)),
                  vec, vec, vec, vec, _resident((1, HEAD_W))],
        out_specs=pl.BlockSpec((1, tq, HEAD_W), lambda bi, h, qi: (bi, qi, h)),
        out_shape=jax.ShapeDtypeStruct((b, s, SEG_W), BF16),
        scratch_shapes=[pltpu.VMEM((2, tq, HEAD_W), BF16)] + [pltpu.VMEM((2, tq, HEAD_W), F32)] * 3,
        compiler_params=_params(("parallel", "parallel", "arbitrary")),
        name="attn_prompt_bounded" if bounded else "attn_prompt_exact",
    )(bound, q, k, v, tbl, *lams, gs)


def _log_gamma(h):
    vals = [math.log(1.0 - 2.0 ** (-5.0 - i)) for i in range(H_R)]
    lg = jnp.float32(vals[H_R - 1])
    for i in range(H_R - 2, -1, -1):
        lg = jnp.where(h == i, jnp.float32(vals[i]), lg)
    return lg


def _retention_chunk(q, k, v, state, lg):
    c = q.shape[0]
    ti = lax.broadcasted_iota(jnp.int32, (c, c), 0)
    si = lax.broadcasted_iota(jnp.int32, (c, c), 1)
    dist = (ti - si).astype(F32)
    decay = jnp.where(dist >= 0, jnp.exp(jnp.maximum(dist, 0.0) * lg), 0.0)
    scores = _dot_nt(q, k) * decay
    row = lax.broadcasted_iota(jnp.int32, (c, 1), 0).astype(F32)
    q_dec = (q.astype(F32) * jnp.exp((row + 1.0) * lg)).astype(BF16)
    o = _dot(scores.astype(BF16), v) + _dot(q_dec, state.astype(BF16))
    k_dec = (k.astype(F32) * jnp.exp((c - 1.0 - row) * lg)).astype(BF16)
    new_state = jnp.exp(c * lg) * state + _dot_tn(k_dec, v)
    return o, new_state


def _retention_out(o, sg):
    on = o * lax.rsqrt(jnp.mean(o * o, axis=-1, keepdims=True) + EPS)
    return on * sg.astype(F32)


def _ret_prompt_kernel(q_ref, k_ref, v_ref, sg_ref, o_ref, st_ref, s_sc, d_sc, qd_sc, kd_sc, *, c):
    h = pl.program_id(1)
    ti = pl.program_id(2)
    tb = q_ref.shape[1]

    @pl.when(ti == 0)
    def _():
        lg = _log_gamma(h)
        s_sc[...] = jnp.zeros(s_sc.shape, F32)
        tt = lax.broadcasted_iota(jnp.int32, (c, c), 0)
        ss = lax.broadcasted_iota(jnp.int32, (c, c), 1)
        dist = (tt - ss).astype(F32)
        d_sc[...] = jnp.where(dist >= 0, jnp.exp(jnp.maximum(dist, 0.0) * lg), 0.0)
        row = lax.broadcasted_iota(jnp.int32, (c, HEAD_W), 0).astype(F32)
        qd_sc[...] = jnp.exp((row + 1.0) * lg)
        kd_sc[...] = jnp.exp((c - 1.0 - row) * lg)

    state = s_sc[...]
    g_c = qd_sc[c - 1:c, :]
    for i in range(tb // c):
        rows = slice(i * c, (i + 1) * c)
        q, k, v = q_ref[0, rows, :], k_ref[0, rows, :], v_ref[0, rows, :]
        scores = _dot_nt(q, k) * d_sc[...]
        q_dec = (q.astype(F32) * qd_sc[...]).astype(BF16)
        o = _dot(scores.astype(BF16), v) + _dot(q_dec, state.astype(BF16))
        k_dec = (k.astype(F32) * kd_sc[...]).astype(BF16)
        state = g_c * state + _dot_tn(k_dec, v)
        o_ref[0, rows, :] = _retention_out(o, sg_ref[0, rows, :]).astype(o_ref.dtype)
    s_sc[...] = state

    @pl.when(ti == pl.num_programs(2) - 1)
    def _():
        st_ref[0, 0] = state


def _ret_prompt(qr, kr, vr, sg, t_blk, c):
    b, s, _ = qr.shape
    assert s % t_blk == 0 and t_blk % c == 0
    blk = pl.BlockSpec((1, t_blk, HEAD_W), lambda bi, h, ti: (bi, ti, h))
    return pl.pallas_call(
        functools.partial(_ret_prompt_kernel, c=c),
        grid=(b, H_R, s // t_blk),
        in_specs=[blk, blk, blk, blk],
        out_specs=[blk, pl.BlockSpec((1, 1, HEAD_W, HEAD_W), lambda bi, h, ti: (bi, h, 0, 0))],
        out_shape=[jax.ShapeDtypeStruct((b, s, SEG_W), BF16),
                   jax.ShapeDtypeStruct((b, H_R, HEAD_W, HEAD_W), F32)],
        scratch_shapes=[pltpu.VMEM((HEAD_W, HEAD_W), F32), pltpu.VMEM((c, c), F32),
                        pltpu.VMEM((c, HEAD_W), F32), pltpu.VMEM((c, HEAD_W), F32)],
        compiler_params=_params(("parallel", "parallel", "arbitrary")),
        name="ret_prompt",
    )(qr, kr, vr, sg)


def _mix_sample_kernel(qa_ref, ka_ref, va_ref, ck_ref, cv_ref, tbl_ref, lq1, lk1, lq2, lk2, gs_ref,
                       qr_ref, kr_ref, vr_ref, sg_ref, st_ref, oa_ref, or_ref, so_ref, *, lam_init, near):
    p_len = ck_ref.shape[2]
    l = qa_ref.shape[1]
    lam = _lambda(lq1, lk1, lq2, lk2, lam_init)
    for h in range(H_A):
        cols = slice(h * HEAD_W, (h + 1) * HEAD_W)
        qq = _stack_maps(qa_ref[0, :, cols])
        kc = ck_ref[0, cols, :].astype(BF16)
        vc = cv_ref[0, pl.ds(h, p_len, stride=H_A), :].astype(BF16)
        bias = tbl_ref[h]
        bias2 = jnp.concatenate([bias, bias], axis=0)
        s_c = _dot(qq, kc)
        s_c = jnp.concatenate([s_c[:, :p_len - near], s_c[:, p_len - near:] + bias2[:, :near]], axis=1)
        s_n = _dot_nt(qq, ka_ref[0, :, cols]) + bias2[:, near:]
        m = jnp.maximum(jnp.max(s_c, axis=-1, keepdims=True), jnp.max(s_n, axis=-1, keepdims=True))
        p_c = jnp.exp2(s_c - m)
        p_n = jnp.exp2(s_n - m)
        lsum = jnp.sum(p_c, axis=-1, keepdims=True) + jnp.sum(p_n, axis=-1, keepdims=True)
        acc = _dot(p_c.astype(BF16), vc) + _dot(p_n.astype(BF16), va_ref[0, :, cols])
        oa_ref[0, :, cols] = _diff_finish(acc, lsum, lam, gs_ref[...], lam_init).astype(oa_ref.dtype)
    for h in range(H_R):
        cols = slice(h * HEAD_W, (h + 1) * HEAD_W)
        lg = jnp.float32(math.log(1.0 - 2.0 ** (-5.0 - h)))
        o, new_state = _retention_chunk(qr_ref[0, :, cols], kr_ref[0, :, cols], vr_ref[0, :, cols],
                                        st_ref[0, h], lg)
        or_ref[0, :, cols] = _retention_out(o, sg_ref[0, :, cols]).astype(or_ref.dtype)
        so_ref[0, h] = new_state


def _mix_sample(qa, ka, va, cache_k, cache_v, tbl, lams, gs, qr, kr, vr, sg, state, lam_init, near):
    b, l, _ = qa.shape
    p_len = cache_k.shape[2]
    assert p_len % CHUNK == 0 and l <= CHUNK and p_len >= near
    tok = pl.BlockSpec((1, l, SEG_W), lambda bi: (bi, 0, 0))
    cache_kt = pl.BlockSpec((1, SEG_W, p_len), lambda bi: (bi, 0, 0))
    cache_vh = pl.BlockSpec((1, p_len * H_A, HEAD_W), lambda bi: (bi, 0, 0))
    st = pl.BlockSpec((1, H_R, HEAD_W, HEAD_W), lambda bi: (bi, 0, 0, 0))
    vec = _resident((1, DK_A))
    return pl.pallas_call(
        functools.partial(_mix_sample_kernel, lam_init=lam_init, near=near),
        grid=(b,),
        in_specs=[tok, tok, tok, cache_kt, cache_vh, _resident(tbl.shape), vec, vec, vec, vec,
                  _resident((1, HEAD_W)), tok, tok, tok, tok, st],
        out_specs=[tok, tok, st],
        out_shape=[jax.ShapeDtypeStruct((b, l, SEG_W), BF16), jax.ShapeDtypeStruct((b, l, SEG_W), BF16),
                   jax.ShapeDtypeStruct((b, H_R, HEAD_W, HEAD_W), F32)],
        compiler_params=_params(("parallel",)),
        name="mix_sample",
    )(qa, ka, va, cache_k, cache_v, tbl, *lams, gs, qr, kr, vr, sg, state)


def _ffn_kernel(x_ref, oa_ref, or_ref, mod_ref, g2_ref, wo_ref, wg_ref, wu_ref, wd_ref, y_ref, *, ff_blk):
    g, r, d = x_ref.shape
    m = g * r
    f = wg_ref.shape[1]
    proj = (_dot(oa_ref[...].reshape(m, SEG_W), wo_ref[:SEG_W, :])
            + _dot(or_ref[...].reshape(m, SEG_W), wo_ref[SEG_W:, :]))
    x1 = x_ref[...] + mod_ref[:, 2:3, :] * proj.reshape(g, r, d)
    gain = g2_ref[...] * (1.0 + mod_ref[:, 4:5, :])
    y = x1 * lax.rsqrt(jnp.mean(x1 * x1, axis=-1, keepdims=True) + EPS) * gain
    h2 = (y + mod_ref[:, 3:4, :]).astype(BF16).reshape(m, d)
    acc = jnp.zeros((m, d), F32)
    for c0 in range(0, f, ff_blk):
        c1 = min(c0 + ff_blk, f)
        ff = (_silu(_dot(h2, wg_ref[:, c0:c1])) * _dot(h2, wu_ref[:, c0:c1])).astype(BF16)
        acc = acc + _dot(ff, wd_ref[c0:c1, :])
    y_ref[...] = x1 + mod_ref[:, 5:6, :] * acc.reshape(g, r, d)


def _ffn_body(ff_blk, stream, shared, outs):
    _ffn_kernel(*stream, *shared, *outs, ff_blk=ff_blk)


def _ffn(xp, oa_p, or_p, mod_p, r_blk, xs, oa_s, or_s, mod_s, g2, wo, wg, wu, wd, ff_blk):
    b, s, d = xp.shape
    bs, l, _ = xs.shape
    assert s % r_blk == 0
    nl = s // r_blk
    n = b * nl
    pj = lambda i: jnp.minimum(i, n - 1)
    tok_p = lambda w: pl.BlockSpec((1, r_blk, w), lambda i: (pj(i) // nl, pj(i) % nl, 0))
    whole = lambda shape: pl.BlockSpec(shape, lambda i: (0,) * len(shape))
    stream_p = [tok_p(d), tok_p(SEG_W), tok_p(SEG_W), pl.BlockSpec((1, 6, d), lambda i: (pj(i) // nl, 0, 0))]
    stream_s = [whole((bs, l, d)), whole((bs, l, SEG_W)), whole((bs, l, SEG_W)), whole((bs, 6, d))]
    shared = [_resident((1, d)), _resident(wo.shape), _resident(wg.shape), _resident(wu.shape), _resident(wd.shape)]
    return pl.pallas_call(
        functools.partial(_tail_step_kernel, functools.partial(_ffn_body, ff_blk), len(stream_p), len(shared), 1),
        grid=(n + 1,),
        in_specs=stream_p + stream_s + shared,
        out_specs=[tok_p(d), whole((bs, l, d))],
        out_shape=[jax.ShapeDtypeStruct((b, s, d), F32), jax.ShapeDtypeStruct((bs, l, d), F32)],
        compiler_params=_params(("arbitrary",)),
        name="outproj_ffn",
    )(xp, oa_p, or_p, mod_p, xs, oa_s, or_s, mod_s, g2, wo, wg, wu, wd)


def _rope_tables(first_pos, n_blocks, blk):
    inv = 1.0 / (ROPE_BASE ** jnp.linspace(0.0, 1.0, HEAD_W // 2, dtype=F32))
    inv2 = jnp.repeat(inv, 2)[None, :]
    even = (jnp.arange(HEAD_W) % 2 == 0)[None, :]
    me, mo = jnp.where(even, -1.0, 0.0), jnp.where(even, 0.0, 1.0)
    t0 = (first_pos + blk * jnp.arange(n_blocks, dtype=jnp.int32)).astype(F32)[:, None] * inv2
    cb, sb = jnp.cos(t0), jnp.sin(t0)
    zero = jnp.zeros_like(cb)
    base = jnp.stack([cb, sb, cb * me, sb * me, cb * mo, sb * mo, zero, zero], axis=1)
    off = jnp.arange(blk, dtype=jnp.int32).astype(F32)[:, None] * inv2
    return base, jnp.cos(off), jnp.sin(off)


def _logit_bound(g_q, g_k, rel_bias):
    slack = 1.02
    dot_max = slack * DK_A * (DK_A ** -0.5 * LOG2E) * jnp.max(jnp.abs(g_q)) * jnp.max(jnp.abs(g_k))
    rb = (rel_bias - rel_bias[FAR_BUCKET][None, :]) * LOG2E
    b_hi = jnp.maximum(jnp.max(rb), 0.0)
    b_lo = jnp.minimum(jnp.min(rb), 0.0)
    ok = (2.0 * dot_max + (b_hi - b_lo)) <= MAX_LOGIT_SPREAD
    return jnp.stack([dot_max + b_hi, ok.astype(F32)])


def _tile_rows(n, target):
    t = min(n, target)
    while n % t:
        t //= 2
    return t


def kernel(x_prompt, x_sample, c_prompt, c_sample, cache_k, cache_v, state_ret, w_ada, b_ada, g_norm1, g_norm2, w_in, g_q, g_k, lam_q1, lam_k1, lam_q2, lam_k2, g_subln, w_out, w_ff_gate, w_ff_up, w_ff_down, rel_bias):
    depth = w_ada.shape[0]
    b, s, d = x_prompt.shape
    bs, l, _ = x_sample.shape
    p_len = cache_k.shape[2]
    assert w_in.shape[2] == N_SEG * SEG_W and d == 2 * SEG_W

    tm = _tile_rows(s, 512)
    tm_in = _tile_rows(s, 1024)
    tq = _tile_rows(s, 4096)
    tk = _tile_rows(tq, 256)
    tkd = _tile_rows(tq, 256)
    tq_exact = _tile_rows(s, 1024)
    ret_blk = _tile_rows(s, 8192)
    ret_c = _tile_rows(ret_blk, 256)
    near = HEAD_W
    assert tq % CHUNK == 0 and tk % CHUNK == 0 and tkd % CHUNK == 0 and near >= BUCKET_THRESHOLDS[-1]

    tab_p = _rope_tables(0, s // tm_in, tm_in)
    tab_s = _rope_tables(p_len, 1, l)
    tbl_p = _bias_table(rel_bias, tkd, near + tkd, near)
    tbl_s = _bias_table(rel_bias, l, near + l, near)
    c_all = jnp.concatenate([c_prompt, c_sample], axis=0)

    xp, xs = x_prompt, x_sample
    outs = [[] for _ in range(6)]
    for layer in range(depth):
        lam_init = _lambda_init(layer)
        mod = _modulation(c_all, w_ada[layer], b_ada[layer]).reshape(b + bs, 6, d)
        mod_p, mod_s = mod[:b], mod[b:]
        g1 = g_norm1[layer].reshape(1, d)
        g2 = g_norm2[layer].reshape(1, d)
        gq_t = jnp.tile(g_q[layer], SEG_W // DK_A).reshape(1, SEG_W)
        gk_t = jnp.tile(g_k[layer], SEG_W // DK_A).reshape(1, SEG_W)
        lams = [v[layer].reshape(1, DK_A) for v in (lam_q1, lam_k1, lam_q2, lam_k2)]
        gs = g_subln[layer].reshape(1, HEAD_W)
        wi = w_in[layer].astype(BF16)
        wo = w_out[layer].astype(BF16)
        wg = w_ff_gate[layer].astype(BF16)
        wu = w_ff_up[layer].astype(BF16)
        wd = w_ff_down[layer].astype(BF16)
        ff_blk = 1024

        (kf, vf, qa, ka, va, qr, kr, vr, sg), (kfs, vfs, qas, kas, vas, qrs, krs, vrs, sgs) = _inproj(
            xp, mod_p, tab_p, tm_in, xs, mod_s, tab_s, g1, wi, gq_t, gk_t)
        bound = _logit_bound(g_q[layer], g_k[layer], rel_bias)
        attn = lambda bounded, tq_, tk_: functools.partial(_attn_prompt, lam_init=lam_init, tq=tq_, tk=tk_, tkd=tkd,
                                                           near=near, bounded=bounded)
        oa = lax.cond(bound[1] > 0.5, attn(True, tq, tk), attn(False, tq_exact, tq_exact),
                      bound, qa, ka, va, tbl_p, lams, gs)
        orr, st_p = _ret_prompt(qr, kr, vr, sg, ret_blk, ret_c)
        ck = jnp.swapaxes(cache_k[layer].reshape(bs, p_len, SEG_W), 1, 2)
        cv = cache_v[layer].reshape(bs, p_len * H_A, HEAD_W)
        oas, ors, st_s = _mix_sample(qas, kas, vas, ck, cv, tbl_s, lams, gs, qrs, krs, vrs, sgs,
                                     state_ret[layer], lam_init, near)
        xp, xs = _ffn(xp, oa, orr, mod_p, tm, xs, oas, ors, mod_s, g2, wo, wg, wu, wd, ff_blk)

        for lst, val in zip(outs, (kf.reshape(b, s, H_A, 2, DK_A), vf.reshape(b, s, H_A, HEAD_W), st_p,
                                   kfs.reshape(bs, l, H_A, 2, DK_A), vfs.reshape(bs, l, H_A, HEAD_W), st_s)):
            lst.append(val)
    return (xp, xs) + tuple(jnp.stack(o) for o in outs)
```

```python
import functools
import math

import jax
import jax.numpy as jnp
from jax import lax
from jax.experimental import pallas as pl
from jax.experimental.pallas import tpu as pltpu

F32 = jnp.float32
BF16 = jnp.bfloat16

CHUNK = 64
H_A = 4
DK_A = 64
H_R = 4
HEAD_W = 128
SEG_W = H_A * HEAD_W
N_SEG = 7
N_BUCKETS = 32
ROPE_BASE = 10000.0
EPS = 1e-6
NEG_INF = -1e30
LOG2E = math.log2(math.e)
BUCKET_THRESHOLDS = (12, 16, 23, 32, 46, 64, 91)
FAR_BUCKET = 15
MAX_LOGIT_SPREAD = 100.0

MXU_W = 256
VMEM_LIMIT = 56 * 1024 * 1024


def _lambda_init(layer):
    return 0.8 - 0.6 * math.exp(-0.3 * layer)


def _params(sem):
    return pltpu.CompilerParams(dimension_semantics=sem, vmem_limit_bytes=VMEM_LIMIT)


def _resident(shape):
    return pl.BlockSpec(shape, lambda *_: (0,) * len(shape), pipeline_mode=pl.Buffered(1))


def _dot(a, b):
    return jnp.dot(a, b, preferred_element_type=F32)


def _dot_nt(a, b):
    return lax.dot_general(a, b, (((1,), (1,)), ((), ())), preferred_element_type=F32)


def _dot_tn(a, b):
    return lax.dot_general(a, b, (((0,), (0,)), ((), ())), preferred_element_type=F32)


def _silu(x):
    return x * jax.nn.sigmoid(x)


def _mod_kernel(c_ref, w_ref, b_ref, o_ref):
    s = _silu(c_ref[...]).astype(BF16)
    o_ref[...] = _dot(s, w_ref[...].astype(BF16)) + b_ref[...]


def _modulation(c_all, w_ada, b_ada):
    n, d = c_all.shape
    e = w_ada.shape[1]
    te = 1536
    assert e % te == 0
    return pl.pallas_call(
        _mod_kernel,
        grid=(e // te,),
        in_specs=[pl.BlockSpec((n, d), lambda j: (0, 0)),
                  pl.BlockSpec((d, te), lambda j: (0, j)),
                  pl.BlockSpec((1, te), lambda j: (0, j))],
        out_specs=pl.BlockSpec((n, te), lambda j: (0, j)),
        out_shape=jax.ShapeDtypeStruct((n, e), F32),
        compiler_params=_params(("parallel",)),
        name="modulation",
    )(c_all, w_ada, b_ada.reshape(1, e))


def _bias_table_kernel(rb_ref, o_ref, *, offset):
    h = pl.program_id(0)
    _, rows, cols = o_ref.shape
    r = lax.broadcasted_iota(jnp.int32, (rows, cols), 0)
    kc = lax.broadcasted_iota(jnp.int32, (rows, cols), 1) - offset
    rel = kc - r
    n = jnp.abs(rel)
    large = jnp.full((rows, cols), N_BUCKETS // 4, jnp.int32)
    for thr in BUCKET_THRESHOLDS:
        large = large + (n >= thr).astype(jnp.int32)
    bucket = jnp.where(rel > 0, N_BUCKETS // 2, 0) + jnp.where(n < N_BUCKETS // 4, n, large)
    val = jnp.zeros((rows, cols), F32)
    for b in range(N_BUCKETS):
        val = jnp.where(bucket == b, rb_ref[b, h], val)
    val = (val - rb_ref[FAR_BUCKET, h]) * LOG2E
    shift = int(math.log2(CHUNK))
    visible = lax.shift_right_arithmetic(kc, shift) <= lax.shift_right_arithmetic(r, shift)
    o_ref[0] = jnp.where(visible, val, NEG_INF)


def _bias_table(rel_bias, rows, cols, offset):
    assert offset >= BUCKET_THRESHOLDS[-1] and offset % CHUNK == 0
    return pl.pallas_call(
        functools.partial(_bias_table_kernel, offset=offset),
        grid=(H_A,),
        in_specs=[pl.BlockSpec(memory_space=pltpu.SMEM)],
        out_specs=pl.BlockSpec((1, rows, cols), lambda h: (h, 0, 0)),
        out_shape=jax.ShapeDtypeStruct((H_A, rows, cols), F32),
        compiler_params=_params(("parallel",)),
        name="bias_table",
    )(rel_bias)


def _inproj_kernel(x_ref, mod_ref, g1_ref, w_ref, gq_ref, gk_ref, rope_ref, cr_ref, sr_ref,
                   kf_ref, vf_ref, qa_ref, ka_ref, va_ref, qr_ref, kr_ref, vr_ref, sg_ref):
    g, r, d = x_ref.shape
    m = g * r
    x = x_ref[...]
    gain = g1_ref[...] * (1.0 + mod_ref[:, 1:2, :])
    hmod = x * lax.rsqrt(jnp.mean(x * x, axis=-1, keepdims=True) + EPS) * gain + mod_ref[:, 0:1, :]
    hb = hmod.astype(BF16).reshape(m, d)

    def seg(j):
        return _dot(hb, w_ref[:, j * SEG_W:(j + 1) * SEG_W])

    def put(ref, val):
        ref[...] = val.reshape(g, r, SEG_W).astype(ref.dtype)

    gi = lax.broadcasted_iota(jnp.int32, (MXU_W, MXU_W), 0) // DK_A
    gj = lax.broadcasted_iota(jnp.int32, (MXU_W, MXU_W), 1) // DK_A
    ones_bd = (gi == gj).astype(BF16)

    def group_rms(z):
        sq = (z * z).astype(BF16)
        ms = jnp.concatenate([_dot(sq[:, c:c + MXU_W], ones_bd) for c in range(0, SEG_W, MXU_W)], axis=1)
        return z * lax.rsqrt(ms * (1.0 / DK_A) + EPS)

    cr, sr = cr_ref[...], sr_ref[...]

    def table(c_row, s_row, sign):
        t = rope_ref[0, c_row:c_row + 1, :] * cr + sign * (rope_ref[0, s_row:s_row + 1, :] * sr)
        return jnp.broadcast_to(t[None], (g, r, HEAD_W)).reshape(m, HEAD_W)

    cos, sine, sino = table(0, 1, -1.0), table(3, 2, 1.0), table(5, 4, 1.0)

    def rotate(z):
        outs = []
        for hh in range(H_R):
            zh = z[:, hh * HEAD_W:(hh + 1) * HEAD_W]
            nxt = pltpu.roll(zh, HEAD_W - 1, axis=1)
            prv = pltpu.roll(zh, 1, axis=1)
            outs.append(zh * cos + nxt * sine + prv * sino)
        return jnp.concatenate(outs, axis=1)

    put(qa_ref, group_rms(seg(0)) * gq_ref[...] * (DK_A ** -0.5 * LOG2E))
    ka = group_rms(seg(1)) * gk_ref[...]
    put(kf_ref, ka)
    put(ka_ref, ka)
    put(qr_ref, rotate(seg(3)))
    put(kr_ref, rotate(seg(4)) * (HEAD_W ** -0.5))
    put(sg_ref, _silu(seg(6)))
    va = seg(2)
    for hh in range(H_A):
        vf_ref[:, pl.ds(hh, r, stride=H_A), :] = va[:, hh * HEAD_W:(hh + 1) * HEAD_W].reshape(g, r, HEAD_W)
    put(va_ref, va)
    put(vr_ref, seg(5))


N_INPROJ_OUT = 9


def _tail_step_kernel(body, n_a, n_shared, n_out, *refs):
    a_in, b_in = refs[:n_a], refs[n_a:2 * n_a]
    shared = refs[2 * n_a:2 * n_a + n_shared]
    outs = refs[2 * n_a + n_shared:]
    a_out, b_out = outs[:n_out], outs[n_out:]
    last = pl.num_programs(0) - 1

    @pl.when(pl.program_id(0) < last)
    def _():
        body(a_in, shared, a_out)

    @pl.when(pl.program_id(0) == last)
    def _():
        body(b_in, shared, b_out)


def _inproj_body(stream, shared, outs):
    x_ref, mod_ref, rope_ref, cr_ref, sr_ref = stream
    g1_ref, w_ref, gq_ref, gk_ref = shared
    _inproj_kernel(x_ref, mod_ref, g1_ref, w_ref, gq_ref, gk_ref, rope_ref, cr_ref, sr_ref, *outs)


def _inproj(xp, mod_p, tab_p, r_blk, xs, mod_s, tab_s, g1, w_in, gq_t, gk_t):
    b, s, d = xp.shape
    bs, l, _ = xs.shape
    assert s % r_blk == 0
    nl = s // r_blk
    n = b * nl
    pj = lambda i: jnp.minimum(i, n - 1)
    tok_p = lambda w, rows=r_blk: pl.BlockSpec((1, rows, w), lambda i: (pj(i) // nl, pj(i) % nl, 0))
    whole = lambda shape: pl.BlockSpec(shape, lambda i: (0,) * len(shape))
    stream_p = [tok_p(d), pl.BlockSpec((1, 6, d), lambda i: (pj(i) // nl, 0, 0)),
                pl.BlockSpec((1, 8, HEAD_W), lambda i: (pj(i) % nl, 0, 0)),
                _resident((r_blk, HEAD_W)), _resident((r_blk, HEAD_W))]
    stream_s = [whole((bs, l, d)), whole((bs, 6, d)), whole((1, 8, HEAD_W)), whole((l, HEAD_W)), whole((l, HEAD_W))]
    shared = [_resident((1, d)), _resident(w_in.shape), _resident((1, SEG_W)), _resident((1, SEG_W))]
    out_p = [tok_p(SEG_W), tok_p(HEAD_W, r_blk * H_A)] + [tok_p(SEG_W)] * 7
    out_s = [whole((bs, l, SEG_W)), whole((bs, l * H_A, HEAD_W))] + [whole((bs, l, SEG_W))] * 7
    shapes = lambda nb, t: ([jax.ShapeDtypeStruct((nb, t, SEG_W), F32), jax.ShapeDtypeStruct((nb, t * H_A, HEAD_W), F32)]
                            + [jax.ShapeDtypeStruct((nb, t, SEG_W), BF16)] * 7)
    res = pl.pallas_call(
        functools.partial(_tail_step_kernel, _inproj_body, len(stream_p), len(shared), N_INPROJ_OUT),
        grid=(n + 1,),
        in_specs=stream_p + stream_s + shared,
        out_specs=out_p + out_s,
        out_shape=shapes(b, s) + shapes(bs, l),
        compiler_params=_params(("arbitrary",)),
        name="inproj",
    )(xp, mod_p, *tab_p, xs, mod_s, *tab_s, g1, w_in, gq_t, gk_t)
    return res[:N_INPROJ_OUT], res[N_INPROJ_OUT:]


def _stack_maps(q):
    lane = lax.broadcasted_iota(jnp.int32, q.shape, 1)
    zero = jnp.zeros_like(q)
    return jnp.concatenate([jnp.where(lane < DK_A, q, zero), jnp.where(lane >= DK_A, q, zero)], axis=0)


def _lambda(lq1, lk1, lq2, lk2, lam_init):
    s1 = jnp.sum(lq1[...] * lk1[...], axis=-1, keepdims=True)
    s2 = jnp.sum(lq2[...] * lk2[...], axis=-1, keepdims=True)
    return jnp.exp(s1) - jnp.exp(s2) + lam_init


def _diff_finish(acc, l, lam, gs, lam_init):
    t = acc.shape[0] // 2
    o = acc[:t] * (1.0 / l[:t]) - lam * (acc[t:] * (1.0 / l[t:]))
    on = o * lax.rsqrt(jnp.mean(o * o, axis=-1, keepdims=True) + EPS)
    return on * gs * (1.0 - lam_init)


def _attn_prompt_kernel(bound_ref, q_ref, k_ref, v_ref, tbl_ref, lq1, lk1, lq2, lk2, gs_ref, o_ref,
                        qq_sc, m_sc, l_sc, acc_sc, *, lam_init, tk, tkd, near, bounded):
    qi = pl.program_id(2)
    tq = q_ref.shape[1]
    q = q_ref[0]
    lane = lax.broadcasted_iota(jnp.int32, q.shape, 1)
    zero = jnp.zeros_like(q)
    qq_sc[0] = jnp.where(lane < DK_A, q, zero)
    qq_sc[1] = jnp.where(lane >= DK_A, q, zero)
    if bounded:
        va_sc = m_sc

        @pl.when(qi == 0)
        def _():
            va_sc[:, :HEAD_W] = v_ref[0]
            va_sc[:, HEAD_W:] = jnp.ones((va_sc.shape[0], HEAD_W), BF16)

    def tile(start, width, row0, corner, diag, bounded, first=False):
        nr = tq - row0
        nblk = width // HEAD_W
        rows = slice(row0, tq)
        flat = lambda x: x.reshape(2 * nr, x.shape[-1])
        kt = k_ref[0, pl.ds(start, width), :]
        vt = va_sc[pl.ds(start, width), :] if bounded else v_ref[0, pl.ds(start, width), :]
        s = _dot_nt(flat(qq_sc[:, rows, :]), kt)
        blocks = [s[:, c * HEAD_W:(c + 1) * HEAD_W] for c in range(nblk)]

        def add_rows(blk, r0, val):
            r1 = r0 + val.shape[0]
            cuts = [(0, r0, False), (r0, r1, True), (r1, nr + r0, False), (nr + r0, nr + r1, True),
                    (nr + r1, 2 * nr, False)]
            return jnp.concatenate([blk[a:b] + val if hit else blk[a:b] for a, b, hit in cuts if b > a], axis=0)

        if diag:
            blocks = [add_rows(blk, 0, tbl_ref[0, :, near + c * HEAD_W:near + (c + 1) * HEAD_W])
                      for c, blk in enumerate(blocks)]
        corner_row0 = width if diag else 0
        if corner_row0 < nr:
            blocks[-1] = add_rows(blocks[-1], corner_row0, corner)
        if bounded:
            ps = [jnp.exp2(blk - bound_ref[0]) for blk in blocks]
            p = jnp.concatenate([pc.astype(BF16) for pc in ps], axis=1)
            acc_new = _dot(p, vt)
            if not first:
                acc_new = flat(acc_sc[:, rows, :]) + acc_new
            acc_sc[:, rows, :] = acc_new.reshape(2, nr, 2 * HEAD_W)
            return
        m_old = flat(m_sc[:, rows, :])
        mx = functools.reduce(jnp.maximum, blocks)
        m_new = jnp.maximum(m_old, jnp.max(mx, axis=-1, keepdims=True))
        alpha = jnp.exp2(m_old - m_new)
        ps = [jnp.exp2(blk - m_new) for blk in blocks]
        l_new = alpha * flat(l_sc[:, rows, :]) + functools.reduce(jnp.add, ps)
        p = jnp.concatenate([pc.astype(BF16) for pc in ps], axis=1)
        acc_new = alpha * flat(acc_sc[:, rows, :]) + _dot(p, vt)
        m_sc[:, rows, :] = m_new.reshape(2, nr, HEAD_W)
        l_sc[:, rows, :] = l_new.reshape(2, nr, HEAD_W)
        acc_sc[:, rows, :] = acc_new.reshape(2, nr, HEAD_W)

    n_far = qi * (tq // tk)
    corner_tbl = tbl_ref[0, :near, :near]

    def diag_tiles(bounded, first):
        for d in range(tq // tkd):
            tile(pl.multiple_of(qi * tq + d * tkd, tkd), tkd, d * tkd, corner_tbl, True, bounded, first and d == 0)

    def run(bounded):
        if bounded:
            diag_tiles(True, True)

        def far_tile(j):
            flag = (j == n_far - 1).astype(F32)
            tile(pl.multiple_of(j * tk, tk), tk, 0, corner_tbl * flag, False, bounded)

        rem = jnp.bitwise_and(n_far, 3)

        def peel(j, carry):
            far_tile(j)
            return carry

        lax.fori_loop(0, rem, peel, 0)

        def far_quad(i, carry):
            for u in range(4):
                far_tile(rem + 4 * i + u)
            return carry

        lax.fori_loop(0, lax.shift_right_logical(n_far, 2), far_quad, 0)
        if not bounded:
            diag_tiles(False, False)

    if not bounded:
        m_sc[...] = jnp.full(m_sc.shape, NEG_INF, F32)
        l_sc[...] = jnp.zeros(l_sc.shape, F32)
        acc_sc[...] = jnp.zeros(acc_sc.shape, F32)
    run(bounded)

    lam = _lambda(lq1, lk1, lq2, lk2, lam_init)
    if bounded:
        l = acc_sc[:, :, HEAD_W:].reshape(2 * tq, HEAD_W)
        acc = acc_sc[:, :, :HEAD_W].reshape(2 * tq, HEAD_W)
    else:
        l = jnp.sum(l_sc[...], axis=-1, keepdims=True).reshape(2 * tq, 1)
        acc = acc_sc[...].reshape(2 * tq, HEAD_W)
    o_ref[0] = _diff_finish(acc, l, lam, gs_ref[...], lam_init).astype(o_ref.dtype)


def _attn_prompt(bound, q, k, v, tbl, lams, gs, lam_init, tq, tk, tkd, near, bounded):
    b, s, _ = q.shape
    assert s % tq == 0 and tq % tk == 0 and tq % tkd == 0 and tbl.shape == (H_A, tkd, near + tkd)
    assert near == HEAD_W and tk % HEAD_W == 0 and tkd % HEAD_W == 0
    vec = _resident((1, DK_A))
    return pl.pallas_call(
        functools.partial(_attn_prompt_kernel, lam_init=lam_init, tk=tk, tkd=tkd, near=near, bounded=bounded),
        grid=(b, H_A, s // tq),
        in_specs=[pl.BlockSpec(memory_space=pltpu.SMEM),
                  pl.BlockSpec((1, tq, HEAD_W), lambda bi, h, qi: (bi, qi, h)),
                  pl.BlockSpec((1, s, HEAD_W), lambda bi, h, qi: (bi, 0, h)),
                  pl.BlockSpec((1, s, HEAD_W), lambda bi, h, qi: (bi, 0, h)),
                  pl.BlockSpec((1, tkd, near + tkd), lambda bi, h, qi: (h, 0, 0)),
                  vec, vec, vec, vec, _resident((1, HEAD_W))],
        out_specs=pl.BlockSpec((1, tq, HEAD_W), lambda bi, h, qi: (bi, qi, h)),
        out_shape=jax.ShapeDtypeStruct((b, s, SEG_W), BF16),
        scratch_shapes=([pltpu.VMEM((2, tq, HEAD_W), BF16), pltpu.VMEM((s, 2 * HEAD_W), BF16),
                         pltpu.VMEM((8, HEAD_W), F32), pltpu.VMEM((2, tq, 2 * HEAD_W), F32)] if bounded else
                        [pltpu.VMEM((2, tq, HEAD_W), BF16)] + [pltpu.VMEM((2, tq, HEAD_W), F32)] * 3),
        compiler_params=_params(("parallel", "parallel", "arbitrary")),
        name="attn_prompt_bounded" if bounded else "attn_prompt_exact",
    )(bound, q, k, v, tbl, *lams, gs)


def _log_gamma(h):
    vals = [math.log(1.0 - 2.0 ** (-5.0 - i)) for i in range(H_R)]
    lg = jnp.float32(vals[H_R - 1])
    for i in range(H_R - 2, -1, -1):
        lg = jnp.where(h == i, jnp.float32(vals[i]), lg)
    return lg


def _retention_chunk(q, k, v, state, lg):
    c = q.shape[0]
    ti = lax.broadcasted_iota(jnp.int32, (c, c), 0)
    si = lax.broadcasted_iota(jnp.int32, (c, c), 1)
    dist = (ti - si).astype(F32)
    decay = jnp.where(dist >= 0, jnp.exp(jnp.maximum(dist, 0.0) * lg), 0.0)
    scores = _dot_nt(q, k) * decay
    row = lax.broadcasted_iota(jnp.int32, (c, 1), 0).astype(F32)
    q_dec = (q.astype(F32) * jnp.exp((row + 1.0) * lg)).astype(BF16)
    o = _dot(scores.astype(BF16), v) + _dot(q_dec, state.astype(BF16))
    k_dec = (k.astype(F32) * jnp.exp((c - 1.0 - row) * lg)).astype(BF16)
    new_state = jnp.exp(c * lg) * state + _dot_tn(k_dec, v)
    return o, new_state


def _retention_out(o, sg):
    on = o * lax.rsqrt(jnp.mean(o * o, axis=-1, keepdims=True) + EPS)
    return on * sg.astype(F32)


def _ret_prompt_kernel(q_ref, k_ref, v_ref, sg_ref, o_ref, st_ref, s_sc, d_sc, qd_sc, kd_sc, *, c):
    h = pl.program_id(1)
    ti = pl.program_id(2)
    tb = q_ref.shape[1]

    @pl.when(ti == 0)
    def _():
        lg = _log_gamma(h)
        s_sc[...] = jnp.zeros(s_sc.shape, F32)
        tt = lax.broadcasted_iota(jnp.int32, (c, c), 0)
        ss = lax.broadcasted_iota(jnp.int32, (c, c), 1)
        dist = (tt - ss).astype(F32)
        d_sc[...] = jnp.where(dist >= 0, jnp.exp(jnp.maximum(dist, 0.0) * lg), 0.0)
        row = lax.broadcasted_iota(jnp.int32, (c, HEAD_W), 0).astype(F32)
        qd_sc[...] = jnp.exp((row + 1.0) * lg)
        kd_sc[...] = jnp.exp((c - 1.0 - row) * lg)

    state = s_sc[...]
    g_c = qd_sc[c - 1:c, :]
    for i in range(tb // c):
        rows = slice(i * c, (i + 1) * c)
        q, k, v = q_ref[0, rows, :], k_ref[0, rows, :], v_ref[0, rows, :]
        scores = _dot_nt(q, k) * d_sc[...]
        q_dec = (q.astype(F32) * qd_sc[...]).astype(BF16)
        o = _dot(scores.astype(BF16), v) + _dot(q_dec, state.astype(BF16))
        k_dec = (k.astype(F32) * kd_sc[...]).astype(BF16)
        state = g_c * state + _dot_tn(k_dec, v)
        o_ref[0, rows, :] = _retention_out(o, sg_ref[0, rows, :]).astype(o_ref.dtype)
    s_sc[...] = state

    @pl.when(ti == pl.num_programs(2) - 1)
    def _():
        st_ref[0, 0] = state


def _ret_prompt(qr, kr, vr, sg, t_blk, c):
    b, s, _ = qr.shape
    assert s % t_blk == 0 and t_blk % c == 0
    blk = pl.BlockSpec((1, t_blk, HEAD_W), lambda bi, h, ti: (bi, ti, h))
    return pl.pallas_call(
        functools.partial(_ret_prompt_kernel, c=c),
        grid=(b, H_R, s // t_blk),
        in_specs=[blk, blk, blk, blk],
        out_specs=[blk, pl.BlockSpec((1, 1, HEAD_W, HEAD_W), lambda bi, h, ti: (bi, h, 0, 0))],
        out_shape=[jax.ShapeDtypeStruct((b, s, SEG_W), BF16),
                   jax.ShapeDtypeStruct((b, H_R, HEAD_W, HEAD_W), F32)],
        scratch_shapes=[pltpu.VMEM((HEAD_W, HEAD_W), F32), pltpu.VMEM((c, c), F32),
                        pltpu.VMEM((c, HEAD_W), F32), pltpu.VMEM((c, HEAD_W), F32)],
        compiler_params=_params(("parallel", "parallel", "arbitrary")),
        name="ret_prompt",
    )(qr, kr, vr, sg)


def _mix_sample_kernel(qa_ref, ka_ref, va_ref, ck_ref, cv_ref, tbl_ref, lq1, lk1, lq2, lk2, gs_ref,
                       qr_ref, kr_ref, vr_ref, sg_ref, st_ref, oa_ref, or_ref, so_ref, *, lam_init, near):
    p_len = ck_ref.shape[2]
    l = qa_ref.shape[1]
    lam = _lambda(lq1, lk1, lq2, lk2, lam_init)
    for h in range(H_A):
        cols = slice(h * HEAD_W, (h + 1) * HEAD_W)
        qq = _stack_maps(qa_ref[0, :, cols])
        kc = ck_ref[0, cols, :].astype(BF16)
        vc = cv_ref[0, pl.ds(h, p_len, stride=H_A), :].astype(BF16)
        bias = tbl_ref[h]
        bias2 = jnp.concatenate([bias, bias], axis=0)
        s_c = _dot(qq, kc)
        s_c = jnp.concatenate([s_c[:, :p_len - near], s_c[:, p_len - near:] + bias2[:, :near]], axis=1)
        s_n = _dot_nt(qq, ka_ref[0, :, cols]) + bias2[:, near:]
        m = jnp.maximum(jnp.max(s_c, axis=-1, keepdims=True), jnp.max(s_n, axis=-1, keepdims=True))
        p_c = jnp.exp2(s_c - m)
        p_n = jnp.exp2(s_n - m)
        lsum = jnp.sum(p_c, axis=-1, keepdims=True) + jnp.sum(p_n, axis=-1, keepdims=True)
        acc = _dot(p_c.astype(BF16), vc) + _dot(p_n.astype(BF16), va_ref[0, :, cols])
        oa_ref[0, :, cols] = _diff_finish(acc, lsum, lam, gs_ref[...], lam_init).astype(oa_ref.dtype)
    for h in range(H_R):
        cols = slice(h * HEAD_W, (h + 1) * HEAD_W)
        lg = jnp.float32(math.log(1.0 - 2.0 ** (-5.0 - h)))
        o, new_state = _retention_chunk(qr_ref[0, :, cols], kr_ref[0, :, cols], vr_ref[0, :, cols],
                                        st_ref[0, h], lg)
        or_ref[0, :, cols] = _retention_out(o, sg_ref[0, :, cols]).astype(or_ref.dtype)
        so_ref[0, h] = new_state


def _mix_sample(qa, ka, va, cache_k, cache_v, tbl, lams, gs, qr, kr, vr, sg, state, lam_init, near):
    b, l, _ = qa.shape
    p_len = cache_k.shape[2]
    assert p_len % CHUNK == 0 and l <= CHUNK and p_len >= near
    tok = pl.BlockSpec((1, l, SEG_W), lambda bi: (bi, 0, 0))
    cache_kt = pl.BlockSpec((1, SEG_W, p_len), lambda bi: (bi, 0, 0))
    cache_vh = pl.BlockSpec((1, p_len * H_A, HEAD_W), lambda bi: (bi, 0, 0))
    st = pl.BlockSpec((1, H_R, HEAD_W, HEAD_W), lambda bi: (bi, 0, 0, 0))
    vec = _resident((1, DK_A))
    return pl.pallas_call(
        functools.partial(_mix_sample_kernel, lam_init=lam_init, near=near),
        grid=(b,),
        in_specs=[tok, tok, tok, cache_kt, cache_vh, _resident(tbl.shape), vec, vec, vec, vec,
                  _resident((1, HEAD_W)), tok, tok, tok, tok, st],
        out_specs=[tok, tok, st],
        out_shape=[jax.ShapeDtypeStruct((b, l, SEG_W), BF16), jax.ShapeDtypeStruct((b, l, SEG_W), BF16),
                   jax.ShapeDtypeStruct((b, H_R, HEAD_W, HEAD_W), F32)],
        compiler_params=_params(("parallel",)),
        name="mix_sample",
    )(qa, ka, va, cache_k, cache_v, tbl, *lams, gs, qr, kr, vr, sg, state)


def _ffn_kernel(x_ref, oa_ref, or_ref, mod_ref, g2_ref, wo_ref, wg_ref, wu_ref, wd_ref, y_ref, *, ff_blk):
    g, r, d = x_ref.shape
    m = g * r
    f = wg_ref.shape[1]
    proj = (_dot(oa_ref[...].reshape(m, SEG_W), wo_ref[:SEG_W, :])
            + _dot(or_ref[...].reshape(m, SEG_W), wo_ref[SEG_W:, :]))
    x1 = x_ref[...] + mod_ref[:, 2:3, :] * proj.reshape(g, r, d)
    gain = g2_ref[...] * (1.0 + mod_ref[:, 4:5, :])
    y = x1 * lax.rsqrt(jnp.mean(x1 * x1, axis=-1, keepdims=True) + EPS) * gain
    h2 = (y + mod_ref[:, 3:4, :]).astype(BF16).reshape(m, d)
    acc = jnp.zeros((m, d), F32)
    for c0 in range(0, f, ff_blk):
        c1 = min(c0 + ff_blk, f)
        ff = (_silu(_dot(h2, wg_ref[:, c0:c1])) * _dot(h2, wu_ref[:, c0:c1])).astype(BF16)
        acc = acc + _dot(ff, wd_ref[c0:c1, :])
    y_ref[...] = x1 + mod_ref[:, 5:6, :] * acc.reshape(g, r, d)


def _ffn_body(ff_blk, stream, shared, outs):
    _ffn_kernel(*stream, *shared, *outs, ff_blk=ff_blk)


def _ffn(xp, oa_p, or_p, mod_p, r_blk, xs, oa_s, or_s, mod_s, g2, wo, wg, wu, wd, ff_blk):
    b, s, d = xp.shape
    bs, l, _ = xs.shape
    assert s % r_blk == 0
    nl = s // r_blk
    n = b * nl
    pj = lambda i: jnp.minimum(i, n - 1)
    tok_p = lambda w: pl.BlockSpec((1, r_blk, w), lambda i: (pj(i) // nl, pj(i) % nl, 0))
    whole = lambda shape: pl.BlockSpec(shape, lambda i: (0,) * len(shape))
    stream_p = [tok_p(d), tok_p(SEG_W), tok_p(SEG_W), pl.BlockSpec((1, 6, d), lambda i: (pj(i) // nl, 0, 0))]
    stream_s = [whole((bs, l, d)), whole((bs, l, SEG_W)), whole((bs, l, SEG_W)), whole((bs, 6, d))]
    shared = [_resident((1, d)), _resident(wo.shape), _resident(wg.shape), _resident(wu.shape), _resident(wd.shape)]
    return pl.pallas_call(
        functools.partial(_tail_step_kernel, functools.partial(_ffn_body, ff_blk), len(stream_p), len(shared), 1),
        grid=(n + 1,),
        in_specs=stream_p + stream_s + shared,
        out_specs=[tok_p(d), whole((bs, l, d))],
        out_shape=[jax.ShapeDtypeStruct((b, s, d), F32), jax.ShapeDtypeStruct((bs, l, d), F32)],
        compiler_params=_params(("arbitrary",)),
        name="outproj_ffn",
    )(xp, oa_p, or_p, mod_p, xs, oa_s, or_s, mod_s, g2, wo, wg, wu, wd)


def _rope_tables(first_pos, n_blocks, blk):
    inv = 1.0 / (ROPE_BASE ** jnp.linspace(0.0, 1.0, HEAD_W // 2, dtype=F32))
    inv2 = jnp.repeat(inv, 2)[None, :]
    even = (jnp.arange(HEAD_W) % 2 == 0)[None, :]
    me, mo = jnp.where(even, -1.0, 0.0), jnp.where(even, 0.0, 1.0)
    t0 = (first_pos + blk * jnp.arange(n_blocks, dtype=jnp.int32)).astype(F32)[:, None] * inv2
    cb, sb = jnp.cos(t0), jnp.sin(t0)
    zero = jnp.zeros_like(cb)
    base = jnp.stack([cb, sb, cb * me, sb * me, cb * mo, sb * mo, zero, zero], axis=1)
    off = jnp.arange(blk, dtype=jnp.int32).astype(F32)[:, None] * inv2
    return base, jnp.cos(off), jnp.sin(off)


def _logit_bound(g_q, g_k, rel_bias):
    slack = 1.02
    dot_max = slack * DK_A * (DK_A ** -0.5 * LOG2E) * jnp.max(jnp.abs(g_q)) * jnp.max(jnp.abs(g_k))
    rb = (rel_bias - rel_bias[FAR_BUCKET][None, :]) * LOG2E
    b_hi = jnp.maximum(jnp.max(rb), 0.0)
    b_lo = jnp.minimum(jnp.min(rb), 0.0)
    ok = (2.0 * dot_max + (b_hi - b_lo)) <= MAX_LOGIT_SPREAD
    return jnp.stack([dot_max + b_hi, ok.astype(F32)])


def _tile_rows(n, target):
    t = min(n, target)
    while n % t:
        t //= 2
    return t


def kernel(x_prompt, x_sample, c_prompt, c_sample, cache_k, cache_v, state_ret, w_ada, b_ada, g_norm1, g_norm2, w_in, g_q, g_k, lam_q1, lam_k1, lam_q2, lam_k2, g_subln, w_out, w_ff_gate, w_ff_up, w_ff_down, rel_bias):
    depth = w_ada.shape[0]
    b, s, d = x_prompt.shape
    bs, l, _ = x_sample.shape
    p_len = cache_k.shape[2]
    assert w_in.shape[2] == N_SEG * SEG_W and d == 2 * SEG_W

    tm = _tile_rows(s, 512)
    tm_in = _tile_rows(s, 1024)
    tq = _tile_rows(s, 4096)
    tk = _tile_rows(tq, 256)
    tkd = _tile_rows(tq, 256)
    ret_blk = _tile_rows(s, 8192)
    ret_c = _tile_rows(ret_blk, 256)
    near = HEAD_W
    assert tq % CHUNK == 0 and tk % CHUNK == 0 and tkd % CHUNK == 0 and near >= BUCKET_THRESHOLDS[-1]

    tab_p = _rope_tables(0, s // tm_in, tm_in)
    tab_s = _rope_tables(p_len, 1, l)
    tbl_p = _bias_table(rel_bias, tkd, near + tkd, near)
    tbl_s = _bias_table(rel_bias, l, near + l, near)
    c_all = jnp.concatenate([c_prompt, c_sample], axis=0)

    xp, xs = x_prompt, x_sample
    outs = [[] for _ in range(6)]
    for layer in range(depth):
        lam_init = _lambda_init(layer)
        mod = _modulation(c_all, w_ada[layer], b_ada[layer]).reshape(b + bs, 6, d)
        mod_p, mod_s = mod[:b], mod[b:]
        g1 = g_norm1[layer].reshape(1, d)
        g2 = g_norm2[layer].reshape(1, d)
        gq_t = jnp.tile(g_q[layer], SEG_W // DK_A).reshape(1, SEG_W)
        gk_t = jnp.tile(g_k[layer], SEG_W // DK_A).reshape(1, SEG_W)
        lams = [v[layer].reshape(1, DK_A) for v in (lam_q1, lam_k1, lam_q2, lam_k2)]
        gs = g_subln[layer].reshape(1, HEAD_W)
        wi = w_in[layer].astype(BF16)
        wo = w_out[layer].astype(BF16)
        wg = w_ff_gate[layer].astype(BF16)
        wu = w_ff_up[layer].astype(BF16)
        wd = w_ff_down[layer].astype(BF16)
        ff_blk = 1024

        (kf, vf, qa, ka, va, qr, kr, vr, sg), (kfs, vfs, qas, kas, vas, qrs, krs, vrs, sgs) = _inproj(
            xp, mod_p, tab_p, tm_in, xs, mod_s, tab_s, g1, wi, gq_t, gk_t)
        bound = _logit_bound(g_q[layer], g_k[layer], rel_bias)
        attn = lambda bounded: functools.partial(_attn_prompt, lam_init=lam_init, tq=tq, tk=tk, tkd=tkd,
                                                 near=near, bounded=bounded)
        oa = lax.cond(bound[1] > 0.5, attn(True), attn(False), bound, qa, ka, va, tbl_p, lams, gs)
        orr, st_p = _ret_prompt(qr, kr, vr, sg, ret_blk, ret_c)
        ck = jnp.swapaxes(cache_k[layer].reshape(bs, p_len, SEG_W), 1, 2)
        cv = cache_v[layer].reshape(bs, p_len * H_A, HEAD_W)
        oas, ors, st_s = _mix_sample(qas, kas, vas, ck, cv, tbl_s, lams, gs, qrs, krs, vrs, sgs,
                                     state_ret[layer], lam_init, near)
        xp, xs = _ffn(xp, oa, orr, mod_p, tm, xs, oas, ors, mod_s, g2, wo, wg, wu, wd, ff_blk)

        for lst, val in zip(outs, (kf.reshape(b, s, H_A, 2, DK_A), vf.reshape(b, s, H_A, HEAD_W), st_p,
                                   kfs.reshape(bs, l, H_A, 2, DK_A), vfs.reshape(bs, l, H_A, HEAD_W), st_s)):
            lst.append(val)
    return (xp, xs) + tuple(jnp.stack(o) for o in outs)
```
